```python
import jax
import jax.numpy as jnp
from jax import lax
import numpy as np

D_MODEL = 2048
BATCH = 2
SEQ = 4096
DEPTH = 1
DEC_BATCH = 128
DEC_SEQ = 8
PAST_LEN = 2048
PAGE_SIZE = 128

NSA_HEADS = 8
NSA_GROUPS = 2
HEADS_PER_GROUP = NSA_HEADS // NSA_GROUPS
HEAD_DIM = 128
CMP_LEN = 32
CMP_STRIDE = 16
CMP_HIDDEN = 2 * HEAD_DIM
SEL_BLK = 64
SEL_TOPK = 16
WINDOW = 512
Q_BLK = 128
RET_HEADS = 8
RET_DK = 128
RET_DV = 256
RET_CHUNK = 128
ROPE_BASE = 10000.0
N_EXPERTS = 32
TOP_K = 4
D_FF = D_MODEL
SWIGLU_LIMIT = 7.0
SWIGLU_ALPHA = 1.702
MOE_BLK = 128

EPS = 1e-6
NEG = -1e30
BIG = 1e30

NSA_Q = NSA_HEADS * HEAD_DIM
KV_W = NSA_GROUPS * HEAD_DIM
RET_QK = RET_HEADS * RET_DK
RET_V = RET_HEADS * RET_DV
IN_SPLITS = (NSA_Q, KV_W, KV_W, KV_W, KV_W, KV_W, KV_W, 3 * NSA_HEADS, RET_QK, RET_QK, RET_V, RET_V, D_MODEL, D_MODEL)
W_IN = sum(IN_SPLITS)

kernel_name = 'nsa_retention_moe_hybrid_step'


def rms_norm(x, g):
    xf = x.astype(jnp.float32)
    y = xf * lax.rsqrt(jnp.mean(xf * xf, axis=-1, keepdims=True) + EPS)
    return (y * g.astype(jnp.float32)).astype(x.dtype)


def split_cols(z):
    offs = np.cumsum(IN_SPLITS)[:-1].tolist()
    return jnp.split(z, offs, axis=-1)


def rotary(x, pos):
    d = x.shape[-1]
    inv = 1.0 / (ROPE_BASE ** jnp.linspace(0.0, 1.0, d // 2, dtype=jnp.float32))
    ang = pos.astype(jnp.float32)[:, None] * inv[None, :]
    cos = jnp.cos(ang)[None, :, None, :]
    sin = jnp.sin(ang)[None, :, None, :]
    xf = x.astype(jnp.float32).reshape(*x.shape[:-1], d // 2, 2)
    x0, x1 = xf[..., 0], xf[..., 1]
    out = jnp.stack([x0 * cos - x1 * sin, x1 * cos + x0 * sin], axis=-1)
    return out.reshape(x.shape).astype(x.dtype)


def ret_log_decay():
    return jnp.log(1.0 - 2.0 ** (-5.0 - jnp.arange(RET_HEADS, dtype=jnp.float32)))


def compress(rows, w1, b1, w2):
    b, l, g, hd = rows.shape
    r_n = CMP_LEN // CMP_STRIDE
    nh = l // CMP_STRIDE
    nc = nh - r_n + 1
    parts = rows[:, :nh * CMP_STRIDE].reshape(b, nh, CMP_STRIDE, g, hd)
    w1r = w1.reshape(r_n, CMP_STRIDE, hd, CMP_HIDDEN)
    pre = b1
    for r in range(r_n):
        pre = pre + jnp.einsum('bnsgd,sdh->bngh', parts[:, r:r + nc], w1r[r])
    return jnp.einsum('bngh,hd->bngd', jax.nn.gelu(pre), w2)


def compressed_tokens(k_rows, v_rows, p):
    kc = rms_norm(compress(k_rows, p['cmp_k_w1'], p['cmp_k_b1'], p['cmp_k_w2']), p['k_cmp_norm_g'])
    vc = compress(v_rows, p['cmp_v_w1'], p['cmp_v_b1'], p['cmp_v_w2'])
    return kc, vc


def nsa_core(q, qpos, gate, kc, vc, gather_sel, n_sel, kw, vw, wpos):
    tq = q.shape[0]
    scale = HEAD_DIM ** -0.5
    qg = q.reshape(tq, NSA_GROUPS, HEADS_PER_GROUP, HEAD_DIM)
    nc = kc.shape[0]
    c_end = jnp.arange(nc) * CMP_STRIDE + (CMP_LEN - 1)
    m_c = (c_end[None, :] <= qpos[:, None])[:, None, None, :]
    s_c = jnp.einsum('tgjd,cgd->tgjc', qg, kc).astype(jnp.float32) * scale
    p_c = jax.nn.softmax(jnp.where(m_c, s_c, NEG), axis=-1) * m_c
    o_c = jnp.einsum('tgjc,cgd->tgjd', p_c.astype(vc.dtype), vc)
    ci = jnp.arange(nc)[:, None] * CMP_STRIDE
    sj = jnp.arange(n_sel)[None, :] * SEL_BLK
    overlap = ((ci < sj + SEL_BLK) & (ci + CMP_LEN > sj)).astype(jnp.float32)
    score = jnp.einsum('tgc,cs->tgs', p_c.sum(axis=2), overlap)
    j = jnp.arange(n_sel)[None, :]
    cur = (qpos // SEL_BLK)[:, None]
    forced = (j == 0) | (j == cur) | (j == cur - 1)
    valid = j * SEL_BLK <= qpos[:, None]
    score = jnp.where(forced[:, None], BIG, score)
    score = jnp.where(valid[:, None], score, NEG)
    _, idx = lax.top_k(score, min(SEL_TOPK, n_sel))
    k_s, v_s = gather_sel(idx)
    kpos = idx[..., None] * SEL_BLK + jnp.arange(SEL_BLK)
    m_s = (kpos <= qpos[:, None, None, None])[:, :, None]
    s_s = jnp.einsum('tgjd,tgnsd->tgjns', qg, k_s).astype(jnp.float32) * scale
    s_s = jnp.where(m_s, s_s, NEG)
    shp = s_s.shape
    p_s = jax.nn.softmax(s_s.reshape(*shp[:3], -1), axis=-1).reshape(shp)
    o_s = jnp.einsum('tgjns,tgnsd->tgjd', p_s.astype(v_s.dtype), v_s)
    dpos = qpos[:, None] - wpos[None, :]
    m_w = ((dpos >= 0) & (dpos < WINDOW) & (wpos[None, :] >= 0))[:, None, None, :]
    s_w = jnp.einsum('tgjd,kgd->tgjk', qg, kw).astype(jnp.float32) * scale
    p_w = jax.nn.softmax(jnp.where(m_w, s_w, NEG), axis=-1)
    o_w = jnp.einsum('tgjk,kgd->tgjd', p_w.astype(vw.dtype), vw)
    o = o_c * gate[..., 0:1] + o_s * gate[..., 1:2] + o_w * gate[..., 2:3]
    return o.reshape(tq, NSA_HEADS * HEAD_DIM)


def nsa_prompt(q, gate, kc, vc, ks, vs, kw, vw):
    b, t = q.shape[:2]
    nq = t // Q_BLK
    n_sel = t // SEL_BLK
    ks5 = ks.reshape(b, n_sel, SEL_BLK, NSA_GROUPS, HEAD_DIM)
    vs5 = vs.reshape(b, n_sel, SEL_BLK, NSA_GROUPS, HEAD_DIM)
    padw = ((0, 0), (WINDOW, 0), (0, 0), (0, 0))
    kw_pad = jnp.pad(kw, padw)
    vw_pad = jnp.pad(vw, padw)
    gi = jnp.arange(NSA_GROUPS)[None, :, None]
    slab = (1, WINDOW + Q_BLK, NSA_GROUPS, HEAD_DIM)

    def item(args):
        qb, gb, bi, start = args
        qpos = start + jnp.arange(Q_BLK)
        ksb, vsb = ks5[bi], vs5[bi]

        def gather_sel(idx):
            return ksb[idx, :, gi], vsb[idx, :, gi]

        kwb = lax.dynamic_slice(kw_pad, (bi, start, 0, 0), slab)[0]
        vwb = lax.dynamic_slice(vw_pad, (bi, start, 0, 0), slab)[0]
        wpos = start - WINDOW + jnp.arange(WINDOW + Q_BLK)
        return nsa_core(qb, qpos, gb, kc[bi], vc[bi], gather_sel, n_sel, kwb, vwb, wpos)

    qs = q.reshape(b * nq, Q_BLK, NSA_HEADS, HEAD_DIM)
    gs = gate.reshape(b * nq, Q_BLK, NSA_GROUPS, HEADS_PER_GROUP, 3)
    bis = jnp.repeat(jnp.arange(b, dtype=jnp.int32), nq)
    starts = jnp.tile(jnp.arange(nq, dtype=jnp.int32) * Q_BLK, b)
    return lax.map(item, (qs, gs, bis, starts)).reshape(b, t, NSA_Q)


def nsa_sample(q, gate, kc, vc, ks_new, vs_new, kw, vw, wpos, pool_k_sel, pool_v_sel, page_table):
    b, t = q.shape[:2]
    past = page_table.shape[1] * PAGE_SIZE
    bpp = PAGE_SIZE // SEL_BLK
    n_past = past // SEL_BLK
    n_new = -(-t // SEL_BLK)
    n_sel = n_past + n_new
    padn = ((0, 0), (0, n_new * SEL_BLK - t), (0, 0), (0, 0))
    kn5 = jnp.pad(ks_new, padn).reshape(b, n_new, SEL_BLK, NSA_GROUPS, HEAD_DIM)
    vn5 = jnp.pad(vs_new, padn).reshape(b, n_new, SEL_BLK, NSA_GROUPS, HEAD_DIM)
    pk = pool_k_sel.reshape(-1, bpp, SEL_BLK, NSA_GROUPS, HEAD_DIM)
    pv = pool_v_sel.reshape(-1, bpp, SEL_BLK, NSA_GROUPS, HEAD_DIM)
    gi = jnp.arange(NSA_GROUPS)[None, :, None]
    qpos = past + jnp.arange(t)

    def item(args):
        qb, gb, bi = args
        pt = page_table[bi]
        knb, vnb = kn5[bi], vn5[bi]

        def gather_sel(idx):
            in_past = (idx < n_past)[..., None, None]
            jp = jnp.minimum(idx, n_past - 1)
            page, bip = pt[jp // bpp], jp % bpp
            jn = jnp.clip(idx - n_past, 0, n_new - 1)
            k = jnp.where(in_past, pk[page, bip, :, gi], knb[jn, :, gi])
            v = jnp.where(in_past, pv[page, bip, :, gi], vnb[jn, :, gi])
            return k, v

        return nsa_core(qb, qpos, gb, kc[bi], vc[bi], gather_sel, n_sel, kw[bi], vw[bi], wpos)

    return lax.map(item, (q, gate, jnp.arange(b, dtype=jnp.int32))).reshape(b, t, NSA_Q)


def retention_chunk(s_prev, q, k, v, log_g):
    c = q.shape[1]
    n = jnp.arange(c, dtype=jnp.float32)
    diff = n[:, None] - n[None, :]
    dmask = jnp.where(diff >= 0, jnp.exp(jnp.maximum(diff, 0.0)[None] * log_g[:, None, None]), 0.0)
    qf, kf, vf = q.astype(jnp.float32), k.astype(jnp.float32), v.astype(jnp.float32)
    inner = jnp.einsum('bihd,bjhd->bhij', qf, kf) * dmask
    o = jnp.einsum('bhij,bjhe->bihe', inner, vf)
    q_dec = jnp.exp((n + 1.0)[:, None] * log_g[None, :])
    o = o + jnp.einsum('bihd,bhde->bihe', qf * q_dec[None, :, :, None], s_prev)
    k_dec = jnp.exp((c - 1.0 - n)[:, None] * log_g[None, :])
    s_new = jnp.exp(c * log_g)[None, :, None, None] * s_prev + jnp.einsum('bjhd,bjhe->bhde', kf * k_dec[None, :, :, None], vf)
    return s_new, o


def moe_ffn(h, p):
    t, d = h.shape
    logits = (h @ p['router_w']).astype(jnp.float32) + p['router_b'].astype(jnp.float32)
    top_v, top_i = lax.top_k(logits, TOP_K)
    gates = jax.nn.softmax(top_v, axis=-1)
    tk = t * TOP_K
    flat_e = top_i.reshape(-1)
    flat_tok = jnp.arange(tk, dtype=jnp.int32) // TOP_K
    order = jnp.argsort(flat_e)
    sorted_e = flat_e[order]
    counts = jnp.bincount(flat_e, length=N_EXPERTS)
    padded = (counts + MOE_BLK - 1) // MOE_BLK * MOE_BLK
    pad_end = jnp.cumsum(padded)
    pad_start = pad_end - padded
    start = jnp.cumsum(counts) - counts
    dest = pad_start[sorted_e] + jnp.arange(tk) - start[sorted_e]
    n_blocks = -(-tk // MOE_BLK) + N_EXPERTS
    n_slots = n_blocks * MOE_BLK
    slot_tok = jnp.full((n_slots,), t, jnp.int32).at[dest].set(flat_tok[order])
    slot_gate = jnp.zeros((n_slots,), jnp.float32).at[dest].set(gates.reshape(-1)[order])
    blk_e = jnp.minimum(jnp.searchsorted(pad_end, jnp.arange(n_blocks) * MOE_BLK, side='right'), N_EXPERTS - 1)
    h_pad = jnp.concatenate([h, jnp.zeros((1, d), h.dtype)], axis=0)
    xb = h_pad[slot_tok].reshape(n_blocks, MOE_BLK, d)
    w_gu, b_gu, w_dn, b_dn = p['w_gate_up'], p['b_gate_up'], p['w_down'], p['b_down']

    def run(args):
        xe, e = args
        gu = xe @ w_gu[e] + b_gu[e]
        g = jnp.minimum(gu[:, :D_FF], SWIGLU_LIMIT)
        u = jnp.clip(gu[:, D_FF:], -SWIGLU_LIMIT, SWIGLU_LIMIT)
        a = (u + 1.0) * g * jax.nn.sigmoid(g * SWIGLU_ALPHA)
        return a @ w_dn[e] + b_dn[e]

    yb = lax.map(run, (xb, blk_e)).reshape(n_slots, d)
    yb = yb * slot_gate[:, None].astype(yb.dtype)
    return jnp.zeros((t + 1, d), yb.dtype).at[slot_tok].add(yb)[:t]


def project(x, pos, p):
    b, t, _ = x.shape
    xn = rms_norm(x, p['ln1_g'])
    (q, kc, vc, ks, vs, kw, vw, g_nsa, rq, rk, rv, rg, ga, gr) = split_cols(xn @ p['w_in'])
    kv = (b, t, NSA_GROUPS, HEAD_DIM)
    return {
        'q': rms_norm(q.reshape(b, t, NSA_HEADS, HEAD_DIM), p['q_norm_g']),
        'kc': kc.reshape(kv), 'vc': vc.reshape(kv),
        'ks': rms_norm(ks.reshape(kv), p['k_sel_norm_g']), 'vs': vs.reshape(kv),
        'kw': rms_norm(kw.reshape(kv), p['k_win_norm_g']), 'vw': vw.reshape(kv),
        'g_nsa': jax.nn.sigmoid(g_nsa).reshape(b, t, NSA_GROUPS, HEADS_PER_GROUP, 3),
        'rq': rotary(rq.reshape(b, t, RET_HEADS, RET_DK), pos),
        'rk': rotary(rk.reshape(b, t, RET_HEADS, RET_DK), pos) * (RET_DK ** -0.5),
        'rv': rv.reshape(b, t, RET_HEADS, RET_DV),
        'rg': rg, 'ga': ga, 'gr': gr,
    }


def merge_and_ffn(x, o_nsa, o_ret, t, p):
    b, s, _ = x.shape
    o_ret = rms_norm(o_ret, p['ret_norm_g']).astype(x.dtype).reshape(b, s, RET_V) * jax.nn.silu(t['rg'])
    mixed = jax.nn.sigmoid(t['ga']) * (o_nsa @ p['w_nsa_br']) + jax.nn.sigmoid(t['gr']) * (o_ret @ p['w_ret_br'])
    x = x + mixed @ p['w_out']
    h = rms_norm(x, p['ln2_g']).reshape(b * s, D_MODEL)
    return x + moe_ffn(h, p).reshape(b, s, D_MODEL)


def prompt_group(x, p):
    b, t, _ = x.shape
    tk = project(x, jnp.arange(t), p)
    kc, vc = compressed_tokens(tk['kc'], tk['vc'], p)
    o_nsa = nsa_prompt(tk['q'], tk['g_nsa'], kc, vc, tk['ks'], tk['vs'], tk['kw'], tk['vw'])
    nch = t // RET_CHUNK

    def to_chunks(a):
        return a.reshape(b, nch, RET_CHUNK, *a.shape[2:]).swapaxes(0, 1)

    log_g = ret_log_decay()
    s0 = jnp.zeros((b, RET_HEADS, RET_DK, RET_DV), jnp.float32)
    s_fin, o = lax.scan(lambda s, xs: retention_chunk(s, xs[0], xs[1], xs[2], log_g), s0,
                        (to_chunks(tk['rq']), to_chunks(tk['rk']), to_chunks(tk['rv'])))
    o_ret = o.swapaxes(0, 1).reshape(b, t, RET_HEADS, RET_DV)
    y = merge_and_ffn(x, o_nsa, o_ret, tk, p)
    wb = min(WINDOW, t)
    states = (tk['kc'], tk['vc'], tk['ks'], tk['vs'], tk['kw'][:, t - wb:], tk['vw'][:, t - wb:], s_fin.astype(x.dtype))
    return y, states


def sample_group(x, pool_k_cmp, pool_v_cmp, pool_k_sel, pool_v_sel, buf_k_win, buf_v_win, s_ret, page_table, p):
    b, t, _ = x.shape
    n_pages = page_table.shape[1]
    past = n_pages * PAGE_SIZE
    tk = project(x, past + jnp.arange(t), p)
    k_rows = jnp.concatenate([pool_k_cmp[page_table].reshape(b, past, NSA_GROUPS, HEAD_DIM), tk['kc']], axis=1)
    v_rows = jnp.concatenate([pool_v_cmp[page_table].reshape(b, past, NSA_GROUPS, HEAD_DIM), tk['vc']], axis=1)
    kc, vc = compressed_tokens(k_rows, v_rows, p)
    wb = buf_k_win.shape[1]
    kw = jnp.concatenate([buf_k_win, tk['kw']], axis=1)
    vw = jnp.concatenate([buf_v_win, tk['vw']], axis=1)
    wpos = past - wb + jnp.arange(wb + t)
    o_nsa = nsa_sample(tk['q'], tk['g_nsa'], kc, vc, tk['ks'], tk['vs'], kw, vw, wpos, pool_k_sel, pool_v_sel, page_table)
    s_new, o_ret = retention_chunk(s_ret.astype(jnp.float32), tk['rq'], tk['rk'], tk['rv'], ret_log_decay())
    y = merge_and_ffn(x, o_nsa, o_ret, tk, p)
    states = (tk['kc'], tk['vc'], tk['ks'], tk['vs'], kw[:, t:], vw[:, t:], s_new.astype(s_ret.dtype))
    return y, states


def setup_inputs(seed: int = 0) -> dict:
    key = jax.random.key(seed)
    ks = jax.random.split(key, 40)
    f32 = jnp.float32
    n_pages = PAST_LEN // PAGE_SIZE
    n_pool = (DEC_BATCH * n_pages * 5) // 4
    wb = min(WINDOW, PAST_LEN)

    def nrm(k, shape, s):
        return jax.random.normal(k, shape, f32) * s

    def gain(k, shape):
        return 1.0 + 0.05 * jax.random.normal(k, shape, f32)

    perm = jax.random.permutation(ks[0], n_pool).astype(jnp.int32)
    page_table = perm[:DEC_BATCH * n_pages].reshape(DEC_BATCH, n_pages)
    pool = (DEPTH, n_pool, PAGE_SIZE, NSA_GROUPS, HEAD_DIM)
    wbuf = (DEPTH, DEC_BATCH, wb, NSA_GROUPS, HEAD_DIM)
    return {
        'x_prompt': nrm(ks[1], (BATCH, SEQ, D_MODEL), 1.0),
        'x_sample': nrm(ks[2], (DEC_BATCH, DEC_SEQ, D_MODEL), 1.0),
        'cache_k_cmp': nrm(ks[3], pool, 1.0),
        'cache_v_cmp': nrm(ks[4], pool, 1.0),
        'cache_k_sel': nrm(ks[5], pool, 1.0),
        'cache_v_sel': nrm(ks[6], pool, 1.0),
        'state_k_win': nrm(ks[7], wbuf, 1.0),
        'state_v_win': nrm(ks[8], wbuf, 1.0),
        'state_ret': nrm(ks[9], (DEPTH, DEC_BATCH, RET_HEADS, RET_DK, RET_DV), 0.3),
        'page_table': page_table,
        'ln1_g': gain(ks[10], (DEPTH, D_MODEL)),
        'w_in': nrm(ks[11], (DEPTH, D_MODEL, W_IN), D_MODEL ** -0.5),
        'cmp_k_w1': nrm(ks[12], (DEPTH, CMP_LEN, HEAD_DIM, CMP_HIDDEN), (CMP_LEN * HEAD_DIM) ** -0.5),
        'cmp_k_b1': nrm(ks[13], (DEPTH, CMP_HIDDEN), 0.02),
        'cmp_k_w2': nrm(ks[14], (DEPTH, CMP_HIDDEN, HEAD_DIM), CMP_HIDDEN ** -0.5),
        'cmp_v_w1': nrm(ks[15], (DEPTH, CMP_LEN, HEAD_DIM, CMP_HIDDEN), (CMP_LEN * HEAD_DIM) ** -0.5),
        'cmp_v_b1': nrm(ks[16], (DEPTH, CMP_HIDDEN), 0.02),
        'cmp_v_w2': nrm(ks[17], (DEPTH, CMP_HIDDEN, HEAD_DIM), CMP_HIDDEN ** -0.5),
        'q_norm_g': gain(ks[18], (DEPTH, HEAD_DIM)),
        'k_cmp_norm_g': gain(ks[19], (DEPTH, HEAD_DIM)),
        'k_sel_norm_g': gain(ks[20], (DEPTH, HEAD_DIM)),
        'k_win_norm_g': gain(ks[21], (DEPTH, HEAD_DIM)),
        'ret_norm_g': gain(ks[22], (DEPTH, RET_HEADS, RET_DV)),
        'w_nsa_br': nrm(ks[23], (DEPTH, NSA_Q, D_MODEL), NSA_Q ** -0.5),
        'w_ret_br': nrm(ks[24], (DEPTH, RET_V, D_MODEL), RET_V ** -0.5),
        'w_out': nrm(ks[25], (DEPTH, D_MODEL, D_MODEL), D_MODEL ** -0.5),
        'ln2_g': gain(ks[26], (DEPTH, D_MODEL)),
        'router_w': nrm(ks[27], (DEPTH, D_MODEL, N_EXPERTS), D_MODEL ** -0.5),
        'router_b': nrm(ks[28], (DEPTH, N_EXPERTS), 0.01),
        'w_gate_up': nrm(ks[29], (DEPTH, N_EXPERTS, D_MODEL, 2 * D_FF), D_MODEL ** -0.5),
        'b_gate_up': nrm(ks[30], (DEPTH, N_EXPERTS, 2 * D_FF), 0.01),
        'w_down': nrm(ks[31], (DEPTH, N_EXPERTS, D_FF, D_MODEL), D_FF ** -0.5),
        'b_down': nrm(ks[32], (DEPTH, N_EXPERTS, D_MODEL), 0.01),
    }


def reference(x_prompt, x_sample, cache_k_cmp, cache_v_cmp, cache_k_sel, cache_v_sel, state_k_win, state_v_win,
              state_ret, page_table, ln1_g, w_in, cmp_k_w1, cmp_k_b1, cmp_k_w2, cmp_v_w1, cmp_v_b1, cmp_v_w2,
              q_norm_g, k_cmp_norm_g, k_sel_norm_g, k_win_norm_g, ret_norm_g, w_nsa_br, w_ret_br, w_out, ln2_g,
              router_w, router_b, w_gate_up, b_gate_up, w_down, b_down):
    y_p, y_s = x_prompt, x_sample
    new_p, new_s = [], []
    for l in range(DEPTH):
        p = {
            'ln1_g': ln1_g[l], 'w_in': w_in[l],
            'cmp_k_w1': cmp_k_w1[l], 'cmp_k_b1': cmp_k_b1[l], 'cmp_k_w2': cmp_k_w2[l],
            'cmp_v_w1': cmp_v_w1[l], 'cmp_v_b1': cmp_v_b1[l], 'cmp_v_w2': cmp_v_w2[l],
            'q_norm_g': q_norm_g[l], 'k_cmp_norm_g': k_cmp_norm_g[l], 'k_sel_norm_g': k_sel_norm_g[l],
            'k_win_norm_g': k_win_norm_g[l], 'ret_norm_g': ret_norm_g[l],
            'w_nsa_br': w_nsa_br[l], 'w_ret_br': w_ret_br[l], 'w_out': w_out[l], 'ln2_g': ln2_g[l],
            'router_w': router_w[l], 'router_b': router_b[l], 'w_gate_up': w_gate_up[l],
            'b_gate_up': b_gate_up[l], 'w_down': w_down[l], 'b_down': b_down[l],
        }
        y_p, sp = prompt_group(y_p, p)
        y_s, ss = sample_group(y_s, cache_k_cmp[l], cache_v_cmp[l], cache_k_sel[l], cache_v_sel[l],
                               state_k_win[l], state_v_win[l], state_ret[l], page_table, p)
        new_p.append(sp)
        new_s.append(ss)
    k_cmp_p, v_cmp_p, k_sel_p, v_sel_p, k_win_p, v_win_p, ret_p = [jnp.stack(a) for a in zip(*new_p)]
    k_cmp_s, v_cmp_s, k_sel_s, v_sel_s, k_win_s, v_win_s, ret_s = [jnp.stack(a) for a in zip(*new_s)]
    return (y_p, y_s, k_cmp_p, v_cmp_p, k_sel_p, v_sel_p, k_win_p, v_win_p, ret_p,
            k_cmp_s, v_cmp_s, k_sel_s, v_sel_s, k_win_s, v_win_s, ret_s)
```

```python
import functools
import math

import jax
import jax.numpy as jnp
from jax import lax
from jax.experimental import pallas as pl
from jax.experimental.pallas import tpu as pltpu

F32 = jnp.float32
BF16 = jnp.bfloat16

D_MODEL = 2048
PAGE_SIZE = 128
NSA_HEADS = 8
NSA_GROUPS = 2
HPG = NSA_HEADS // NSA_GROUPS
HEAD_DIM = 128
CMP_LEN = 32
CMP_STRIDE = 16
CMP_HIDDEN = 2 * HEAD_DIM
SEL_BLK = 64
SEL_SHIFT = 6
SEL_TOPK = 16
WINDOW = 512
Q_BLK = 128
RET_HEADS = 8
RET_DK = 128
RET_DV = 256
RET_CHUNK = 128
ROPE_BASE = 10000.0
N_EXPERTS = 32
TOP_K = 4
D_FF = D_MODEL
SWIGLU_LIMIT = 7.0
SWIGLU_ALPHA = 1.702
EPS = 1e-6
NEG = -1e30
BIG = 1e30

NSA_Q = NSA_HEADS * HEAD_DIM
KV_W = NSA_GROUPS * HEAD_DIM
RET_QK = RET_HEADS * RET_DK
RET_V = RET_HEADS * RET_DV

LANES = 128
ROW_TILE = 512
SEL_KV_TILE = 512
MOE_TILE = 512
MOE_FF_TILE = 256
VMEM_LIMIT = 56 * 1024 * 1024


def _cparams(*sem):
    return pltpu.CompilerParams(dimension_semantics=sem, vmem_limit_bytes=VMEM_LIMIT)


def _dot(a, b):
    return jnp.dot(a, b, preferred_element_type=F32)


def _dot_nt(a, b):
    return lax.dot_general(a, b, (((1,), (1,)), ((), ())), preferred_element_type=F32)


def _dot_tn(a, b):
    return lax.dot_general(a, b, (((0,), (0,)), ((), ())), preferred_element_type=F32)


def _sigmoid(x):
    return 1.0 / (1.0 + jnp.exp(-x))


def _unit_rms(x):
    return x * lax.rsqrt(jnp.mean(x * x, axis=-1, keepdims=True) + EPS)


def _iota(shape, dim):
    return lax.broadcasted_iota(jnp.int32, shape, dim)


def _split_bf16(x):
    hi = x.astype(BF16)
    lo = (x - hi.astype(F32)).astype(BF16)
    return hi, lo


def _ln1_body(x_ref, g_ref, o_ref):
    o_ref[...] = (_unit_rms(x_ref[...]) * g_ref[...]).astype(o_ref.dtype)


def _ln1(x, g):
    t, d = x.shape
    return pl.pallas_call(
        _ln1_body,
        grid=(t // ROW_TILE,),
        in_specs=[pl.BlockSpec((ROW_TILE, d), lambda i: (i, 0)),
                  pl.BlockSpec((1, d), lambda i: (0, 0))],
        out_specs=pl.BlockSpec((ROW_TILE, d), lambda i: (i, 0)),
        out_shape=jax.ShapeDtypeStruct((t, d), BF16),
        compiler_params=_cparams("parallel"),
        name="ln1",
    )(x, g.reshape(1, d))


def _proj_q_body(x_ref, w_ref, g_ref, o_ref):
    y = _dot(x_ref[...], w_ref[...])
    g = g_ref[...] * (HEAD_DIM ** -0.5)
    for c in range(NSA_HEADS):
        sl = slice(c * LANES, (c + 1) * LANES)
        o_ref[:, sl] = (_unit_rms(y[:, sl]) * g).astype(o_ref.dtype)


def _proj_kv_body(x_ref, w_ref, gs_ref, gw_ref, kvf_ref, kvb_ref, gate_ref):
    y = _dot(x_ref[...], w_ref[...])
    for c in range(12):
        sl = slice(c * LANES, (c + 1) * LANES)
        yc = y[:, sl]
        if c in (4, 5):
            yc = _unit_rms(yc) * gs_ref[...]
        elif c in (8, 9):
            yc = _unit_rms(yc) * gw_ref[...]
        kvf_ref[:, sl] = yc
        if c >= 4:
            kvb_ref[:, (c - 4) * LANES:(c - 3) * LANES] = yc.astype(BF16)
    gate_ref[...] = _sigmoid(y[:, 12 * LANES:13 * LANES])


def _proj_rot_body(x_ref, w_ref, inv_ref, o_ref, *, n_prompt, seq, past, dec_seq):
    j = pl.program_id(0)
    i = pl.program_id(1)
    y = _dot(x_ref[...], w_ref[...])
    tm = y.shape[0]
    row = i * tm + _iota((tm, LANES), 0)
    pos = jnp.where(row < n_prompt, row & (seq - 1), past + ((row - n_prompt) & (dec_seq - 1)))
    ang = pos.astype(F32) * inv_ref[...]
    cos = jnp.cos(ang)
    sin = jnp.sin(ang)
    even = (_iota((tm, LANES), 1) & 1) == 0
    sin = jnp.where(even, -sin, sin)
    scale = jnp.where(j == 0, 1.0, RET_DK ** -0.5).astype(F32)
    for c in range(RET_HEADS):
        sl = slice(c * LANES, (c + 1) * LANES)
        yc = y[:, sl]
        partner = jnp.where(even, pltpu.roll(yc, LANES - 1, 1), pltpu.roll(yc, 1, 1))
        o_ref[:, sl] = ((yc * cos + partner * sin) * scale).astype(o_ref.dtype)


def _proj_act_body(x_ref, w_ref, o_ref, *, tiles_plain, tiles_silu):
    j = pl.program_id(0)
    y = _dot(x_ref[...], w_ref[...])

    @pl.when(j < tiles_plain)
    def _():
        o_ref[...] = y.astype(o_ref.dtype)

    @pl.when((j >= tiles_plain) & (j < tiles_plain + tiles_silu))
    def _():
        o_ref[...] = (y * _sigmoid(y)).astype(o_ref.dtype)

    @pl.when(j >= tiles_plain + tiles_silu)
    def _():
        o_ref[...] = _sigmoid(y).astype(o_ref.dtype)


def _proj_call(body, xn, w, extra, extra_specs, out_shape, out_specs, tn, name):
    t, k = xn.shape
    n = w.shape[1]
    in_specs = [pl.BlockSpec((ROW_TILE, k), lambda j, i: (i, 0)),
                pl.BlockSpec((k, tn), lambda j, i: (0, j))] + extra_specs
    return pl.pallas_call(
        body,
        grid=(n // tn, t // ROW_TILE),
        in_specs=in_specs,
        out_specs=out_specs,
        out_shape=out_shape,
        compiler_params=_cparams("parallel", "parallel"),
        name=name,
    )(xn, w, *extra)


def _vec_spec(n):
    return pl.BlockSpec((1, n), lambda j, i: (0, 0))


def _compress_mlp(x_cat, w1_ref, b1_ref, w2_ref):
    nh = x_cat.shape[0]
    a = _dot(x_cat, w1_ref[...])
    pre = a[:, :CMP_HIDDEN] + pltpu.roll(a[:, CMP_HIDDEN:], nh - 1, 0) + b1_ref[...]
    hid = pre * (0.5 * (1.0 + jnp.tanh(math.sqrt(2.0 / math.pi) * (pre + 0.044715 * (pre * pre * pre)))))
    out = _dot(hid.astype(BF16), w2_ref[...])
    return jnp.where(_iota(out.shape, 0) < nh - 1, out, 0.0)


def _compress_finish(xk, xv, wrefs, kc_ref, vc_ref):
    w1k, b1k, w2k, w1v, b1v, w2v, gk = wrefs
    for g in range(NSA_GROUPS):
        sl = slice(g * LANES, (g + 1) * LANES)
        kc = _compress_mlp(xk[g], w1k, b1k, w2k)
        kc_ref[:, sl] = (_unit_rms(kc) * gk[...]).astype(kc_ref.dtype)
        vc_ref[:, sl] = _compress_mlp(xv[g], w1v, b1v, w2v).astype(vc_ref.dtype)


def _compress_prompt_body(*refs, nh):
    slabs, wrefs, (kc_ref, vc_ref) = refs[:4], refs[4:11], refs[11:]

    def gather(rows_ref):
        return jnp.concatenate(
            [rows_ref[pl.ds(s, nh, stride=CMP_STRIDE), :].astype(BF16) for s in range(CMP_STRIDE)], axis=1)

    xk = [gather(slabs[g]) for g in range(NSA_GROUPS)]
    xv = [gather(slabs[NSA_GROUPS + g]) for g in range(NSA_GROUPS)]
    _compress_finish(xk, xv, wrefs, kc_ref, vc_ref)


def _compress_sample_body(pt_ref, *refs, n_pages):
    del pt_ref
    kpages, vpages = refs[:n_pages], refs[n_pages:2 * n_pages]
    wrefs, (kc_ref, vc_ref) = refs[2 * n_pages:2 * n_pages + 7], refs[2 * n_pages + 7:]
    per_page = PAGE_SIZE // CMP_STRIDE

    def gather(pages, g):
        cols = []
        for s in range(CMP_STRIDE):
            rows = [p[pl.ds(NSA_GROUPS * s + g, per_page, stride=NSA_GROUPS * CMP_STRIDE), :] for p in pages]
            cols.append(jnp.concatenate(rows, axis=0).astype(BF16))
        return jnp.concatenate(cols, axis=1)

    xk = [gather(kpages, g) for g in range(NSA_GROUPS)]
    xv = [gather(vpages, g) for g in range(NSA_GROUPS)]
    _compress_finish(xk, xv, wrefs, kc_ref, vc_ref)


def _const_spec(shape, nargs):
    zeros = (0,) * len(shape)
    return pl.BlockSpec(shape, lambda *a: zeros)


def _compress_weights(w1, b1, w2):
    r_n = CMP_LEN // CMP_STRIDE
    w1r = w1.reshape(r_n, CMP_STRIDE * HEAD_DIM, CMP_HIDDEN)
    w1cat = jnp.concatenate([w1r[r] for r in range(r_n)], axis=1).astype(BF16)
    return w1cat, b1.reshape(1, CMP_HIDDEN), w2.astype(BF16)


def _compress_weight_specs():
    k16 = CMP_STRIDE * HEAD_DIM
    one = [_const_spec((k16, 2 * CMP_HIDDEN), 0), _const_spec((1, CMP_HIDDEN), 0),
           _const_spec((CMP_HIDDEN, HEAD_DIM), 0)]
    return one + one + [_const_spec((1, HEAD_DIM), 0)]


def _compress_prompt(kvf, weights, batch, seq):
    nh = seq // CMP_STRIDE
    out = jax.ShapeDtypeStruct((batch, nh, KV_W), BF16)
    ospec = pl.BlockSpec((None, nh, KV_W), lambda b: (b, 0, 0))
    return pl.pallas_call(
        functools.partial(_compress_prompt_body, nh=nh),
        grid=(batch,),
        in_specs=[pl.BlockSpec((seq, LANES), functools.partial(lambda c, b: (b, c), c))
                  for c in range(2 * NSA_GROUPS)] + _compress_weight_specs(),
        out_specs=[ospec, ospec],
        out_shape=[out, out],
        compiler_params=_cparams("parallel"),
        name="compress_prompt",
    )(*([kvf] * (2 * NSA_GROUPS)), *weights)


def _page_specs(n_pages):
    def spec(p):
        return pl.BlockSpec((None, NSA_GROUPS * PAGE_SIZE, HEAD_DIM),
                            lambda b, pt: (pt[b * n_pages + p], 0, 0))
    return [spec(p) for p in range(n_pages)]


def _compress_sample(pool_k, pool_v, pt_flat, weights, dec_batch, n_pages):
    nh = n_pages * PAGE_SIZE // CMP_STRIDE
    out = jax.ShapeDtypeStruct((dec_batch, nh, KV_W), BF16)
    ospec = pl.BlockSpec((None, nh, KV_W), lambda b, pt: (b, 0, 0))
    grid_spec = pltpu.PrefetchScalarGridSpec(
        num_scalar_prefetch=1,
        grid=(dec_batch,),
        in_specs=_page_specs(n_pages) + _page_specs(n_pages) + _compress_weight_specs(),
        out_specs=[ospec, ospec],
    )
    return pl.pallas_call(
        functools.partial(_compress_sample_body, n_pages=n_pages),
        grid_spec=grid_spec,
        out_shape=[out, out],
        compiler_params=_cparams("parallel"),
        name="compress_sample",
    )(pt_flat, *([pool_k] * n_pages), *([pool_v] * n_pages), *weights)


def _masked_softmax(s3, ok):
    s3 = jnp.where(ok[None], s3, NEG)
    m = jnp.max(s3, axis=-1, keepdims=True)
    e = jnp.where(ok[None], jnp.exp(s3 - m), 0.0)
    den = jnp.sum(e, axis=-1, keepdims=True)
    return e / jnp.where(den > 0.0, den, 1.0)


def _topk_lanes(score, n_sel, k):
    lane = _iota(score.shape, 1)
    sc = jnp.where(lane < n_sel, score, -jnp.inf)
    rank = jnp.zeros(score.shape, F32)
    for i in range(n_sel):
        ci = sc[:, i:i + 1]
        later = jnp.where(lane > i, 1.0, 0.0)
        rank = rank + jnp.where(ci > sc, 1.0, jnp.where(ci == sc, later, 0.0))
    return jnp.where((rank < k) & (lane < n_sel), 1.0, 0.0)


def _select_blocks(qg, kcg, vcg, qpos, n_cmp, n_sel):
    t = qpos.shape[0]
    nc_pad = kcg.shape[0]
    c_idx = _iota((t, nc_pad), 1)
    vis = (c_idx * CMP_STRIDE + (CMP_LEN - 1) <= qpos) & (c_idx < n_cmp)
    p3 = _masked_softmax(_dot_nt(qg, kcg).reshape(HPG, t, nc_pad), vis)
    o_c = _dot(p3.reshape(HPG * t, nc_pad).astype(BF16), vcg)
    psum = jnp.sum(p3, axis=0)
    ci = _iota((nc_pad, LANES), 0) * CMP_STRIDE
    sj = _iota((nc_pad, LANES), 1) * SEL_BLK
    overlap = jnp.where((ci < sj + SEL_BLK) & (ci + CMP_LEN > sj), 1.0, 0.0).astype(BF16)
    hi, lo = _split_bf16(psum)
    score = _dot(hi, overlap) + _dot(lo, overlap)
    j = _iota((t, LANES), 1)
    cur = qpos >> SEL_SHIFT
    forced = (j == 0) | (j == cur) | (j == cur - 1)
    score = jnp.where(forced, BIG, score)
    score = jnp.where(j * SEL_BLK <= qpos, score, NEG)
    return o_c, _topk_lanes(score, n_sel, min(SEL_TOPK, n_sel))


def _expand_blocks(sel, blk0, n):
    blk = blk0 + (_iota((LANES, n), 1) >> SEL_SHIFT)
    expand = jnp.where(_iota((LANES, n), 0) == blk, 1.0, 0.0).astype(BF16)
    return _dot(sel.astype(BF16), expand)


def _nsa_prompt_body(q_ref, gate_ref, kc_ref, vc_ref, kv_ref, o_ref, *, seq):
    tq = Q_BLK
    start = pl.program_id(1) * tq
    n_sel = seq // SEL_BLK
    n_cmp = seq // CMP_STRIDE - CMP_LEN // CMP_STRIDE + 1
    qpos = start + _iota((tq, 1), 0)
    gates = gate_ref[...]
    wk = WINDOW + tq
    for g in range(NSA_GROUPS):
        gsl = slice(g * LANES, (g + 1) * LANES)
        qg = jnp.concatenate(
            [q_ref[:, (g * HPG + j) * LANES:(g * HPG + j + 1) * LANES] for j in range(HPG)], axis=0)
        o_c, sel = _select_blocks(qg, kc_ref[:, gsl], vc_ref[:, gsl], qpos, n_cmp, n_sel)

        def sel_step(k, carry):
            m, l, acc = carry
            k0 = pl.multiple_of(k * SEL_KV_TILE, SEL_KV_TILE)
            kt = kv_ref[pl.ds(k0, SEL_KV_TILE), pl.ds(g * LANES, LANES)]
            vt = kv_ref[pl.ds(k0, SEL_KV_TILE), pl.ds(KV_W + g * LANES, LANES)]
            s3 = _dot_nt(qg, kt).reshape(HPG, tq, SEL_KV_TILE)
            chosen = _expand_blocks(sel, k * (SEL_KV_TILE // SEL_BLK), SEL_KV_TILE)
            ok = (chosen > 0.5) & (k0 + _iota((tq, SEL_KV_TILE), 1) <= qpos)
            s3 = jnp.where(ok[None], s3, NEG)
            m_new = jnp.maximum(m, jnp.max(s3, axis=-1, keepdims=True))
            alpha = jnp.exp(m - m_new)
            p = jnp.where(ok[None], jnp.exp(s3 - m_new), 0.0)
            l = alpha * l + jnp.sum(p, axis=-1, keepdims=True)
            pv = _dot(p.reshape(HPG * tq, SEL_KV_TILE).astype(BF16), vt)
            return m_new, l, alpha * acc + pv.reshape(HPG, tq, LANES)

        n_kv = (start + tq + SEL_KV_TILE - 1) // SEL_KV_TILE
        init = (jnp.full((HPG, tq, 1), NEG, F32), jnp.zeros((HPG, tq, 1), F32),
                jnp.zeros((HPG, tq, LANES), F32))
        _, l_s, acc_s = lax.fori_loop(0, n_kv, sel_step, init)
        o_s = acc_s / l_s

        w0 = pl.multiple_of(jnp.maximum(start - WINDOW, 0), Q_BLK)
        kw = kv_ref[pl.ds(w0, wk), pl.ds(2 * KV_W + g * LANES, LANES)]
        vw = kv_ref[pl.ds(w0, wk), pl.ds(3 * KV_W + g * LANES, LANES)]
        dpos = qpos - (w0 + _iota((tq, wk), 1))
        p_w = _masked_softmax(_dot_nt(qg, kw).reshape(HPG, tq, wk), (dpos >= 0) & (dpos < WINDOW))
        o_w = _dot(p_w.reshape(HPG * tq, wk).astype(BF16), vw)

        for j in range(HPG):
            h = g * HPG + j
            rows = slice(j * tq, (j + 1) * tq)
            o = (o_c[rows] * gates[:, 3 * h:3 * h + 1] + o_s[j] * gates[:, 3 * h + 1:3 * h + 2]
                 + o_w[rows] * gates[:, 3 * h + 2:3 * h + 3])
            o_ref[:, h * LANES:(h + 1) * LANES] = o.astype(o_ref.dtype)


def _nsa_prompt(q, gate, kc, vc, kvb, batch, seq):
    nq = seq // Q_BLK
    nh = seq // CMP_STRIDE
    return pl.pallas_call(
        functools.partial(_nsa_prompt_body, seq=seq),
        grid=(batch, nq),
        in_specs=[pl.BlockSpec((Q_BLK, NSA_Q), lambda b, i: (b * nq + i, 0)),
                  pl.BlockSpec((Q_BLK, LANES), lambda b, i: (b * nq + i, 0)),
                  pl.BlockSpec((None, nh, KV_W), lambda b, i: (b, 0, 0)),
                  pl.BlockSpec((None, nh, KV_W), lambda b, i: (b, 0, 0)),
                  pl.BlockSpec((seq, 4 * KV_W), lambda b, i: (b, 0))],
        out_specs=pl.BlockSpec((Q_BLK, NSA_Q), lambda b, i: (b * nq + i, 0)),
        out_shape=jax.ShapeDtypeStruct((batch * seq, NSA_Q), BF16),
        compiler_params=_cparams("parallel", "parallel"),
        name="nsa_prompt",
    )(q, gate, kc, vc, kvb)


def _nsa_sample_body(pt_ref, q_ref, gate_ref, new_ref, kc_ref, vc_ref, *refs, n_pages, past):
    del pt_ref
    kpages, vpages = refs[:n_pages], refs[n_pages:2 * n_pages]
    kwbuf_ref, vwbuf_ref, o_ref, kwout_ref, vwout_ref = refs[2 * n_pages:]
    t = q_ref.shape[0]
    wb = kwbuf_ref.shape[0] // NSA_GROUPS
    n_past = past // SEL_BLK
    n_sel = n_past + -(-t // SEL_BLK)
    n_cmp = (past + t) // CMP_STRIDE - CMP_LEN // CMP_STRIDE + 1
    qpos = past + _iota((t, 1), 0)
    gates = gate_ref[...]
    q = q_ref[...]
    new = new_ref[...]
    pad = jnp.zeros((LANES - t, LANES), BF16)
    n_keys = past + LANES
    wk = wb + LANES
    for g in range(NSA_GROUPS):
        gsl = slice(g * LANES, (g + 1) * LANES)

        def new_rows(which):
            return new[:, which * KV_W + g * LANES:which * KV_W + (g + 1) * LANES]

        qg = jnp.concatenate(
            [q[:, (g * HPG + j) * LANES:(g * HPG + j + 1) * LANES] for j in range(HPG)], axis=0).astype(BF16)
        o_c, sel = _select_blocks(qg, kc_ref[:, gsl], vc_ref[:, gsl], qpos, n_cmp, n_sel)

        def keys(pages, which):
            past_rows = [p[pl.ds(g, PAGE_SIZE, stride=NSA_GROUPS), :].astype(BF16) for p in pages]
            return jnp.concatenate(past_rows + [new_rows(which).astype(BF16), pad], axis=0)

        ks, vs = keys(kpages, 2), keys(vpages, 3)
        ok = (_expand_blocks(sel, 0, n_keys) > 0.5) & (_iota((t, n_keys), 1) <= qpos)
        p_s = _masked_softmax(_dot_nt(qg, ks).reshape(HPG, t, n_keys), ok)
        o_s = _dot(p_s.reshape(HPG * t, n_keys).astype(BF16), vs)

        def window(buf_ref, which):
            return jnp.concatenate([buf_ref[pl.ds(g, wb, stride=NSA_GROUPS), :].astype(BF16),
                                    new_rows(which).astype(BF16), pad], axis=0)

        kw, vw = window(kwbuf_ref, 4), window(vwbuf_ref, 5)
        dpos = qpos - (past - wb + _iota((t, wk), 1))
        p_w = _masked_softmax(_dot_nt(qg, kw).reshape(HPG, t, wk), (dpos >= 0) & (dpos < WINDOW))
        o_w = _dot(p_w.reshape(HPG * t, wk).astype(BF16), vw)

        for j in range(HPG):
            h = g * HPG + j
            rows = slice(j * t, (j + 1) * t)
            o = (o_c[rows] * gates[:, 3 * h:3 * h + 1] + o_s[rows] * gates[:, 3 * h + 1:3 * h + 2]
                 + o_w[rows] * gates[:, 3 * h + 2:3 * h + 3])
            o_ref[:, h * LANES:(h + 1) * LANES] = o

        kwout_ref[pl.ds((wb - t) * NSA_GROUPS + g, t, stride=NSA_GROUPS), :] = new_rows(4)
        vwout_ref[pl.ds((wb - t) * NSA_GROUPS + g, t, stride=NSA_GROUPS), :] = new_rows(5)
    keep = (wb - t) * NSA_GROUPS
    kwout_ref[pl.ds(0, keep), :] = kwbuf_ref[pl.ds(t * NSA_GROUPS, keep), :]
    vwout_ref[pl.ds(0, keep), :] = vwbuf_ref[pl.ds(t * NSA_GROUPS, keep), :]


def _nsa_sample(q, gate, new, kc, vc, pool_k, pool_v, kwbuf, vwbuf, pt_flat, dec_batch, t, n_pages):
    past = n_pages * PAGE_SIZE
    nh = past // CMP_STRIDE
    wrows = kwbuf.shape[1]

    def rows(n):
        return pl.BlockSpec((t, n), lambda b, pt: (b, 0))

    def per_b(r, c):
        return pl.BlockSpec((None, r, c), lambda b, pt: (b, 0, 0))

    grid_spec = pltpu.PrefetchScalarGridSpec(
        num_scalar_prefetch=1,
        grid=(dec_batch,),
        in_specs=[rows(NSA_Q), rows(LANES), rows(6 * KV_W), per_b(nh, KV_W), per_b(nh, KV_W)]
        + _page_specs(n_pages) + _page_specs(n_pages) + [per_b(wrows, HEAD_DIM), per_b(wrows, HEAD_DIM)],
        out_specs=[rows(NSA_Q), per_b(wrows, HEAD_DIM), per_b(wrows, HEAD_DIM)],
    )
    wout = jax.ShapeDtypeStruct(kwbuf.shape, F32)
    return pl.pallas_call(
        functools.partial(_nsa_sample_body, n_pages=n_pages, past=past),
        grid_spec=grid_spec,
        out_shape=[jax.ShapeDtypeStruct((dec_batch * t, NSA_Q), F32), wout, wout],
        compiler_params=_cparams("parallel"),
        name="nsa_sample",
    )(pt_flat, q, gate, new, kc, vc, *([pool_k] * n_pages), *([pool_v] * n_pages), kwbuf, vwbuf)


def _log_decay(h):
    return math.log(1.0 - 2.0 ** (-5.0 - h))


def _ret_finish(o, h, gn_ref, rg):
    sl = slice(h * RET_DV, (h + 1) * RET_DV)
    return _unit_rms(o) * gn_ref[:, sl] * rg[:, sl]


def _ret_prompt_body(rq_ref, rk_ref, rv_ref, rg_ref, gn_ref, o_ref, s_ref):
    c = RET_CHUNK

    @pl.when(pl.program_id(1) == 0)
    def _():
        s_ref[...] = jnp.zeros(s_ref.shape, F32)

    n_col = _iota((c, 1), 0).astype(F32)
    diff = (_iota((c, c), 0) - _iota((c, c), 1)).astype(F32)
    rg = rg_ref[...].astype(F32)
    for h in range(RET_HEADS):
        lg = _log_decay(h)
        q = rq_ref[:, h * RET_DK:(h + 1) * RET_DK]
        k = rk_ref[:, h * RET_DK:(h + 1) * RET_DK]
        v = rv_ref[:, h * RET_DV:(h + 1) * RET_DV]
        s_prev = s_ref[h]
        dmask = jnp.where(diff >= 0.0, jnp.exp(jnp.maximum(diff, 0.0) * lg), 0.0)
        o = _dot((_dot_nt(q, k) * dmask).astype(BF16), v)
        q_dec = (q.astype(F32) * jnp.exp((n_col + 1.0) * lg)).astype(BF16)
        o = o + _dot(q_dec, s_prev.astype(BF16))
        k_dec = (k.astype(F32) * jnp.exp((c - 1.0 - n_col) * lg)).astype(BF16)
        s_ref[h] = math.exp(c * lg) * s_prev + _dot_tn(k_dec, v)
        o_ref[:, h * RET_DV:(h + 1) * RET_DV] = _ret_finish(o, h, gn_ref, rg).astype(o_ref.dtype)


def _ret_prompt(rot, act, gn, batch, seq):
    nch = seq // RET_CHUNK
    qk_w = RET_QK

    def rows(n, col):
        return pl.BlockSpec((RET_CHUNK, n), lambda b, i: (b * nch + i, col))

    return pl.pallas_call(
        _ret_prompt_body,
        grid=(batch, nch),
        in_specs=[rows(qk_w, 0), rows(qk_w, 1), rows(RET_V, 0), rows(RET_V, 1),
                  pl.BlockSpec((1, RET_V), lambda b, i: (0, 0))],
        out_specs=[rows(RET_V, 0),
                   pl.BlockSpec((None, RET_HEADS, RET_DK, RET_DV), lambda b, i: (b, 0, 0, 0))],
        out_shape=[jax.ShapeDtypeStruct((batch * seq, RET_V), BF16),
                   jax.ShapeDtypeStruct((batch, RET_HEADS, RET_DK, RET_DV), F32)],
        compiler_params=_cparams("parallel", "arbitrary"),
        name="retention_prompt",
    )(rot, rot, act, act, gn)


def _ret_sample_body(rq_ref, rk_ref, rv_ref, rg_ref, gn_ref, s_ref, o_ref, so_ref):
    c = rq_ref.shape[0]
    n_col = _iota((c, 1), 0).astype(F32)
    rg = rg_ref[...]
    zk = jnp.zeros((LANES - c, RET_DK), BF16)
    zv = jnp.zeros((LANES - c, RET_DV), BF16)
    for h in range(RET_HEADS):
        lg = _log_decay(h)
        q = rq_ref[:, h * RET_DK:(h + 1) * RET_DK]
        k = rk_ref[:, h * RET_DK:(h + 1) * RET_DK]
        v = rv_ref[:, h * RET_DV:(h + 1) * RET_DV]
        s_prev = s_ref[h]
        o = _dot((q * jnp.exp((n_col + 1.0) * lg)).astype(BF16), s_prev.astype(BF16))
        for j in range(c):
            qk = jnp.sum(q * k[j:j + 1, :], axis=-1, keepdims=True)
            dj = jnp.where(n_col >= j, jnp.exp(jnp.maximum(n_col - j, 0.0) * lg), 0.0)
            o = o + (qk * dj) * v[j:j + 1, :]
        k_dec = jnp.concatenate([(k * jnp.exp((c - 1.0 - n_col) * lg)).astype(BF16), zk], axis=0)
        v_pad = jnp.concatenate([v.astype(BF16), zv], axis=0)
        so_ref[h] = math.exp(c * lg) * s_prev + _dot_tn(k_dec, v_pad)
        o_ref[:, h * RET_DV:(h + 1) * RET_DV] = _ret_finish(o, h, gn_ref, rg)


def _ret_sample(rq, rk, rv, rg, gn, state, dec_batch, t):
    def rows(n):
        return pl.BlockSpec((t, n), lambda b: (b, 0))

    sspec = pl.BlockSpec((None, RET_HEADS, RET_DK, RET_DV), lambda b: (b, 0, 0, 0))
    return pl.pallas_call(
        _ret_sample_body,
        grid=(dec_batch,),
        in_specs=[rows(RET_QK), rows(RET_QK), rows(RET_V), rows(RET_V),
                  pl.BlockSpec((1, RET_V), lambda b: (0, 0)), sspec],
        out_specs=[rows(RET_V), sspec],
        out_shape=[jax.ShapeDtypeStruct((dec_batch * t, RET_V), F32),
                   jax.ShapeDtypeStruct(state.shape, F32)],
        compiler_params=_cparams("parallel"),
        name="retention_sample",
    )(rq, rk, rv, rg, gn, state)


def _merge_body(on_ref, or_ref, wn_ref, wr_ref, ga_ref, gr_ref, o_ref):
    a = _dot(on_ref[...], wn_ref[...])
    r = _dot(or_ref[...], wr_ref[...])
    o_ref[...] = (ga_ref[...].astype(F32) * a + gr_ref[...].astype(F32) * r).astype(o_ref.dtype)


def _merge(o_nsa, o_ret, wn, wr, act, tn=1024):
    t = o_nsa.shape[0]
    d = wn.shape[1]
    ga0 = (RET_V + RET_V) // tn
    gr0 = (RET_V + RET_V + D_MODEL) // tn
    return pl.pallas_call(
        _merge_body,
        grid=(d // tn, t // ROW_TILE),
        in_specs=[pl.BlockSpec((ROW_TILE, NSA_Q), lambda j, i: (i, 0)),
                  pl.BlockSpec((ROW_TILE, RET_V), lambda j, i: (i, 0)),
                  pl.BlockSpec((NSA_Q, tn), lambda j, i: (0, j)),
                  pl.BlockSpec((RET_V, tn), lambda j, i: (0, j)),
                  pl.BlockSpec((ROW_TILE, tn), lambda j, i: (i, ga0 + j)),
                  pl.BlockSpec((ROW_TILE, tn), lambda j, i: (i, gr0 + j))],
        out_specs=pl.BlockSpec((ROW_TILE, tn), lambda j, i: (i, j)),
        out_shape=jax.ShapeDtypeStruct((t, d), BF16),
        compiler_params=_cparams("parallel", "parallel"),
        name="merge",
    )(o_nsa, o_ret, wn, wr, act, act)


def _out_router_body(x_ref, mix_ref, wo_ref, g_ref, rwh_ref, rwl_ref, rb_ref,
                     x1_ref, h_ref, route_ref, cnt_ref):
    @pl.when(pl.program_id(0) == 0)
    def _():
        cnt_ref[...] = jnp.zeros(cnt_ref.shape, F32)

    x1 = x_ref[...] + _dot(mix_ref[...], wo_ref[...])
    x1_ref[...] = x1
    h = _unit_rms(x1) * g_ref[...]
    h_ref[...] = h
    tm = h.shape[0]
    lane = _iota((tm, LANES), 1)
    hi, lo = _split_bf16(h)
    logits = _dot(hi, rwh_ref[...]) + _dot(lo, rwh_ref[...]) + _dot(hi, rwl_ref[...]) + rb_ref[...]
    work = jnp.where(lane < N_EXPERTS, logits, -jnp.inf)
    vals, idxs = [], []
    for _ in range(TOP_K):
        v = jnp.max(work, axis=-1, keepdims=True)
        ix = jnp.min(jnp.where(work == v, lane, LANES), axis=-1, keepdims=True)
        vals.append(v)
        idxs.append(ix)
        work = jnp.where(lane == ix, -jnp.inf, work)
    es = [jnp.exp(v - vals[0]) for v in vals]
    den = es[0] + es[1] + es[2] + es[3]
    hot = jnp.zeros((tm, LANES), F32)
    for ix in idxs:
        hot = hot + jnp.where(lane == ix, 1.0, 0.0)
    before = jnp.where(_iota((tm, tm), 1) < _iota((tm, tm), 0), 1.0, 0.0).astype(BF16)
    ranks = _dot(before, hot.astype(BF16)) + cnt_ref[...]
    route = jnp.zeros((tm, LANES), F32)
    for k in range(TOP_K):
        rk = jnp.sum(jnp.where(lane == idxs[k], ranks, 0.0), axis=-1, keepdims=True)
        route = route + jnp.where(lane == k, idxs[k].astype(F32), 0.0)
        route = route + jnp.where(lane == TOP_K + k, es[k] / den, 0.0)
        route = route + jnp.where(lane == 2 * TOP_K + k, rk, 0.0)
    route_ref[...] = route
    cnt_ref[...] = cnt_ref[...] + jnp.sum(hot, axis=0, keepdims=True)


def _out_router(x, mixed, wo, g2, rwh, rwl, rb, tm=256):
    t, d = x.shape
    rows = pl.BlockSpec((tm, d), lambda i: (i, 0))
    lanes = pl.BlockSpec((tm, LANES), lambda i: (i, 0))

    def const(r, c):
        return pl.BlockSpec((r, c), lambda i: (0, 0))

    return pl.pallas_call(
        _out_router_body,
        grid=(t // tm,),
        in_specs=[rows, rows, const(d, d), const(1, d), const(d, LANES), const(d, LANES), const(1, LANES)],
        out_specs=[rows, rows, lanes, const(1, LANES)],
        out_shape=[jax.ShapeDtypeStruct((t, d), F32), jax.ShapeDtypeStruct((t, d), F32),
                   jax.ShapeDtypeStruct((t, LANES), F32), jax.ShapeDtypeStruct((1, LANES), F32)],
        compiler_params=_cparams("arbitrary"),
        name="out_router",
    )(x, mixed, wo, g2, rwh, rwl, rb)


def _row_copy(src, r_src, dst, r_dst, sem):
    return pltpu.make_async_copy(src.at[pl.ds(r_src, 1), :], dst.at[pl.ds(r_dst, 1), :], sem)


def _dispatch_body(start_ref, cnt_ref, pend_ref, e_ref, rank_ref, h_ref, xs_ref, zero_ref, sem, zsem):
    tm = h_ref.shape[0]

    def slot(r, k):
        return start_ref[e_ref[r * TOP_K + k]] + rank_ref[r * TOP_K + k]

    def issue(r, _):
        for k in range(TOP_K):
            _row_copy(h_ref, r, xs_ref, slot(r, k), sem).start()
        return 0

    def drain(r, _):
        for k in range(TOP_K):
            _row_copy(h_ref, r, xs_ref, slot(r, k), sem).wait()
        return 0

    lax.fori_loop(0, tm, issue, 0)

    @pl.when(pl.program_id(0) == 0)
    def _():
        zero_ref[...] = jnp.zeros(zero_ref.shape, F32)
        zrows = zero_ref.shape[0]
        for phase in ("start", "wait"):
            def per_expert(e, _):
                def per_row(s, _):
                    cp = _row_copy(zero_ref, 0, xs_ref, s, zsem)
                    cp.start() if phase == "start" else cp.wait()
                    return 0
                return lax.fori_loop(start_ref[e] + cnt_ref[e], pend_ref[e], per_row, 0)
            lax.fori_loop(0, N_EXPERTS, per_expert, 0)

            def per_chunk(s, _):
                s0 = pl.multiple_of(s * zrows, zrows)
                cp = pltpu.make_async_copy(zero_ref, xs_ref.at[pl.ds(s0, zrows), :], zsem)
                cp.start() if phase == "start" else cp.wait()
                return 0
            lax.fori_loop(pend_ref[N_EXPERTS - 1] // zrows, xs_ref.shape[0] // zrows, per_chunk, 0)

    lax.fori_loop(0, tm, drain, 0)


def _dispatch(h, e_flat, rank_flat, pad_start, counts, pad_end, n_slots, tm=256):
    t, d = h.shape
    smem = pl.BlockSpec((tm * TOP_K,), lambda i, *_: (i,), memory_space=pltpu.SMEM)
    grid_spec = pltpu.PrefetchScalarGridSpec(
        num_scalar_prefetch=3,
        grid=(t // tm,),
        in_specs=[smem, smem, pl.BlockSpec((tm, d), lambda i, *_: (i, 0))],
        out_specs=pl.BlockSpec(memory_space=pl.ANY),
        scratch_shapes=[pltpu.VMEM((MOE_TILE // 8, d), F32), pltpu.SemaphoreType.DMA(()),
                        pltpu.SemaphoreType.DMA(())],
    )
    return pl.pallas_call(
        _dispatch_body,
        grid_spec=grid_spec,
        out_shape=jax.ShapeDtypeStruct((n_slots, d), F32),
        compiler_params=pltpu.CompilerParams(dimension_semantics=("arbitrary",), vmem_limit_bytes=VMEM_LIMIT,
                                             has_side_effects=True),
        name="moe_dispatch",
    )(pad_start, counts, pad_end, e_flat, rank_flat, h)


def _experts_body(te_ref, nu_ref, xs_ref, wg_ref, wu_ref, bg_ref, bu_ref, wd_ref, bd_ref, y_ref, xb_ref):
    del te_ref
    i = pl.program_id(0)
    c = pl.program_id(1)
    used = i < nu_ref[0]

    @pl.when(used & (c == 0))
    def _():
        xb_ref[...] = xs_ref[...].astype(BF16)

    @pl.when(jnp.logical_not(used) & (c == 0))
    def _():
        y_ref[...] = jnp.zeros(y_ref.shape, F32)

    @pl.when(used)
    def _():
        x = xb_ref[...]
        g = jnp.minimum(_dot(x, wg_ref[...].astype(BF16)) + bg_ref[...], SWIGLU_LIMIT)
        u = jnp.clip(_dot(x, wu_ref[...].astype(BF16)) + bu_ref[...], -SWIGLU_LIMIT, SWIGLU_LIMIT)
        a = (u + 1.0) * g * _sigmoid(g * SWIGLU_ALPHA)
        part = _dot(a.astype(BF16), wd_ref[...].astype(BF16))

        @pl.when(c == 0)
        def _():
            y_ref[...] = part + bd_ref[...]

        @pl.when(c > 0)
        def _():
            y_ref[...] = y_ref[...] + part


def _experts(xs, tile_e, n_used, w_gu, b_gu, w_dn, b_dn):
    n_slots, d = xs.shape
    n_tiles = n_slots // MOE_TILE
    n_ff = D_FF // MOE_FF_TILE
    tf = MOE_FF_TILE

    def last_used(i, nu):
        return jnp.minimum(i, nu[0] - 1)

    def chunk(i, c, nu):
        return jnp.where(i < nu[0], c, n_ff - 1)

    grid_spec = pltpu.PrefetchScalarGridSpec(
        num_scalar_prefetch=2,
        grid=(n_tiles, n_ff),
        in_specs=[
            pl.BlockSpec((MOE_TILE, d), lambda i, c, te, nu: (last_used(i, nu), 0)),
            pl.BlockSpec((None, d, tf), lambda i, c, te, nu: (te[i], 0, chunk(i, c, nu))),
            pl.BlockSpec((None, d, tf), lambda i, c, te, nu: (te[i], 0, n_ff + chunk(i, c, nu))),
            pl.BlockSpec((None, 1, tf), lambda i, c, te, nu: (te[i], 0, chunk(i, c, nu))),
            pl.BlockSpec((None, 1, tf), lambda i, c, te, nu: (te[i], 0, n_ff + chunk(i, c, nu))),
            pl.BlockSpec((None, tf, d), lambda i, c, te, nu: (te[i], chunk(i, c, nu), 0)),
            pl.BlockSpec((None, 1, d), lambda i, c, te, nu: (te[i], 0, 0)),
        ],
        out_specs=pl.BlockSpec((MOE_TILE, d), lambda i, c, te, nu: (i, 0)),
        scratch_shapes=[pltpu.VMEM((MOE_TILE, d), BF16)],
    )
    return pl.pallas_call(
        _experts_body,
        grid_spec=grid_spec,
        out_shape=jax.ShapeDtypeStruct((n_slots, d), F32),
        compiler_params=_cparams("arbitrary", "arbitrary"),
        name="moe_experts",
    )(tile_e, n_used, xs, w_gu, w_gu, b_gu, b_gu, w_dn, b_dn)


def _combine_body(start_ref, e_ref, rank_ref, x1_ref, route_ref, y_ref, o_ref, buf_ref, sem):
    tm = x1_ref.shape[0]

    def slot(r, k):
        return start_ref[e_ref[r * TOP_K + k]] + rank_ref[r * TOP_K + k]

    def issue(r, _):
        for k in range(TOP_K):
            _row_copy(y_ref, slot(r, k), buf_ref.at[k], r, sem).start()
        return 0

    def drain(r, _):
        for k in range(TOP_K):
            _row_copy(y_ref, slot(r, k), buf_ref.at[k], r, sem).wait()
        return 0

    lax.fori_loop(0, tm, issue, 0)
    lax.fori_loop(0, tm, drain, 0)
    route = route_ref[...]
    out = x1_ref[...]
    for k in range(TOP_K):
        out = out + route[:, TOP_K + k:TOP_K + k + 1] * buf_ref[k]
    o_ref[...] = out


def _combine(x1, route, y, e_flat, rank_flat, pad_start, tm=256):
    t, d = x1.shape
    smem = pl.BlockSpec((tm * TOP_K,), lambda i, *_: (i,), memory_space=pltpu.SMEM)
    grid_spec = pltpu.PrefetchScalarGridSpec(
        num_scalar_prefetch=1,
        grid=(t // tm,),
        in_specs=[smem, smem, pl.BlockSpec((tm, d), lambda i, *_: (i, 0)),
                  pl.BlockSpec((tm, LANES), lambda i, *_: (i, 0)),
                  pl.BlockSpec(memory_space=pl.ANY)],
        out_specs=pl.BlockSpec((tm, d), lambda i, *_: (i, 0)),
        scratch_shapes=[pltpu.VMEM((TOP_K, tm, d), F32), pltpu.SemaphoreType.DMA(())],
    )
    return pl.pallas_call(
        _combine_body,
        grid_spec=grid_spec,
        out_shape=jax.ShapeDtypeStruct((t, d), F32),
        compiler_params=_cparams("arbitrary"),
        name="moe_combine",
    )(pad_start, e_flat, rank_flat, x1, route, y)


def _moe(x1, h, route, counts, w_gu, b_gu, w_dn, b_dn):
    t = x1.shape[0]
    n_tiles = -(-t * TOP_K // MOE_TILE) + N_EXPERTS
    n_slots = n_tiles * MOE_TILE
    cnt = counts[0, :N_EXPERTS].astype(jnp.int32)
    padded = (cnt + MOE_TILE - 1) // MOE_TILE * MOE_TILE
    pad_end = jnp.cumsum(padded)
    pad_start = pad_end - padded
    n_used = (pad_end[-1] // MOE_TILE).reshape(1)
    tile_first = jnp.minimum(jnp.arange(n_tiles, dtype=jnp.int32), n_used[0] - 1) * MOE_TILE
    tile_e = jnp.minimum(jnp.searchsorted(pad_end, tile_first, side="right"), N_EXPERTS - 1).astype(jnp.int32)
    e_flat = route[:, :TOP_K].astype(jnp.int32).reshape(-1)
    rank_flat = route[:, 2 * TOP_K:3 * TOP_K].astype(jnp.int32).reshape(-1)
    xs = _dispatch(h, e_flat, rank_flat, pad_start, cnt, pad_end, n_slots)
    y = _experts(xs, tile_e, n_used, w_gu, b_gu.reshape(N_EXPERTS, 1, -1), w_dn, b_dn.reshape(N_EXPERTS, 1, -1))
    return _combine(x1, route, y, e_flat, rank_flat, pad_start)


def _layer(xp, xs, pools, kwbuf, vwbuf, s_ret, page_table, p):
    batch, seq, d = xp.shape
    dec_batch, t_new, _ = xs.shape
    n_pages = page_table.shape[1]
    past = n_pages * PAGE_SIZE
    n_prompt = batch * seq
    n_sample = dec_batch * t_new
    assert seq & (seq - 1) == 0 and t_new & (t_new - 1) == 0 and seq % SEL_KV_TILE == 0
    assert n_prompt % ROW_TILE == 0 and n_sample % ROW_TILE == 0 and t_new % 8 == 0

    x_all = jnp.concatenate([xp.reshape(n_prompt, d), xs.reshape(n_sample, d)], axis=0)
    t_all = n_prompt + n_sample
    xn = _ln1(x_all, p["ln1_g"])

    w_in = p["w_in"]
    offs = [0]
    for n in (NSA_Q, KV_W, KV_W, KV_W, KV_W, KV_W, KV_W, 3 * NSA_HEADS, RET_QK, RET_QK, RET_V, RET_V, D_MODEL, D_MODEL):
        offs.append(offs[-1] + n)
    w_q = w_in[:, offs[0]:offs[1]].astype(BF16)
    w_kv = jnp.concatenate([w_in[:, offs[1]:offs[8]],
                            jnp.zeros((d, LANES - 3 * NSA_HEADS), F32)], axis=1).astype(BF16)
    w_rot = w_in[:, offs[8]:offs[10]].astype(BF16)
    w_act = w_in[:, offs[10]:offs[14]].astype(BF16)

    def rows(n):
        return pl.BlockSpec((ROW_TILE, n), lambda j, i: (i, 0))

    q = _proj_call(_proj_q_body, xn, w_q, [p["q_norm_g"].reshape(1, HEAD_DIM)], [_vec_spec(HEAD_DIM)],
                   jax.ShapeDtypeStruct((t_all, NSA_Q), BF16), rows(NSA_Q), NSA_Q, "proj_q")
    kvf, kvb, gate = _proj_call(
        _proj_kv_body, xn, w_kv,
        [p["k_sel_norm_g"].reshape(1, HEAD_DIM), p["k_win_norm_g"].reshape(1, HEAD_DIM)],
        [_vec_spec(HEAD_DIM), _vec_spec(HEAD_DIM)],
        [jax.ShapeDtypeStruct((t_all, 6 * KV_W), F32), jax.ShapeDtypeStruct((t_all, 4 * KV_W), BF16),
         jax.ShapeDtypeStruct((t_all, LANES), F32)],
        [rows(6 * KV_W), rows(4 * KV_W), rows(LANES)], 6 * KV_W + LANES, "proj_kv")
    inv = 1.0 / (ROPE_BASE ** jnp.linspace(0.0, 1.0, RET_DK // 2, dtype=F32))
    rot = _proj_call(
        functools.partial(_proj_rot_body, n_prompt=n_prompt, seq=seq, past=past, dec_seq=t_new),
        xn, w_rot, [jnp.repeat(inv, 2).reshape(1, RET_DK)], [_vec_spec(RET_DK)],
        jax.ShapeDtypeStruct((t_all, 2 * RET_QK), BF16),
        pl.BlockSpec((ROW_TILE, RET_QK), lambda j, i: (i, j)), RET_QK, "proj_rot")
    act_tn = 1024
    act = _proj_call(
        functools.partial(_proj_act_body, tiles_plain=RET_V // act_tn, tiles_silu=RET_V // act_tn),
        xn, w_act, [], [], jax.ShapeDtypeStruct((t_all, 2 * RET_V + 2 * D_MODEL), BF16),
        pl.BlockSpec((ROW_TILE, act_tn), lambda j, i: (i, j)), act_tn, "proj_act")

    cmp_w = (_compress_weights(p["cmp_k_w1"], p["cmp_k_b1"], p["cmp_k_w2"])
             + _compress_weights(p["cmp_v_w1"], p["cmp_v_b1"], p["cmp_v_w2"])
             + (p["k_cmp_norm_g"].reshape(1, HEAD_DIM),))
    gn = p["ret_norm_g"].reshape(1, RET_V)

    kc_p, vc_p = _compress_prompt(kvf, cmp_w, batch, seq)
    o_nsa_p = _nsa_prompt(q, gate, kc_p, vc_p, kvb, batch, seq)
    o_ret_p, ret_p = _ret_prompt(rot, act, gn, batch, seq)

    pool_kc, pool_vc, pool_ks, pool_vs = [
        a.reshape(a.shape[0], NSA_GROUPS * PAGE_SIZE, HEAD_DIM) for a in pools]
    pt_flat = page_table.reshape(-1)
    kc_s, vc_s = _compress_sample(pool_kc, pool_vc, pt_flat, cmp_w, dec_batch, n_pages)
    wb = kwbuf.shape[1]
    o_nsa_s, kw_s, vw_s = _nsa_sample(
        q[n_prompt:].astype(F32), gate[n_prompt:], kvf[n_prompt:], kc_s, vc_s, pool_ks, pool_vs,
        kwbuf.reshape(dec_batch, wb * NSA_GROUPS, HEAD_DIM), vwbuf.reshape(dec_batch, wb * NSA_GROUPS, HEAD_DIM),
        pt_flat, dec_batch, t_new, n_pages)
    rot_s = rot[n_prompt:].astype(F32)
    act_s = act[n_prompt:, :2 * RET_V].astype(F32)
    o_ret_s, ret_s = _ret_sample(rot_s[:, :RET_QK], rot_s[:, RET_QK:], act_s[:, :RET_V], act_s[:, RET_V:],
                                 gn, s_ret, dec_batch, t_new)

    o_nsa = jnp.concatenate([o_nsa_p, o_nsa_s.astype(BF16)], axis=0)
    o_ret = jnp.concatenate([o_ret_p, o_ret_s.astype(BF16)], axis=0)
    mixed = _merge(o_nsa, o_ret, p["w_nsa_br"].astype(BF16), p["w_ret_br"].astype(BF16), act)
    rw = jnp.concatenate([p["router_w"], jnp.zeros((d, LANES - N_EXPERTS), F32)], axis=1)
    rwh, rwl = _split_bf16(rw)
    rb = jnp.concatenate([p["router_b"], jnp.zeros((LANES - N_EXPERTS,), F32)]).reshape(1, LANES)
    x1, h, route, counts = _out_router(x_all, mixed, p["w_out"].astype(BF16), p["ln2_g"].reshape(1, d), rwh, rwl, rb)
    y = _moe(x1, h, route, counts, p["w_gate_up"], p["b_gate_up"], p["w_down"], p["b_down"])

    kv5 = (NSA_GROUPS, HEAD_DIM)
    kvf_p = kvf[:n_prompt].reshape(batch, seq, 6, *kv5)
    kvf_s = kvf[n_prompt:].reshape(dec_batch, t_new, 6, *kv5)
    wbp = min(WINDOW, seq)
    states_p = (kvf_p[:, :, 0], kvf_p[:, :, 1], kvf_p[:, :, 2], kvf_p[:, :, 3],
                kvf_p[:, seq - wbp:, 4], kvf_p[:, seq - wbp:, 5], ret_p)
    states_s = (kvf_s[:, :, 0], kvf_s[:, :, 1], kvf_s[:, :, 2], kvf_s[:, :, 3],
                kw_s.reshape(dec_batch, wb, *kv5), vw_s.reshape(dec_batch, wb, *kv5), ret_s)
    return y[:n_prompt].reshape(batch, seq, d), y[n_prompt:].reshape(dec_batch, t_new, d), states_p, states_s


def kernel(x_prompt, x_sample, cache_k_cmp, cache_v_cmp, cache_k_sel, cache_v_sel, state_k_win, state_v_win,
           state_ret, page_table, ln1_g, w_in, cmp_k_w1, cmp_k_b1, cmp_k_w2, cmp_v_w1, cmp_v_b1, cmp_v_w2,
           q_norm_g, k_cmp_norm_g, k_sel_norm_g, k_win_norm_g, ret_norm_g, w_nsa_br, w_ret_br, w_out, ln2_g,
           router_w, router_b, w_gate_up, b_gate_up, w_down, b_down):
    depth = w_in.shape[0]
    y_p, y_s = x_prompt, x_sample
    new_p, new_s = [], []
    for l in range(depth):
        p = {
            "ln1_g": ln1_g[l], "w_in": w_in[l],
            "cmp_k_w1": cmp_k_w1[l], "cmp_k_b1": cmp_k_b1[l], "cmp_k_w2": cmp_k_w2[l],
            "cmp_v_w1": cmp_v_w1[l], "cmp_v_b1": cmp_v_b1[l], "cmp_v_w2": cmp_v_w2[l],
            "q_norm_g": q_norm_g[l], "k_cmp_norm_g": k_cmp_norm_g[l], "k_sel_norm_g": k_sel_norm_g[l],
            "k_win_norm_g": k_win_norm_g[l], "ret_norm_g": ret_norm_g[l],
            "w_nsa_br": w_nsa_br[l], "w_ret_br": w_ret_br[l], "w_out": w_out[l], "ln2_g": ln2_g[l],
            "router_w": router_w[l], "router_b": router_b[l], "w_gate_up": w_gate_up[l],
            "b_gate_up": b_gate_up[l], "w_down": w_down[l], "b_down": b_down[l],
        }
        pools = (cache_k_cmp[l], cache_v_cmp[l], cache_k_sel[l], cache_v_sel[l])
        y_p, y_s, sp, ss = _layer(y_p, y_s, pools, state_k_win[l], state_v_win[l], state_ret[l], page_table, p)
        new_p.append(sp)
        new_s.append(ss)
    outs_p = [jnp.stack(a) for a in zip(*new_p)]
    outs_s = [jnp.stack(a) for a in zip(*new_s)]
    return (y_p, y_s, *outs_p, *outs_s)
```

```python
import functools
import math

import jax
import jax.numpy as jnp
from jax import lax
from jax.experimental import pallas as pl
from jax.experimental.pallas import tpu as pltpu

F32 = jnp.float32
BF16 = jnp.bfloat16

D_MODEL = 2048
PAGE_SIZE = 128
NSA_HEADS = 8
NSA_GROUPS = 2
HPG = NSA_HEADS // NSA_GROUPS
HEAD_DIM = 128
CMP_LEN = 32
CMP_STRIDE = 16
CMP_HIDDEN = 2 * HEAD_DIM
SEL_BLK = 64
SEL_SHIFT = 6
SEL_TOPK = 16
WINDOW = 512
Q_BLK = 128
RET_HEADS = 8
RET_DK = 128
RET_DV = 256
RET_CHUNK = 128
ROPE_BASE = 10000.0
N_EXPERTS = 32
TOP_K = 4
D_FF = D_MODEL
SWIGLU_LIMIT = 7.0
SWIGLU_ALPHA = 1.702
EPS = 1e-6
NEG = -1e30
BIG = 1e30

NSA_Q = NSA_HEADS * HEAD_DIM
KV_W = NSA_GROUPS * HEAD_DIM
RET_QK = RET_HEADS * RET_DK
RET_V = RET_HEADS * RET_DV

LANES = 128
ROW_TILE = 512
SEL_KV_TILE = 512
MOE_TILE = 512
MOE_FF_TILE = 512
VMEM_LIMIT = 56 * 1024 * 1024


def _cparams(*sem):
    return pltpu.CompilerParams(dimension_semantics=sem, vmem_limit_bytes=VMEM_LIMIT)


def _dot(a, b):
    return jnp.dot(a, b, preferred_element_type=F32)


def _dot_nt(a, b):
    return lax.dot_general(a, b, (((1,), (1,)), ((), ())), preferred_element_type=F32)


def _dot_tn(a, b):
    return lax.dot_general(a, b, (((0,), (0,)), ((), ())), preferred_element_type=F32)


def _sigmoid(x):
    return 1.0 / (1.0 + jnp.exp(-x))


def _unit_rms(x):
    return x * lax.rsqrt(jnp.mean(x * x, axis=-1, keepdims=True) + EPS)


def _iota(shape, dim):
    return lax.broadcasted_iota(jnp.int32, shape, dim)


def _split_bf16(x):
    hi = x.astype(BF16)
    lo = (x - hi.astype(F32)).astype(BF16)
    return hi, lo


def _two_group_specs(tm, d, tiles_p):
    return [pl.BlockSpec((tm, d), lambda i: (jnp.minimum(i, tiles_p - 1), 0)),
            pl.BlockSpec((tm, d), lambda i: (jnp.maximum(i - tiles_p, 0), 0))]


def _ln1_body(xp_ref, xs_ref, g_ref, o_ref, *, tiles_p):
    def norm(x_ref):
        o_ref[...] = (_unit_rms(x_ref[...]) * g_ref[...]).astype(o_ref.dtype)

    pl.when(pl.program_id(0) < tiles_p)(lambda: norm(xp_ref))
    pl.when(pl.program_id(0) >= tiles_p)(lambda: norm(xs_ref))


def _ln1(xp, xs, g):
    d = xp.shape[1]
    tiles_p = xp.shape[0] // ROW_TILE
    t = xp.shape[0] + xs.shape[0]
    return pl.pallas_call(
        functools.partial(_ln1_body, tiles_p=tiles_p),
        grid=(t // ROW_TILE,),
        in_specs=_two_group_specs(ROW_TILE, d, tiles_p) + [pl.BlockSpec((1, d), lambda i: (0, 0))],
        out_specs=pl.BlockSpec((ROW_TILE, d), lambda i: (i, 0)),
        out_shape=jax.ShapeDtypeStruct((t, d), BF16),
        compiler_params=_cparams("arbitrary"),
        name="ln1",
    )(xp, xs, g.reshape(1, d))


def _proj_q_body(x_ref, w_ref, g_ref, o_ref):
    y = _dot(x_ref[...], w_ref[...])
    g = g_ref[...] * (HEAD_DIM ** -0.5)
    for c in range(NSA_HEADS):
        sl = slice(c * LANES, (c + 1) * LANES)
        o_ref[:, sl] = (_unit_rms(y[:, sl]) * g).astype(o_ref.dtype)


def _proj_kv_body(x_ref, w_ref, gs_ref, gw_ref, kc_ref, vc_ref, ks_ref, vs_ref, kw_ref, vw_ref, kvb_ref, gate_ref):
    y = _dot(x_ref[...], w_ref[...])
    tm = y.shape[0]
    outs = (kc_ref, vc_ref, ks_ref, vs_ref, kw_ref, vw_ref)
    for c in range(12):
        sl = slice(c * LANES, (c + 1) * LANES)
        yc = y[:, sl]
        if c in (4, 5):
            yc = _unit_rms(yc) * gs_ref[...]
        elif c in (8, 9):
            yc = _unit_rms(yc) * gw_ref[...]
        outs[c // NSA_GROUPS][pl.ds(c % NSA_GROUPS, tm, stride=NSA_GROUPS), :] = yc
        if c >= 4:
            kvb_ref[:, (c - 4) * LANES:(c - 3) * LANES] = yc.astype(BF16)
    gate_ref[...] = _sigmoid(y[:, 12 * LANES:13 * LANES])


def _proj_rot_body(x_ref, w_ref, inv_ref, o_ref, *, n_prompt, seq, past, dec_seq):
    j = pl.program_id(0)
    i = pl.program_id(1)
    y = _dot(x_ref[...], w_ref[...])
    tm = y.shape[0]
    row = i * tm + _iota((tm, LANES), 0)
    pos = jnp.where(row < n_prompt, row & (seq - 1), past + ((row - n_prompt) & (dec_seq - 1)))
    ang = pos.astype(F32) * inv_ref[...]
    cos = jnp.cos(ang)
    sin = jnp.sin(ang)
    even = (_iota((tm, LANES), 1) & 1) == 0
    sin = jnp.where(even, -sin, sin)
    scale = jnp.where(j == 0, 1.0, RET_DK ** -0.5).astype(F32)
    for c in range(RET_HEADS):
        sl = slice(c * LANES, (c + 1) * LANES)
        yc = y[:, sl]
        partner = jnp.where(even, pltpu.roll(yc, LANES - 1, 1), pltpu.roll(yc, 1, 1))
        o_ref[:, sl] = ((yc * cos + partner * sin) * scale).astype(o_ref.dtype)


def _proj_act_body(x_ref, w_ref, o_ref, *, tiles_plain, tiles_silu):
    j = pl.program_id(0)
    y = _dot(x_ref[...], w_ref[...])

    @pl.when(j < tiles_plain)
    def _():
        o_ref[...] = y.astype(o_ref.dtype)

    @pl.when((j >= tiles_plain) & (j < tiles_plain + tiles_silu))
    def _():
        o_ref[...] = (y * _sigmoid(y)).astype(o_ref.dtype)

    @pl.when(j >= tiles_plain + tiles_silu)
    def _():
        o_ref[...] = _sigmoid(y).astype(o_ref.dtype)


def _proj_call(body, xn, w, extra, extra_specs, out_shape, out_specs, tn, name):
    t, k = xn.shape
    n = w.shape[1]
    in_specs = [pl.BlockSpec((ROW_TILE, k), lambda j, i: (i, 0)),
                pl.BlockSpec((k, tn), lambda j, i: (0, j))] + extra_specs
    return pl.pallas_call(
        body,
        grid=(n // tn, t // ROW_TILE),
        in_specs=in_specs,
        out_specs=out_specs,
        out_shape=out_shape,
        compiler_params=_cparams("parallel", "parallel"),
        name=name,
    )(xn, w, *extra)


def _vec_spec(n):
    return pl.BlockSpec((1, n), lambda j, i: (0, 0))


def _compress_mlp(x_cat, w1_ref, b1_ref, w2_ref):
    nh = x_cat.shape[0]
    a = _dot(x_cat, w1_ref[...])
    pre = a[:, :CMP_HIDDEN] + pltpu.roll(a[:, CMP_HIDDEN:], nh - 1, 0) + b1_ref[...]
    hid = pre * (0.5 * (1.0 + jnp.tanh(math.sqrt(2.0 / math.pi) * (pre + 0.044715 * (pre * pre * pre)))))
    out = _dot(hid.astype(BF16), w2_ref[...])
    return jnp.where(_iota(out.shape, 0) < nh - 1, out, 0.0)


def _compress_finish(xk, xv, wrefs, kc_ref, vc_ref):
    w1k, b1k, w2k, w1v, b1v, w2v, gk = wrefs
    for g in range(NSA_GROUPS):
        sl = slice(g * LANES, (g + 1) * LANES)
        kc = _compress_mlp(xk[g], w1k, b1k, w2k)
        kc_ref[:, sl] = (_unit_rms(kc) * gk[...]).astype(kc_ref.dtype)
        vc_ref[:, sl] = _compress_mlp(xv[g], w1v, b1v, w2v).astype(vc_ref.dtype)


def _compress_prompt_body(krows_ref, vrows_ref, *refs, nh):
    wrefs, (kc_ref, vc_ref) = refs[:7], refs[7:]

    def gather(rows_ref, g):
        return jnp.concatenate(
            [rows_ref[pl.ds(NSA_GROUPS * s + g, nh, stride=NSA_GROUPS * CMP_STRIDE), :].astype(BF16)
             for s in range(CMP_STRIDE)], axis=1)

    xk = [gather(krows_ref, g) for g in range(NSA_GROUPS)]
    xv = [gather(vrows_ref, g) for g in range(NSA_GROUPS)]
    _compress_finish(xk, xv, wrefs, kc_ref, vc_ref)


def _compress_sample_body(pt_ref, *refs, n_pages):
    del pt_ref
    kpages, vpages = refs[:n_pages], refs[n_pages:2 * n_pages]
    wrefs, (kc_ref, vc_ref) = refs[2 * n_pages:2 * n_pages + 7], refs[2 * n_pages + 7:]
    per_page = PAGE_SIZE // CMP_STRIDE

    def gather(pages, g):
        cols = []
        for s in range(CMP_STRIDE):
            rows = [p[pl.ds(NSA_GROUPS * s + g, per_page, stride=NSA_GROUPS * CMP_STRIDE), :] for p in pages]
            cols.append(jnp.concatenate(rows, axis=0).astype(BF16))
        return jnp.concatenate(cols, axis=1)

    xk = [gather(kpages, g) for g in range(NSA_GROUPS)]
    xv = [gather(vpages, g) for g in range(NSA_GROUPS)]
    _compress_finish(xk, xv, wrefs, kc_ref, vc_ref)


def _const_spec(shape, nargs):
    zeros = (0,) * len(shape)
    return pl.BlockSpec(shape, lambda *a: zeros)


def _compress_weights(w1, b1, w2):
    r_n = CMP_LEN // CMP_STRIDE
    w1r = w1.reshape(r_n, CMP_STRIDE * HEAD_DIM, CMP_HIDDEN)
    w1cat = jnp.concatenate([w1r[r] for r in range(r_n)], axis=1).astype(BF16)
    return w1cat, b1.reshape(1, CMP_HIDDEN), w2.astype(BF16)


def _compress_weight_specs():
    k16 = CMP_STRIDE * HEAD_DIM
    one = [_const_spec((k16, 2 * CMP_HIDDEN), 0), _const_spec((1, CMP_HIDDEN), 0),
           _const_spec((CMP_HIDDEN, HEAD_DIM), 0)]
    return one + one + [_const_spec((1, HEAD_DIM), 0)]


def _compress_prompt(krows, vrows, weights, batch, seq):
    nh = seq // CMP_STRIDE
    out = jax.ShapeDtypeStruct((batch, nh, KV_W), BF16)
    ospec = pl.BlockSpec((None, nh, KV_W), lambda b: (b, 0, 0))
    return pl.pallas_call(
        functools.partial(_compress_prompt_body, nh=nh),
        grid=(batch,),
        in_specs=[pl.BlockSpec((NSA_GROUPS * seq, LANES), lambda b: (b, 0))] * 2 + _compress_weight_specs(),
        out_specs=[ospec, ospec],
        out_shape=[out, out],
        compiler_params=_cparams("parallel"),
        name="compress_prompt",
    )(krows, vrows, *weights)


def _page_specs(n_pages):
    def spec(p):
        return pl.BlockSpec((None, NSA_GROUPS * PAGE_SIZE, HEAD_DIM),
                            lambda b, pt: (pt[b * n_pages + p], 0, 0))
    return [spec(p) for p in range(n_pages)]


def _compress_sample(pool_k, pool_v, pt_flat, weights, dec_batch, n_pages):
    nh = n_pages * PAGE_SIZE // CMP_STRIDE
    out = jax.ShapeDtypeStruct((dec_batch, nh, KV_W), BF16)
    ospec = pl.BlockSpec((None, nh, KV_W), lambda b, pt: (b, 0, 0))
    grid_spec = pltpu.PrefetchScalarGridSpec(
        num_scalar_prefetch=1,
        grid=(dec_batch,),
        in_specs=_page_specs(n_pages) + _page_specs(n_pages) + _compress_weight_specs(),
        out_specs=[ospec, ospec],
    )
    return pl.pallas_call(
        functools.partial(_compress_sample_body, n_pages=n_pages),
        grid_spec=grid_spec,
        out_shape=[out, out],
        compiler_params=_cparams("parallel"),
        name="compress_sample",
    )(pt_flat, *([pool_k] * n_pages), *([pool_v] * n_pages), *weights)


def _masked_softmax(s3, ok):
    s3 = jnp.where(ok[None], s3, NEG)
    m = jnp.max(s3, axis=-1, keepdims=True)
    e = jnp.where(ok[None], jnp.exp(s3 - m), 0.0)
    den = jnp.sum(e, axis=-1, keepdims=True)
    return e / jnp.where(den > 0.0, den, 1.0)


def _topk_lanes(score, n_sel, k):
    lane = _iota(score.shape, 1)
    sc = jnp.where(lane < n_sel, score, -jnp.inf)
    rank = jnp.zeros(score.shape, F32)
    for i in range(n_sel):
        ci = sc[:, i:i + 1]
        later = jnp.where(lane > i, 1.0, 0.0)
        rank = rank + jnp.where(ci > sc, 1.0, jnp.where(ci == sc, later, 0.0))
    return jnp.where((rank < k) & (lane < n_sel), 1.0, 0.0)


def _bias_softmax_pv(s3, ok, v):
    h, t, n = s3.shape
    s3 = s3 + jnp.where(ok, 0.0, NEG)[None]
    e = jnp.exp(s3 - jnp.max(s3, axis=-1, keepdims=True))
    inv = 1.0 / jnp.sum(e, axis=-1, keepdims=True)
    return _dot(e.reshape(h * t, n).astype(BF16), v) * inv.reshape(h * t, 1)


def _compressed_branch(qg, kcg, vcg, qpos, n_cmp):
    t = qpos.shape[0]
    nc_pad = kcg.shape[0]
    c_idx = _iota((t, nc_pad), 1)
    vis = (c_idx * CMP_STRIDE + (CMP_LEN - 1) <= qpos) & (c_idx < n_cmp)
    p3 = _masked_softmax(_dot_nt(qg, kcg).reshape(HPG, t, nc_pad), vis)
    o_c = _dot(p3.reshape(HPG * t, nc_pad).astype(BF16), vcg)
    return o_c, jnp.sum(p3, axis=0)


def _overlap(nc_pad, blocks_first):
    shape = (LANES, nc_pad) if blocks_first else (nc_pad, LANES)
    ci = _iota(shape, 1 if blocks_first else 0) * CMP_STRIDE
    sj = _iota(shape, 0 if blocks_first else 1) * SEL_BLK
    return jnp.where((ci < sj + SEL_BLK) & (ci + CMP_LEN > sj), 1.0, 0.0).astype(BF16)


def _select_lanes(psum, qpos, n_sel):
    t, nc_pad = psum.shape
    hi, lo = _split_bf16(psum)
    ov = _overlap(nc_pad, False)
    score = _dot(hi, ov) + _dot(lo, ov)
    j = _iota((t, LANES), 1)
    cur = qpos >> SEL_SHIFT
    forced = (j == 0) | (j == cur) | (j == cur - 1)
    score = jnp.where(forced, BIG, score)
    score = jnp.where(j * SEL_BLK <= qpos, score, NEG)
    return _topk_lanes(score, n_sel, min(SEL_TOPK, n_sel))


def _select_sublanes(psum, start, n_sel):
    t, nc_pad = psum.shape
    hi, lo = _split_bf16(psum)
    ov = _overlap(nc_pad, True)
    rows = -(-n_sel // 8) * 8
    sc = (_dot_nt(ov, hi) + _dot_nt(ov, lo))[:rows]
    j = _iota((rows, t), 0)
    qp = start + _iota((rows, t), 1)
    cur = qp >> SEL_SHIFT
    forced = (j == 0) | (j == cur) | (j == cur - 1)
    sc = jnp.where(forced, BIG, sc)
    sc = jnp.where(j * SEL_BLK <= qp, sc, NEG)
    sc = jnp.where(j < n_sel, sc, -jnp.inf)
    rank = jnp.zeros((rows, t), F32)
    for i in range(n_sel):
        ri = sc[i:i + 1, :]
        later = jnp.where(j > i, 1.0, 0.0)
        rank = rank + jnp.where(ri > sc, 1.0, jnp.where(ri == sc, later, 0.0))
    sel = jnp.where((rank < min(SEL_TOPK, n_sel)) & (j < n_sel), 1.0, 0.0).astype(BF16)
    if rows < LANES:
        sel = jnp.concatenate([sel, jnp.zeros((LANES - rows, t), BF16)], axis=0)
    return sel


def _block_to_key(blk0, n):
    blk = blk0 + (_iota((LANES, n), 1) >> SEL_SHIFT)
    return jnp.where(_iota((LANES, n), 0) == blk, 1.0, 0.0).astype(BF16)


def _nsa_prompt_body(q_ref, gate_ref, kc_ref, vc_ref, kv_ref, o_ref, *, seq):
    tq = Q_BLK
    start = pl.program_id(1) * tq
    n_sel = seq // SEL_BLK
    n_cmp = seq // CMP_STRIDE - CMP_LEN // CMP_STRIDE + 1
    qpos = start + _iota((tq, 1), 0)
    gates = gate_ref[...]
    wk = WINDOW + tq
    for g in range(NSA_GROUPS):
        gsl = slice(g * LANES, (g + 1) * LANES)
        qg = jnp.concatenate(
            [q_ref[:, (g * HPG + j) * LANES:(g * HPG + j + 1) * LANES] for j in range(HPG)], axis=0)
        o_c, psum = _compressed_branch(qg, kc_ref[:, gsl], vc_ref[:, gsl], qpos, n_cmp)
        sel_t = _select_sublanes(psum, start, n_sel)

        def sel_step(k, carry):
            m, l, acc = carry
            k0 = pl.multiple_of(k * SEL_KV_TILE, SEL_KV_TILE)
            kt = kv_ref[pl.ds(k0, SEL_KV_TILE), pl.ds(g * LANES, LANES)]
            vt = kv_ref[pl.ds(k0, SEL_KV_TILE), pl.ds(KV_W + g * LANES, LANES)]
            chosen = _dot_tn(sel_t, _block_to_key(k * (SEL_KV_TILE // SEL_BLK), SEL_KV_TILE))
            ok = (chosen > 0.5) & (k0 + _iota((tq, SEL_KV_TILE), 1) <= qpos)
            s3 = _dot_nt(qg, kt).reshape(HPG, tq, SEL_KV_TILE) + jnp.where(ok, 0.0, NEG)[None]
            m_new = jnp.maximum(m, jnp.max(s3, axis=-1, keepdims=True))
            alpha = jnp.exp(m - m_new)
            p = jnp.exp(s3 - m_new)
            l = alpha * l + jnp.sum(p, axis=-1, keepdims=True)
            pv = _dot(p.reshape(HPG * tq, SEL_KV_TILE).astype(BF16), vt)
            return m_new, l, alpha * acc + pv.reshape(HPG, tq, LANES)

        n_kv = (start + tq + SEL_KV_TILE - 1) // SEL_KV_TILE
        init = (jnp.full((HPG, tq, 1), NEG, F32), jnp.zeros((HPG, tq, 1), F32),
                jnp.zeros((HPG, tq, LANES), F32))
        _, l_s, acc_s = lax.fori_loop(0, n_kv, sel_step, init)
        o_s = acc_s * (1.0 / l_s)

        w0 = pl.multiple_of(jnp.maximum(start - WINDOW, 0), Q_BLK)
        kw = kv_ref[pl.ds(w0, wk), pl.ds(2 * KV_W + g * LANES, LANES)]
        vw = kv_ref[pl.ds(w0, wk), pl.ds(3 * KV_W + g * LANES, LANES)]
        dpos = qpos - (w0 + _iota((tq, wk), 1))
        o_w = _bias_softmax_pv(_dot_nt(qg, kw).reshape(HPG, tq, wk), (dpos >= 0) & (dpos < WINDOW), vw)

        for j in range(HPG):
            h = g * HPG + j
            rows = slice(j * tq, (j + 1) * tq)
            o = (o_c[rows] * gates[:, 3 * h:3 * h + 1] + o_s[j] * gates[:, 3 * h + 1:3 * h + 2]
                 + o_w[rows] * gates[:, 3 * h + 2:3 * h + 3])
            o_ref[:, h * LANES:(h + 1) * LANES] = o.astype(o_ref.dtype)


def _nsa_prompt(q, gate, kc, vc, kvb, batch, seq):
    nq = seq // Q_BLK
    nh = seq // CMP_STRIDE
    return pl.pallas_call(
        functools.partial(_nsa_prompt_body, seq=seq),
        grid=(batch, nq),
        in_specs=[pl.BlockSpec((Q_BLK, NSA_Q), lambda b, i: (b * nq + i, 0)),
                  pl.BlockSpec((Q_BLK, LANES), lambda b, i: (b * nq + i, 0)),
                  pl.BlockSpec((None, nh, KV_W), lambda b, i: (b, 0, 0)),
                  pl.BlockSpec((None, nh, KV_W), lambda b, i: (b, 0, 0)),
                  pl.BlockSpec((seq, 4 * KV_W), lambda b, i: (b, 0))],
        out_specs=pl.BlockSpec((Q_BLK, NSA_Q), lambda b, i: (b * nq + i, 0)),
        out_shape=jax.ShapeDtypeStruct((batch * seq, NSA_Q), BF16),
        compiler_params=_cparams("parallel", "parallel"),
        name="nsa_prompt",
    )(q, gate, kc, vc, kvb)


def _nsa_sample_body(pt_ref, q_ref, gate_ref, ksn_ref, vsn_ref, kwn_ref, vwn_ref, kc_ref, vc_ref, *refs,
                     n_pages, past):
    del pt_ref
    kpages, vpages = refs[:n_pages], refs[n_pages:2 * n_pages]
    kwbuf_ref, vwbuf_ref, o_ref, kwout_ref, vwout_ref = refs[2 * n_pages:]
    new_refs = {2: ksn_ref, 3: vsn_ref, 4: kwn_ref, 5: vwn_ref}
    t = q_ref.shape[0]
    wb = kwbuf_ref.shape[0] // NSA_GROUPS
    n_past = past // SEL_BLK
    n_sel = n_past + -(-t // SEL_BLK)
    n_cmp = (past + t) // CMP_STRIDE - CMP_LEN // CMP_STRIDE + 1
    qpos = past + _iota((t, 1), 0)
    gates = gate_ref[...]
    q = q_ref[...]
    pad = jnp.zeros((LANES - t, LANES), BF16)
    n_keys = past + LANES
    wk = wb + LANES
    for g in range(NSA_GROUPS):
        gsl = slice(g * LANES, (g + 1) * LANES)

        def new_rows(which):
            return new_refs[which][pl.ds(g, t, stride=NSA_GROUPS), :]

        qg = jnp.concatenate(
            [q[:, (g * HPG + j) * LANES:(g * HPG + j + 1) * LANES] for j in range(HPG)], axis=0).astype(BF16)
        o_c, psum = _compressed_branch(qg, kc_ref[:, gsl], vc_ref[:, gsl], qpos, n_cmp)
        sel = _select_lanes(psum, qpos, n_sel)

        def keys(pages, which):
            past_rows = [p[pl.ds(g, PAGE_SIZE, stride=NSA_GROUPS), :].astype(BF16) for p in pages]
            return jnp.concatenate(past_rows + [new_rows(which).astype(BF16), pad], axis=0)

        ks, vs = keys(kpages, 2), keys(vpages, 3)
        chosen = _dot(sel.astype(BF16), _block_to_key(0, n_keys))
        ok = (chosen > 0.5) & (_iota((t, n_keys), 1) <= qpos)
        o_s = _bias_softmax_pv(_dot_nt(qg, ks).reshape(HPG, t, n_keys), ok, vs)

        def window(buf_ref, which):
            return jnp.concatenate([buf_ref[pl.ds(g, wb, stride=NSA_GROUPS), :].astype(BF16),
                                    new_rows(which).astype(BF16), pad], axis=0)

        kw, vw = window(kwbuf_ref, 4), window(vwbuf_ref, 5)
        dpos = qpos - (past - wb + _iota((t, wk), 1))
        o_w = _bias_softmax_pv(_dot_nt(qg, kw).reshape(HPG, t, wk), (dpos >= 0) & (dpos < WINDOW), vw)

        for j in range(HPG):
            h = g * HPG + j
            rows = slice(j * t, (j + 1) * t)
            o = (o_c[rows] * gates[:, 3 * h:3 * h + 1] + o_s[rows] * gates[:, 3 * h + 1:3 * h + 2]
                 + o_w[rows] * gates[:, 3 * h + 2:3 * h + 3])
            o_ref[:, h * LANES:(h + 1) * LANES] = o

        kwout_ref[pl.ds((wb - t) * NSA_GROUPS + g, t, stride=NSA_GROUPS), :] = new_rows(4)
        vwout_ref[pl.ds((wb - t) * NSA_GROUPS + g, t, stride=NSA_GROUPS), :] = new_rows(5)
    keep = (wb - t) * NSA_GROUPS
    kwout_ref[pl.ds(0, keep), :] = kwbuf_ref[pl.ds(t * NSA_GROUPS, keep), :]
    vwout_ref[pl.ds(0, keep), :] = vwbuf_ref[pl.ds(t * NSA_GROUPS, keep), :]


def _nsa_sample(q, gate, new, new_block0, kc, vc, pool_k, pool_v, kwbuf, vwbuf, pt_flat, dec_batch, t, n_pages):
    past = n_pages * PAGE_SIZE
    nh = past // CMP_STRIDE
    wrows = kwbuf.shape[1]

    def rows(n):
        return pl.BlockSpec((t, n), lambda b, pt: (b, 0))

    new_spec = pl.BlockSpec((NSA_GROUPS * t, HEAD_DIM), lambda b, pt: (new_block0 + b, 0))

    def per_b(r, c):
        return pl.BlockSpec((None, r, c), lambda b, pt: (b, 0, 0))

    grid_spec = pltpu.PrefetchScalarGridSpec(
        num_scalar_prefetch=1,
        grid=(dec_batch,),
        in_specs=[rows(NSA_Q), rows(LANES)] + [new_spec] * 4 + [per_b(nh, KV_W), per_b(nh, KV_W)]
        + _page_specs(n_pages) + _page_specs(n_pages) + [per_b(wrows, HEAD_DIM), per_b(wrows, HEAD_DIM)],
        out_specs=[rows(NSA_Q), per_b(wrows, HEAD_DIM), per_b(wrows, HEAD_DIM)],
    )
    wout = jax.ShapeDtypeStruct(kwbuf.shape, F32)
    return pl.pallas_call(
        functools.partial(_nsa_sample_body, n_pages=n_pages, past=past),
        grid_spec=grid_spec,
        out_shape=[jax.ShapeDtypeStruct((dec_batch * t, NSA_Q), F32), wout, wout],
        compiler_params=_cparams("parallel"),
        name="nsa_sample",
    )(pt_flat, q, gate, *new, kc, vc, *([pool_k] * n_pages), *([pool_v] * n_pages), kwbuf, vwbuf)


def _log_decay(h):
    return math.log(1.0 - 2.0 ** (-5.0 - h))


def _ret_finish(o, h, gn_ref, rg):
    sl = slice(h * RET_DV, (h + 1) * RET_DV)
    return _unit_rms(o) * gn_ref[:, sl] * rg[:, sl]


def _ret_prompt_body(rq_ref, rk_ref, rv_ref, rg_ref, gn_ref, o_ref, s_ref):
    c = RET_CHUNK

    @pl.when(pl.program_id(1) == 0)
    def _():
        s_ref[...] = jnp.zeros(s_ref.shape, F32)

    n_col = _iota((c, 1), 0).astype(F32)
    diff = (_iota((c, c), 0) - _iota((c, c), 1)).astype(F32)
    rg = rg_ref[...].astype(F32)
    for h in range(RET_HEADS):
        lg = _log_decay(h)
        q = rq_ref[:, h * RET_DK:(h + 1) * RET_DK]
        k = rk_ref[:, h * RET_DK:(h + 1) * RET_DK]
        v = rv_ref[:, h * RET_DV:(h + 1) * RET_DV]
        s_prev = s_ref[h]
        dmask = jnp.where(diff >= 0.0, jnp.exp(jnp.maximum(diff, 0.0) * lg), 0.0)
        o = _dot((_dot_nt(q, k) * dmask).astype(BF16), v)
        q_dec = (q.astype(F32) * jnp.exp((n_col + 1.0) * lg)).astype(BF16)
        o = o + _dot(q_dec, s_prev.astype(BF16))
        k_dec = (k.astype(F32) * jnp.exp((c - 1.0 - n_col) * lg)).astype(BF16)
        s_ref[h] = math.exp(c * lg) * s_prev + _dot_tn(k_dec, v)
        o_ref[:, h * RET_DV:(h + 1) * RET_DV] = _ret_finish(o, h, gn_ref, rg).astype(o_ref.dtype)


def _ret_prompt(rot, act, gn, batch, seq):
    nch = seq // RET_CHUNK
    qk_w = RET_QK

    def rows(n, col):
        return pl.BlockSpec((RET_CHUNK, n), lambda b, i: (b * nch + i, col))

    return pl.pallas_call(
        _ret_prompt_body,
        grid=(batch, nch),
        in_specs=[rows(qk_w, 0), rows(qk_w, 1), rows(RET_V, 0), rows(RET_V, 1),
                  pl.BlockSpec((1, RET_V), lambda b, i: (0, 0))],
        out_specs=[rows(RET_V, 0),
                   pl.BlockSpec((None, RET_HEADS, RET_DK, RET_DV), lambda b, i: (b, 0, 0, 0))],
        out_shape=[jax.ShapeDtypeStruct((batch * seq, RET_V), BF16),
                   jax.ShapeDtypeStruct((batch, RET_HEADS, RET_DK, RET_DV), F32)],
        compiler_params=_cparams("parallel", "arbitrary"),
        name="retention_prompt",
    )(rot, rot, act, act, gn)


def _ret_sample_body(rq_ref, rk_ref, rv_ref, rg_ref, gn_ref, s_ref, o_ref, so_ref):
    c = rq_ref.shape[0]
    n_col = _iota((c, 1), 0).astype(F32)
    rg = rg_ref[...]
    zk = jnp.zeros((LANES - c, RET_DK), BF16)
    zv = jnp.zeros((LANES - c, RET_DV), BF16)
    for h in range(RET_HEADS):
        lg = _log_decay(h)
        q = rq_ref[:, h * RET_DK:(h + 1) * RET_DK]
        k = rk_ref[:, h * RET_DK:(h + 1) * RET_DK]
        v = rv_ref[:, h * RET_DV:(h + 1) * RET_DV]
        s_prev = s_ref[h]
        o = _dot((q * jnp.exp((n_col + 1.0) * lg)).astype(BF16), s_prev.astype(BF16))
        for j in range(c):
            qk = jnp.sum(q * k[j:j + 1, :], axis=-1, keepdims=True)
            dj = jnp.where(n_col >= j, jnp.exp(jnp.maximum(n_col - j, 0.0) * lg), 0.0)
            o = o + (qk * dj) * v[j:j + 1, :]
        k_dec = jnp.concatenate([(k * jnp.exp((c - 1.0 - n_col) * lg)).astype(BF16), zk], axis=0)
        v_pad = jnp.concatenate([v.astype(BF16), zv], axis=0)
        so_ref[h] = math.exp(c * lg) * s_prev + _dot_tn(k_dec, v_pad)
        o_ref[:, h * RET_DV:(h + 1) * RET_DV] = _ret_finish(o, h, gn_ref, rg)


def _ret_sample(rq, rk, rv, rg, gn, state, dec_batch, t):
    def rows(n):
        return pl.BlockSpec((t, n), lambda b: (b, 0))

    sspec = pl.BlockSpec((None, RET_HEADS, RET_DK, RET_DV), lambda b: (b, 0, 0, 0))
    return pl.pallas_call(
        _ret_sample_body,
        grid=(dec_batch,),
        in_specs=[rows(RET_QK), rows(RET_QK), rows(RET_V), rows(RET_V),
                  pl.BlockSpec((1, RET_V), lambda b: (0, 0)), sspec],
        out_specs=[rows(RET_V), sspec],
        out_shape=[jax.ShapeDtypeStruct((dec_batch * t, RET_V), F32),
                   jax.ShapeDtypeStruct(state.shape, F32)],
        compiler_params=_cparams("parallel"),
        name="retention_sample",
    )(rq, rk, rv, rg, gn, state)


def _merge_body(on_ref, or_ref, wn_ref, wr_ref, ga_ref, gr_ref, o_ref):
    a = _dot(on_ref[...], wn_ref[...])
    r = _dot(or_ref[...], wr_ref[...])
    o_ref[...] = (ga_ref[...].astype(F32) * a + gr_ref[...].astype(F32) * r).astype(o_ref.dtype)


def _merge(o_nsa, o_ret, wn, wr, act, tn=1024):
    t = o_nsa.shape[0]
    d = wn.shape[1]
    ga0 = (RET_V + RET_V) // tn
    gr0 = (RET_V + RET_V + D_MODEL) // tn
    return pl.pallas_call(
        _merge_body,
        grid=(d // tn, t // ROW_TILE),
        in_specs=[pl.BlockSpec((ROW_TILE, NSA_Q), lambda j, i: (i, 0)),
                  pl.BlockSpec((ROW_TILE, RET_V), lambda j, i: (i, 0)),
                  pl.BlockSpec((NSA_Q, tn), lambda j, i: (0, j)),
                  pl.BlockSpec((RET_V, tn), lambda j, i: (0, j)),
                  pl.BlockSpec((ROW_TILE, tn), lambda j, i: (i, ga0 + j)),
                  pl.BlockSpec((ROW_TILE, tn), lambda j, i: (i, gr0 + j))],
        out_specs=pl.BlockSpec((ROW_TILE, tn), lambda j, i: (i, j)),
        out_shape=jax.ShapeDtypeStruct((t, d), BF16),
        compiler_params=_cparams("parallel", "parallel"),
        name="merge",
    )(o_nsa, o_ret, wn, wr, act, act)


def _out_router_body(xp_ref, xs_ref, mix_ref, wo_ref, g_ref, rwh_ref, rwl_ref, rb_ref,
                     x1_ref, h_ref, route_ref, cnt_ref, *, tiles_p):
    @pl.when(pl.program_id(0) == 0)
    def _():
        cnt_ref[...] = jnp.zeros(cnt_ref.shape, F32)

    x = jnp.where(pl.program_id(0) < tiles_p, xp_ref[...], xs_ref[...])
    x1 = x + _dot(mix_ref[...], wo_ref[...])
    x1_ref[...] = x1
    h = _unit_rms(x1) * g_ref[...]
    h_ref[...] = h
    tm = h.shape[0]
    lane = _iota((tm, LANES), 1)
    hi, lo = _split_bf16(h)
    logits = _dot(hi, rwh_ref[...]) + _dot(lo, rwh_ref[...]) + _dot(hi, rwl_ref[...]) + rb_ref[...]
    work = jnp.where(lane < N_EXPERTS, logits, -jnp.inf)
    vals, idxs = [], []
    for _ in range(TOP_K):
        v = jnp.max(work, axis=-1, keepdims=True)
        ix = jnp.min(jnp.where(work == v, lane, LANES), axis=-1, keepdims=True)
        vals.append(v)
        idxs.append(ix)
        work = jnp.where(lane == ix, -jnp.inf, work)
    es = [jnp.exp(v - vals[0]) for v in vals]
    den = es[0] + es[1] + es[2] + es[3]
    hot = jnp.zeros((tm, LANES), F32)
    for ix in idxs:
        hot = hot + jnp.where(lane == ix, 1.0, 0.0)
    before = jnp.where(_iota((tm, tm), 1) < _iota((tm, tm), 0), 1.0, 0.0).astype(BF16)
    ranks = _dot(before, hot.astype(BF16)) + cnt_ref[...]
    route = jnp.zeros((tm, LANES), F32)
    for k in range(TOP_K):
        rk = jnp.sum(jnp.where(lane == idxs[k], ranks, 0.0), axis=-1, keepdims=True)
        route = route + jnp.where(lane == k, idxs[k].astype(F32), 0.0)
        route = route + jnp.where(lane == TOP_K + k, es[k] / den, 0.0)
        route = route + jnp.where(lane == 2 * TOP_K + k, rk, 0.0)
    route_ref[...] = route
    cnt_ref[...] = cnt_ref[...] + jnp.sum(hot, axis=0, keepdims=True)


def _out_router(xp, xs, mixed, wo, g2, rwh, rwl, rb, tm=256):
    d = xp.shape[1]
    t = xp.shape[0] + xs.shape[0]
    tiles_p = xp.shape[0] // tm
    rows = pl.BlockSpec((tm, d), lambda i: (i, 0))
    lanes = pl.BlockSpec((tm, LANES), lambda i: (i, 0))

    def const(r, c):
        return pl.BlockSpec((r, c), lambda i: (0, 0))

    return pl.pallas_call(
        functools.partial(_out_router_body, tiles_p=tiles_p),
        grid=(t // tm,),
        in_specs=_two_group_specs(tm, d, tiles_p)
        + [rows, const(d, d), const(1, d), const(d, LANES), const(d, LANES), const(1, LANES)],
        out_specs=[rows, rows, lanes, const(1, LANES)],
        out_shape=[jax.ShapeDtypeStruct((t, d), F32), jax.ShapeDtypeStruct((t, d), F32),
                   jax.ShapeDtypeStruct((t, LANES), F32), jax.ShapeDtypeStruct((1, LANES), F32)],
        compiler_params=_cparams("arbitrary"),
        name="out_router",
    )(xp, xs, mixed, wo, g2, rwh, rwl, rb)


def _row_copy(src, r_src, dst, r_dst, sem):
    return pltpu.make_async_copy(src.at[pl.ds(r_src, 1), :], dst.at[pl.ds(r_dst, 1), :], sem)


def _dispatch_body(start_ref, cnt_ref, pend_ref, e_ref, rank_ref, h_ref, xs_ref, zero_ref, sem, zsem):
    tm = h_ref.shape[0]

    def slot(r, k):
        return start_ref[e_ref[r * TOP_K + k]] + rank_ref[r * TOP_K + k]

    def issue(r, _):
        for k in range(TOP_K):
            _row_copy(h_ref, r, xs_ref, slot(r, k), sem).start()
        return 0

    def drain(r, _):
        for k in range(TOP_K):
            _row_copy(h_ref, r, xs_ref, slot(r, k), sem).wait()
        return 0

    lax.fori_loop(0, tm, issue, 0)

    @pl.when(pl.program_id(0) == 0)
    def _():
        zero_ref[...] = jnp.zeros(zero_ref.shape, F32)
        zrows = zero_ref.shape[0]
        for phase in ("start", "wait"):
            def per_expert(e, _):
                def per_row(s, _):
                    cp = _row_copy(zero_ref, 0, xs_ref, s, zsem)
                    cp.start() if phase == "start" else cp.wait()
                    return 0
                return lax.fori_loop(start_ref[e] + cnt_ref[e], pend_ref[e], per_row, 0)
            lax.fori_loop(0, N_EXPERTS, per_expert, 0)

            def per_chunk(s, _):
                s0 = pl.multiple_of(s * zrows, zrows)
                cp = pltpu.make_async_copy(zero_ref, xs_ref.at[pl.ds(s0, zrows), :], zsem)
                cp.start() if phase == "start" else cp.wait()
                return 0
            lax.fori_loop(pend_ref[N_EXPERTS - 1] // zrows, xs_ref.shape[0] // zrows, per_chunk, 0)

    lax.fori_loop(0, tm, drain, 0)


def _dispatch(h, e_flat, rank_flat, pad_start, counts, pad_end, n_slots, tm=256):
    t, d = h.shape
    smem = pl.BlockSpec((tm * TOP_K,), lambda i, *_: (i,), memory_space=pltpu.SMEM)
    grid_spec = pltpu.PrefetchScalarGridSpec(
        num_scalar_prefetch=3,
        grid=(t // tm,),
        in_specs=[smem, smem, pl.BlockSpec((tm, d), lambda i, *_: (i, 0))],
        out_specs=pl.BlockSpec(memory_space=pl.ANY),
        scratch_shapes=[pltpu.VMEM((MOE_TILE // 8, d), F32), pltpu.SemaphoreType.DMA(()),
                        pltpu.SemaphoreType.DMA(())],
    )
    return pl.pallas_call(
        _dispatch_body,
        grid_spec=grid_spec,
        out_shape=jax.ShapeDtypeStruct((n_slots, d), F32),
        compiler_params=pltpu.CompilerParams(dimension_semantics=("arbitrary",), vmem_limit_bytes=VMEM_LIMIT,
                                             has_side_effects=True),
        name="moe_dispatch",
    )(pad_start, counts, pad_end, e_flat, rank_flat, h)


def _experts_body(te_ref, nu_ref, xs_ref, wg_ref, wu_ref, bg_ref, bu_ref, wd_ref, bd_ref, y_ref, xb_ref):
    del te_ref
    i = pl.program_id(0)
    c = pl.program_id(1)
    used = i < nu_ref[0]

    @pl.when(used & (c == 0))
    def _():
        xb_ref[...] = xs_ref[...].astype(BF16)

    @pl.when(jnp.logical_not(used) & (c == 0))
    def _():
        y_ref[...] = jnp.zeros(y_ref.shape, F32)

    @pl.when(used)
    def _():
        x = xb_ref[...]
        g = jnp.minimum(_dot(x, wg_ref[...].astype(BF16)) + bg_ref[...], SWIGLU_LIMIT)
        u = jnp.clip(_dot(x, wu_ref[...].astype(BF16)) + bu_ref[...], -SWIGLU_LIMIT, SWIGLU_LIMIT)
        a = (u + 1.0) * g * _sigmoid(g * SWIGLU_ALPHA)
        part = _dot(a.astype(BF16), wd_ref[...].astype(BF16))

        @pl.when(c == 0)
        def _():
            y_ref[...] = part + bd_ref[...]

        @pl.when(c > 0)
        def _():
            y_ref[...] = y_ref[...] + part


def _experts(xs, tile_e, n_used, w_gu, b_gu, w_dn, b_dn):
    n_slots, d = xs.shape
    n_tiles = n_slots // MOE_TILE
    n_ff = D_FF // MOE_FF_TILE
    tf = MOE_FF_TILE

    def last_used(i, nu):
        return jnp.minimum(i, nu[0] - 1)

    def chunk(i, c, nu):
        return jnp.where(i < nu[0], c, n_ff - 1)

    grid_spec = pltpu.PrefetchScalarGridSpec(
        num_scalar_prefetch=2,
        grid=(n_tiles, n_ff),
        in_specs=[
            pl.BlockSpec((MOE_TILE, d), lambda i, c, te, nu: (last_used(i, nu), 0)),
            pl.BlockSpec((None, d, tf), lambda i, c, te, nu: (te[i], 0, chunk(i, c, nu))),
            pl.BlockSpec((None, d, tf), lambda i, c, te, nu: (te[i], 0, n_ff + chunk(i, c, nu))),
            pl.BlockSpec((None, 1, tf), lambda i, c, te, nu: (te[i], 0, chunk(i, c, nu))),
            pl.BlockSpec((None, 1, tf), lambda i, c, te, nu: (te[i], 0, n_ff + chunk(i, c, nu))),
            pl.BlockSpec((None, tf, d), lambda i, c, te, nu: (te[i], chunk(i, c, nu), 0)),
            pl.BlockSpec((None, 1, d), lambda i, c, te, nu: (te[i], 0, 0)),
        ],
        out_specs=pl.BlockSpec((MOE_TILE, d), lambda i, c, te, nu: (i, 0)),
        scratch_shapes=[pltpu.VMEM((MOE_TILE, d), BF16)],
    )
    return pl.pallas_call(
        _experts_body,
        grid_spec=grid_spec,
        out_shape=jax.ShapeDtypeStruct((n_slots, d), F32),
        compiler_params=_cparams("arbitrary", "arbitrary"),
        name="moe_experts",
    )(tile_e, n_used, xs, w_gu, w_gu, b_gu, b_gu, w_dn, b_dn)


def _combine_body(start_ref, e_ref, rank_ref, x1_ref, route_ref, y_ref, op_ref, os_ref, buf_ref, sem, *, tiles_p):
    tm = x1_ref.shape[0]

    def slot(r, k):
        return start_ref[e_ref[r * TOP_K + k]] + rank_ref[r * TOP_K + k]

    def issue(r, _):
        for k in range(TOP_K):
            _row_copy(y_ref, slot(r, k), buf_ref.at[k], r, sem).start()
        return 0

    def drain(r, _):
        for k in range(TOP_K):
            _row_copy(y_ref, slot(r, k), buf_ref.at[k], r, sem).wait()
        return 0

    lax.fori_loop(0, tm, issue, 0)
    lax.fori_loop(0, tm, drain, 0)
    route = route_ref[...]
    out = x1_ref[...]
    for k in range(TOP_K):
        out = out + route[:, TOP_K + k:TOP_K + k + 1] * buf_ref[k]

    @pl.when(pl.program_id(0) < tiles_p)
    def _():
        op_ref[...] = out

    @pl.when(pl.program_id(0) >= tiles_p)
    def _():
        os_ref[...] = out


def _combine(x1, route, y, e_flat, rank_flat, pad_start, n_prompt, tm=256):
    t, d = x1.shape
    tiles_p = n_prompt // tm
    smem = pl.BlockSpec((tm * TOP_K,), lambda i, *_: (i,), memory_space=pltpu.SMEM)
    grid_spec = pltpu.PrefetchScalarGridSpec(
        num_scalar_prefetch=1,
        grid=(t // tm,),
        in_specs=[smem, smem, pl.BlockSpec((tm, d), lambda i, *_: (i, 0)),
                  pl.BlockSpec((tm, LANES), lambda i, *_: (i, 0)),
                  pl.BlockSpec(memory_space=pl.ANY)],
        out_specs=[pl.BlockSpec((tm, d), lambda i, *_: (jnp.minimum(i, tiles_p - 1), 0)),
                   pl.BlockSpec((tm, d), lambda i, *_: (jnp.maximum(i - tiles_p, 0), 0))],
        scratch_shapes=[pltpu.VMEM((TOP_K, tm, d), F32), pltpu.SemaphoreType.DMA(())],
    )
    return pl.pallas_call(
        functools.partial(_combine_body, tiles_p=tiles_p),
        grid_spec=grid_spec,
        out_shape=[jax.ShapeDtypeStruct((n_prompt, d), F32), jax.ShapeDtypeStruct((t - n_prompt, d), F32)],
        compiler_params=_cparams("arbitrary"),
        name="moe_combine",
    )(pad_start, e_flat, rank_flat, x1, route, y)


def _moe(x1, h, route, counts, w_gu, b_gu, w_dn, b_dn, n_prompt):
    t = x1.shape[0]
    n_tiles = -(-t * TOP_K // MOE_TILE) + N_EXPERTS
    n_slots = n_tiles * MOE_TILE
    cnt = counts[0, :N_EXPERTS].astype(jnp.int32)
    padded = (cnt + MOE_TILE - 1) // MOE_TILE * MOE_TILE
    pad_end = jnp.cumsum(padded)
    pad_start = pad_end - padded
    n_used = (pad_end[-1] // MOE_TILE).reshape(1)
    tile_first = jnp.minimum(jnp.arange(n_tiles, dtype=jnp.int32), n_used[0] - 1) * MOE_TILE
    tile_e = jnp.minimum(jnp.searchsorted(pad_end, tile_first, side="right"), N_EXPERTS - 1).astype(jnp.int32)
    e_flat = route[:, :TOP_K].astype(jnp.int32).reshape(-1)
    rank_flat = route[:, 2 * TOP_K:3 * TOP_K].astype(jnp.int32).reshape(-1)
    xs = _dispatch(h, e_flat, rank_flat, pad_start, cnt, pad_end, n_slots)
    y = _experts(xs, tile_e, n_used, w_gu, b_gu.reshape(N_EXPERTS, 1, -1), w_dn, b_dn.reshape(N_EXPERTS, 1, -1))
    return _combine(x1, route, y, e_flat, rank_flat, pad_start, n_prompt)


def _layer(xp, xs, pools, kwbuf, vwbuf, s_ret, page_table, p):
    batch, seq, d = xp.shape
    dec_batch, t_new, _ = xs.shape
    n_pages = page_table.shape[1]
    past = n_pages * PAGE_SIZE
    n_prompt = batch * seq
    n_sample = dec_batch * t_new
    assert seq & (seq - 1) == 0 and t_new & (t_new - 1) == 0 and seq % SEL_KV_TILE == 0
    assert n_prompt % ROW_TILE == 0 and n_sample % ROW_TILE == 0 and t_new % 8 == 0

    xp2, xs2 = xp.reshape(n_prompt, d), xs.reshape(n_sample, d)
    t_all = n_prompt + n_sample
    xn = _ln1(xp2, xs2, p["ln1_g"])

    w_in = p["w_in"]
    offs = [0]
    for n in (NSA_Q, KV_W, KV_W, KV_W, KV_W, KV_W, KV_W, 3 * NSA_HEADS, RET_QK, RET_QK, RET_V, RET_V, D_MODEL, D_MODEL):
        offs.append(offs[-1] + n)
    w_q = w_in[:, offs[0]:offs[1]].astype(BF16)
    w_kv = jnp.concatenate([w_in[:, offs[1]:offs[8]],
                            jnp.zeros((d, LANES - 3 * NSA_HEADS), F32)], axis=1).astype(BF16)
    w_rot = w_in[:, offs[8]:offs[10]].astype(BF16)
    w_act = w_in[:, offs[10]:offs[14]].astype(BF16)

    def rows(n):
        return pl.BlockSpec((ROW_TILE, n), lambda j, i: (i, 0))

    q = _proj_call(_proj_q_body, xn, w_q, [p["q_norm_g"].reshape(1, HEAD_DIM)], [_vec_spec(HEAD_DIM)],
                   jax.ShapeDtypeStruct((t_all, NSA_Q), BF16), rows(NSA_Q), NSA_Q, "proj_q")
    row2 = jax.ShapeDtypeStruct((NSA_GROUPS * t_all, HEAD_DIM), F32)
    row2_spec = pl.BlockSpec((NSA_GROUPS * ROW_TILE, HEAD_DIM), lambda j, i: (i, 0))
    *kv_rows, kvb, gate = _proj_call(
        _proj_kv_body, xn, w_kv,
        [p["k_sel_norm_g"].reshape(1, HEAD_DIM), p["k_win_norm_g"].reshape(1, HEAD_DIM)],
        [_vec_spec(HEAD_DIM), _vec_spec(HEAD_DIM)],
        [row2] * 6 + [jax.ShapeDtypeStruct((t_all, 4 * KV_W), BF16), jax.ShapeDtypeStruct((t_all, LANES), F32)],
        [row2_spec] * 6 + [rows(4 * KV_W), rows(LANES)], 6 * KV_W + LANES, "proj_kv")
    inv = 1.0 / (ROPE_BASE ** jnp.linspace(0.0, 1.0, RET_DK // 2, dtype=F32))
    rot = _proj_call(
        functools.partial(_proj_rot_body, n_prompt=n_prompt, seq=seq, past=past, dec_seq=t_new),
        xn, w_rot, [jnp.repeat(inv, 2).reshape(1, RET_DK)], [_vec_spec(RET_DK)],
        jax.ShapeDtypeStruct((t_all, 2 * RET_QK), BF16),
        pl.BlockSpec((ROW_TILE, RET_QK), lambda j, i: (i, j)), RET_QK, "proj_rot")
    act_tn = 1024
    act = _proj_call(
        functools.partial(_proj_act_body, tiles_plain=RET_V // act_tn, tiles_silu=RET_V // act_tn),
        xn, w_act, [], [], jax.ShapeDtypeStruct((t_all, 2 * RET_V + 2 * D_MODEL), BF16),
        pl.BlockSpec((ROW_TILE, act_tn), lambda j, i: (i, j)), act_tn, "proj_act")

    cmp_w = (_compress_weights(p["cmp_k_w1"], p["cmp_k_b1"], p["cmp_k_w2"])
             + _compress_weights(p["cmp_v_w1"], p["cmp_v_b1"], p["cmp_v_w2"])
             + (p["k_cmp_norm_g"].reshape(1, HEAD_DIM),))
    gn = p["ret_norm_g"].reshape(1, RET_V)

    kc_p, vc_p = _compress_prompt(kv_rows[0], kv_rows[1], cmp_w, batch, seq)
    o_nsa_p = _nsa_prompt(q, gate, kc_p, vc_p, kvb, batch, seq)
    o_ret_p, ret_p = _ret_prompt(rot, act, gn, batch, seq)

    pool_kc, pool_vc, pool_ks, pool_vs = [
        a.reshape(a.shape[0], NSA_GROUPS * PAGE_SIZE, HEAD_DIM) for a in pools]
    pt_flat = page_table.reshape(-1)
    kc_s, vc_s = _compress_sample(pool_kc, pool_vc, pt_flat, cmp_w, dec_batch, n_pages)
    wb = kwbuf.shape[1]
    o_nsa_s, kw_s, vw_s = _nsa_sample(
        q[n_prompt:].astype(F32), gate[n_prompt:], kv_rows[2:], n_prompt // t_new, kc_s, vc_s, pool_ks, pool_vs,
        kwbuf.reshape(dec_batch, wb * NSA_GROUPS, HEAD_DIM), vwbuf.reshape(dec_batch, wb * NSA_GROUPS, HEAD_DIM),
        pt_flat, dec_batch, t_new, n_pages)
    rot_s = rot[n_prompt:].astype(F32)
    act_s = act[n_prompt:, :2 * RET_V].astype(F32)
    o_ret_s, ret_s = _ret_sample(rot_s[:, :RET_QK], rot_s[:, RET_QK:], act_s[:, :RET_V], act_s[:, RET_V:],
                                 gn, s_ret, dec_batch, t_new)

    o_nsa = jnp.concatenate([o_nsa_p, o_nsa_s.astype(BF16)], axis=0)
    o_ret = jnp.concatenate([o_ret_p, o_ret_s.astype(BF16)], axis=0)
    mixed = _merge(o_nsa, o_ret, p["w_nsa_br"].astype(BF16), p["w_ret_br"].astype(BF16), act)
    rw = jnp.concatenate([p["router_w"], jnp.zeros((d, LANES - N_EXPERTS), F32)], axis=1)
    rwh, rwl = _split_bf16(rw)
    rb = jnp.concatenate([p["router_b"], jnp.zeros((LANES - N_EXPERTS,), F32)]).reshape(1, LANES)
    x1, h, route, counts = _out_router(xp2, xs2, mixed, p["w_out"].astype(BF16), p["ln2_g"].reshape(1, d),
                                       rwh, rwl, rb)
    y_p, y_s = _moe(x1, h, route, counts, p["w_gate_up"], p["b_gate_up"], p["w_down"], p["b_down"], n_prompt)

    kv5 = (NSA_GROUPS, HEAD_DIM)
    wbp = min(WINDOW, seq)
    rows_p = [a[:NSA_GROUPS * n_prompt].reshape(batch, seq, *kv5) for a in kv_rows]
    rows_s = [a[NSA_GROUPS * n_prompt:].reshape(dec_batch, t_new, *kv5) for a in kv_rows[:4]]
    states_p = (*rows_p[:4], rows_p[4][:, seq - wbp:], rows_p[5][:, seq - wbp:], ret_p)
    states_s = (*rows_s, kw_s.reshape(dec_batch, wb, *kv5), vw_s.reshape(dec_batch, wb, *kv5), ret_s)
    return y_p.reshape(batch, seq, d), y_s.reshape(dec_batch, t_new, d), states_p, states_s


def kernel(x_prompt, x_sample, cache_k_cmp, cache_v_cmp, cache_k_sel, cache_v_sel, state_k_win, state_v_win,
           state_ret, page_table, ln1_g, w_in, cmp_k_w1, cmp_k_b1, cmp_k_w2, cmp_v_w1, cmp_v_b1, cmp_v_w2,
           q_norm_g, k_cmp_norm_g, k_sel_norm_g, k_win_norm_g, ret_norm_g, w_nsa_br, w_ret_br, w_out, ln2_g,
           router_w, router_b, w_gate_up, b_gate_up, w_down, b_down):
    depth = w_in.shape[0]
    y_p, y_s = x_prompt, x_sample
    new_p, new_s = [], []
    for l in range(depth):
        p = {
            "ln1_g": ln1_g[l], "w_in": w_in[l],
            "cmp_k_w1": cmp_k_w1[l], "cmp_k_b1": cmp_k_b1[l], "cmp_k_w2": cmp_k_w2[l],
            "cmp_v_w1": cmp_v_w1[l], "cmp_v_b1": cmp_v_b1[l], "cmp_v_w2": cmp_v_w2[l],
            "q_norm_g": q_norm_g[l], "k_cmp_norm_g": k_cmp_norm_g[l], "k_sel_norm_g": k_sel_norm_g[l],
            "k_win_norm_g": k_win_norm_g[l], "ret_norm_g": ret_norm_g[l],
            "w_nsa_br": w_nsa_br[l], "w_ret_br": w_ret_br[l], "w_out": w_out[l], "ln2_g": ln2_g[l],
            "router_w": router_w[l], "router_b": router_b[l], "w_gate_up": w_gate_up[l],
            "b_gate_up": b_gate_up[l], "w_down": w_down[l], "b_down": b_down[l],
        }
        pools = (cache_k_cmp[l], cache_v_cmp[l], cache_k_sel[l], cache_v_sel[l])
        y_p, y_s, sp, ss = _layer(y_p, y_s, pools, state_k_win[l], state_v_win[l], state_ret[l], page_table, p)
        new_p.append(sp)
        new_s.append(ss)
    outs_p = [jnp.stack(a) for a in zip(*new_p)]
    outs_s = [jnp.stack(a) for a in zip(*new_s)]
    return (y_p, y_s, *outs_p, *outs_s)
```

```python
import functools
import math

import jax
import jax.numpy as jnp
from jax import lax
from jax.experimental import pallas as pl
from jax.experimental.pallas import tpu as pltpu

F32 = jnp.float32
BF16 = jnp.bfloat16

D_MODEL = 2048
PAGE_SIZE = 128
NSA_HEADS = 8
NSA_GROUPS = 2
HPG = NSA_HEADS // NSA_GROUPS
HEAD_DIM = 128
CMP_LEN = 32
CMP_STRIDE = 16
CMP_HIDDEN = 2 * HEAD_DIM
SEL_BLK = 64
SEL_SHIFT = 6
SEL_TOPK = 16
WINDOW = 512
Q_BLK = 128
RET_HEADS = 8
RET_DK = 128
RET_DV = 256
RET_CHUNK = 128
ROPE_BASE = 10000.0
N_EXPERTS = 32
TOP_K = 4
D_FF = D_MODEL
SWIGLU_LIMIT = 7.0
SWIGLU_ALPHA = 1.702
EPS = 1e-6
NEG = -1e30
BIG = 1e30

NSA_Q = NSA_HEADS * HEAD_DIM
KV_W = NSA_GROUPS * HEAD_DIM
RET_QK = RET_HEADS * RET_DK
RET_V = RET_HEADS * RET_DV

LANES = 128
ROW_TILE = 512
SEL_KV_TILE = 512
MOE_TILE = 512
STAGE_PITCH = 24
MOE_SUB = 256
MOE_FF_TILE = 512
VMEM_LIMIT = 56 * 1024 * 1024


def _cparams(*sem):
    return pltpu.CompilerParams(dimension_semantics=sem, vmem_limit_bytes=VMEM_LIMIT)


def _dot(a, b):
    return jnp.dot(a, b, preferred_element_type=F32)


def _dot_nt(a, b):
    return lax.dot_general(a, b, (((1,), (1,)), ((), ())), preferred_element_type=F32)


def _dot_tn(a, b):
    return lax.dot_general(a, b, (((0,), (0,)), ((), ())), preferred_element_type=F32)


def _sigmoid(x):
    return 1.0 / (1.0 + jnp.exp(-x))


def _unit_rms(x):
    return x * lax.rsqrt(jnp.mean(x * x, axis=-1, keepdims=True) + EPS)


def _iota(shape, dim):
    return lax.broadcasted_iota(jnp.int32, shape, dim)


def _split_bf16(x):
    hi = x.astype(BF16)
    lo = (x - hi.astype(F32)).astype(BF16)
    return hi, lo


def _two_group_specs(tm, d, tiles_p):
    return [pl.BlockSpec((tm, d), lambda i: (jnp.minimum(i, tiles_p - 1), 0)),
            pl.BlockSpec((tm, d), lambda i: (jnp.maximum(i - tiles_p, 0), 0))]


def _ln1_body(xp_ref, xs_ref, g_ref, o_ref, *, tiles_p):
    def norm(x_ref):
        o_ref[...] = (_unit_rms(x_ref[...]) * g_ref[...]).astype(o_ref.dtype)

    pl.when(pl.program_id(0) < tiles_p)(lambda: norm(xp_ref))
    pl.when(pl.program_id(0) >= tiles_p)(lambda: norm(xs_ref))


def _ln1(xp, xs, g):
    d = xp.shape[1]
    tiles_p = xp.shape[0] // ROW_TILE
    t = xp.shape[0] + xs.shape[0]
    return pl.pallas_call(
        functools.partial(_ln1_body, tiles_p=tiles_p),
        grid=(t // ROW_TILE,),
        in_specs=_two_group_specs(ROW_TILE, d, tiles_p) + [pl.BlockSpec((1, d), lambda i: (0, 0))],
        out_specs=pl.BlockSpec((ROW_TILE, d), lambda i: (i, 0)),
        out_shape=jax.ShapeDtypeStruct((t, d), BF16),
        compiler_params=_cparams("arbitrary"),
        name="ln1",
    )(xp, xs, g.reshape(1, d))


def _proj_q_body(x_ref, w_ref, g_ref, o_ref):
    y = _dot(x_ref[...], w_ref[...])
    g = g_ref[...] * (HEAD_DIM ** -0.5)
    for c in range(NSA_HEADS):
        sl = slice(c * LANES, (c + 1) * LANES)
        o_ref[:, sl] = (_unit_rms(y[:, sl]) * g).astype(o_ref.dtype)


def _proj_kv_body(x_ref, w_ref, gs_ref, gw_ref, kc_ref, vc_ref, ks_ref, vs_ref, kw_ref, vw_ref, kvb_ref, gate_ref):
    y = _dot(x_ref[...], w_ref[...])
    tm = y.shape[0]
    outs = (kc_ref, vc_ref, ks_ref, vs_ref, kw_ref, vw_ref)
    for c in range(12):
        sl = slice(c * LANES, (c + 1) * LANES)
        yc = y[:, sl]
        if c in (4, 5):
            yc = _unit_rms(yc) * gs_ref[...]
        elif c in (8, 9):
            yc = _unit_rms(yc) * gw_ref[...]
        outs[c // NSA_GROUPS][pl.ds(c % NSA_GROUPS, tm, stride=NSA_GROUPS), :] = yc
        if c >= 4:
            kvb_ref[:, (c - 4) * LANES:(c - 3) * LANES] = yc.astype(BF16)
    gate_ref[...] = _sigmoid(y[:, 12 * LANES:13 * LANES])


def _proj_rot_body(x_ref, w_ref, inv_ref, o_ref, *, n_prompt, seq, past, dec_seq):
    j = pl.program_id(0)
    i = pl.program_id(1)
    y = _dot(x_ref[...], w_ref[...])
    tm = y.shape[0]
    row = i * tm + _iota((tm, LANES), 0)
    pos = jnp.where(row < n_prompt, row & (seq - 1), past + ((row - n_prompt) & (dec_seq - 1)))
    ang = pos.astype(F32) * inv_ref[...]
    cos = jnp.cos(ang)
    sin = jnp.sin(ang)
    even = (_iota((tm, LANES), 1) & 1) == 0
    sin = jnp.where(even, -sin, sin)
    scale = jnp.where(j == 0, 1.0, RET_DK ** -0.5).astype(F32)
    for c in range(RET_HEADS):
        sl = slice(c * LANES, (c + 1) * LANES)
        yc = y[:, sl]
        partner = jnp.where(even, pltpu.roll(yc, LANES - 1, 1), pltpu.roll(yc, 1, 1))
        o_ref[:, sl] = ((yc * cos + partner * sin) * scale).astype(o_ref.dtype)


def _proj_act_body(x_ref, w_ref, o_ref, *, tiles_plain, tiles_silu):
    j = pl.program_id(0)
    y = _dot(x_ref[...], w_ref[...])

    @pl.when(j < tiles_plain)
    def _():
        o_ref[...] = y.astype(o_ref.dtype)

    @pl.when((j >= tiles_plain) & (j < tiles_plain + tiles_silu))
    def _():
        o_ref[...] = (y * _sigmoid(y)).astype(o_ref.dtype)

    @pl.when(j >= tiles_plain + tiles_silu)
    def _():
        o_ref[...] = _sigmoid(y).astype(o_ref.dtype)


def _proj_call(body, xn, w, extra, extra_specs, out_shape, out_specs, tn, name):
    t, k = xn.shape
    n = w.shape[1]
    in_specs = [pl.BlockSpec((ROW_TILE, k), lambda j, i: (i, 0)),
                pl.BlockSpec((k, tn), lambda j, i: (0, j))] + extra_specs
    return pl.pallas_call(
        body,
        grid=(n // tn, t // ROW_TILE),
        in_specs=in_specs,
        out_specs=out_specs,
        out_shape=out_shape,
        compiler_params=_cparams("parallel", "parallel"),
        name=name,
    )(xn, w, *extra)


def _vec_spec(n):
    return pl.BlockSpec((1, n), lambda j, i: (0, 0))


def _compress_mlp(x_cat, w1_ref, b1_ref, w2_ref):
    nh = x_cat.shape[0]
    a = _dot(x_cat, w1_ref[...])
    pre = a[:, :CMP_HIDDEN] + pltpu.roll(a[:, CMP_HIDDEN:], nh - 1, 0) + b1_ref[...]
    hid = pre * (0.5 * (1.0 + jnp.tanh(math.sqrt(2.0 / math.pi) * (pre + 0.044715 * (pre * pre * pre)))))
    out = _dot(hid.astype(BF16), w2_ref[...])
    return jnp.where(_iota(out.shape, 0) < nh - 1, out, 0.0)


def _compress_finish(xk, xv, wrefs, kc_ref, vc_ref):
    w1k, b1k, w2k, w1v, b1v, w2v, gk = wrefs
    for g in range(NSA_GROUPS):
        sl = slice(g * LANES, (g + 1) * LANES)
        kc = _compress_mlp(xk[g], w1k, b1k, w2k)
        kc_ref[:, sl] = (_unit_rms(kc) * gk[...]).astype(kc_ref.dtype)
        vc_ref[:, sl] = _compress_mlp(xv[g], w1v, b1v, w2v).astype(vc_ref.dtype)


def _compress_prompt_body(krows_ref, vrows_ref, *refs, nh):
    wrefs, (kc_ref, vc_ref) = refs[:7], refs[7:]

    def gather(rows_ref, g):
        return jnp.concatenate(
            [rows_ref[pl.ds(NSA_GROUPS * s + g, nh, stride=NSA_GROUPS * CMP_STRIDE), :].astype(BF16)
             for s in range(CMP_STRIDE)], axis=1)

    xk = [gather(krows_ref, g) for g in range(NSA_GROUPS)]
    xv = [gather(vrows_ref, g) for g in range(NSA_GROUPS)]
    _compress_finish(xk, xv, wrefs, kc_ref, vc_ref)


def _compress_sample_body(pt_ref, *refs, n_pages):
    del pt_ref
    kpages, vpages = refs[:n_pages], refs[n_pages:2 * n_pages]
    wrefs = refs[2 * n_pages:2 * n_pages + 7]
    kc_ref, vc_ref, stage_ref = refs[2 * n_pages + 7:]
    per_page = PAGE_SIZE // CMP_STRIDE
    nh = n_pages * per_page

    def gather(pages, g, slab):
        stage = stage_ref.at[slab]
        for p, page in enumerate(pages):
            for m in range(per_page):
                half = page[pl.ds(NSA_GROUPS * CMP_STRIDE * m + g, CMP_STRIDE, stride=NSA_GROUPS), :]
                stage[pl.ds((p * per_page + m) * STAGE_PITCH, CMP_STRIDE), :] = half
        return jnp.concatenate(
            [stage[pl.ds(s, nh, stride=STAGE_PITCH), :].astype(BF16) for s in range(CMP_STRIDE)], axis=1)

    xk = [gather(kpages, g, g) for g in range(NSA_GROUPS)]
    xv = [gather(vpages, g, NSA_GROUPS + g) for g in range(NSA_GROUPS)]
    _compress_finish(xk, xv, wrefs, kc_ref, vc_ref)


def _const_spec(shape, nargs):
    zeros = (0,) * len(shape)
    return pl.BlockSpec(shape, lambda *a: zeros)


def _compress_weights(w1, b1, w2):
    r_n = CMP_LEN // CMP_STRIDE
    w1r = w1.reshape(r_n, CMP_STRIDE * HEAD_DIM, CMP_HIDDEN)
    w1cat = jnp.concatenate([w1r[r] for r in range(r_n)], axis=1).astype(BF16)
    return w1cat, b1.reshape(1, CMP_HIDDEN), w2.astype(BF16)


def _compress_weight_specs():
    k16 = CMP_STRIDE * HEAD_DIM
    one = [_const_spec((k16, 2 * CMP_HIDDEN), 0), _const_spec((1, CMP_HIDDEN), 0),
           _const_spec((CMP_HIDDEN, HEAD_DIM), 0)]
    return one + one + [_const_spec((1, HEAD_DIM), 0)]


def _compress_prompt(krows, vrows, weights, batch, seq):
    nh = seq // CMP_STRIDE
    out = jax.ShapeDtypeStruct((batch, nh, KV_W), BF16)
    ospec = pl.BlockSpec((None, nh, KV_W), lambda b: (b, 0, 0))
    return pl.pallas_call(
        functools.partial(_compress_prompt_body, nh=nh),
        grid=(batch,),
        in_specs=[pl.BlockSpec((NSA_GROUPS * seq, LANES), lambda b: (b, 0))] * 2 + _compress_weight_specs(),
        out_specs=[ospec, ospec],
        out_shape=[out, out],
        compiler_params=_cparams("parallel"),
        name="compress_prompt",
    )(krows, vrows, *weights)


def _page_specs(n_pages, nb=1, bb=0):
    def spec(p):
        return pl.BlockSpec((None, NSA_GROUPS * PAGE_SIZE, HEAD_DIM),
                            lambda b, pt: (pt[(b * nb + bb) * n_pages + p], 0, 0))
    return [spec(p) for p in range(n_pages)]


def _compress_sample(pool_k, pool_v, pt_flat, weights, dec_batch, n_pages):
    nh = n_pages * PAGE_SIZE // CMP_STRIDE
    out = jax.ShapeDtypeStruct((dec_batch, nh, KV_W), BF16)
    ospec = pl.BlockSpec((None, nh, KV_W), lambda b, pt: (b, 0, 0))
    grid_spec = pltpu.PrefetchScalarGridSpec(
        num_scalar_prefetch=1,
        grid=(dec_batch,),
        in_specs=_page_specs(n_pages) + _page_specs(n_pages) + _compress_weight_specs(),
        out_specs=[ospec, ospec],
        scratch_shapes=[pltpu.VMEM((2 * NSA_GROUPS, nh * STAGE_PITCH, HEAD_DIM), F32)],
    )
    return pl.pallas_call(
        functools.partial(_compress_sample_body, n_pages=n_pages),
        grid_spec=grid_spec,
        out_shape=[out, out],
        compiler_params=_cparams("parallel"),
        name="compress_sample",
    )(pt_flat, *([pool_k] * n_pages), *([pool_v] * n_pages), *weights)


def _masked_softmax(s3, ok):
    s3 = jnp.where(ok[None], s3, NEG)
    m = jnp.max(s3, axis=-1, keepdims=True)
    e = jnp.where(ok[None], jnp.exp(s3 - m), 0.0)
    den = jnp.sum(e, axis=-1, keepdims=True)
    return e / jnp.where(den > 0.0, den, 1.0)


def _topk_lanes(score, n_sel, k):
    lane = _iota(score.shape, 1)
    sc = jnp.where(lane < n_sel, score, -jnp.inf)
    rank = jnp.zeros(score.shape, F32)
    for i in range(n_sel):
        ci = sc[:, i:i + 1]
        later = jnp.where(lane > i, 1.0, 0.0)
        rank = rank + jnp.where(ci > sc, 1.0, jnp.where(ci == sc, later, 0.0))
    return jnp.where((rank < k) & (lane < n_sel), 1.0, 0.0)


def _bias_softmax_pv(s3, ok, v):
    h, t, n = s3.shape
    s3 = s3 + jnp.where(ok, 0.0, NEG)[None]
    e = jnp.exp(s3 - jnp.max(s3, axis=-1, keepdims=True))
    inv = 1.0 / jnp.sum(e, axis=-1, keepdims=True)
    return _dot(e.reshape(h * t, n).astype(BF16), v) * inv.reshape(h * t, 1)


def _compressed_branch(qg, kcg, vcg, qpos, n_cmp):
    t = qpos.shape[0]
    nc_pad = kcg.shape[0]
    c_idx = _iota((t, nc_pad), 1)
    vis = (c_idx * CMP_STRIDE + (CMP_LEN - 1) <= qpos) & (c_idx < n_cmp)
    p3 = _masked_softmax(_dot_nt(qg, kcg).reshape(HPG, t, nc_pad), vis)
    o_c = _dot(p3.reshape(HPG * t, nc_pad).astype(BF16), vcg)
    return o_c, jnp.sum(p3, axis=0)


def _overlap(nc_pad, blocks_first):
    shape = (LANES, nc_pad) if blocks_first else (nc_pad, LANES)
    ci = _iota(shape, 1 if blocks_first else 0) * CMP_STRIDE
    sj = _iota(shape, 0 if blocks_first else 1) * SEL_BLK
    return jnp.where((ci < sj + SEL_BLK) & (ci + CMP_LEN > sj), 1.0, 0.0).astype(BF16)


def _select_lanes(psum, qpos, n_sel):
    t, nc_pad = psum.shape
    hi, lo = _split_bf16(psum)
    ov = _overlap(nc_pad, False)
    score = _dot(hi, ov) + _dot(lo, ov)
    j = _iota((t, LANES), 1)
    cur = qpos >> SEL_SHIFT
    forced = (j == 0) | (j == cur) | (j == cur - 1)
    score = jnp.where(forced, BIG, score)
    score = jnp.where(j * SEL_BLK <= qpos, score, NEG)
    return _topk_lanes(score, n_sel, min(SEL_TOPK, n_sel))


def _select_sublanes(psum, start, n_sel):
    t, nc_pad = psum.shape
    hi, lo = _split_bf16(psum)
    ov = _overlap(nc_pad, True)
    rows = -(-n_sel // 8) * 8
    sc = (_dot_nt(ov, hi) + _dot_nt(ov, lo))[:rows]
    j = _iota((rows, t), 0)
    qp = start + _iota((rows, t), 1)
    cur = qp >> SEL_SHIFT
    forced = (j == 0) | (j == cur) | (j == cur - 1)
    sc = jnp.where(forced, BIG, sc)
    sc = jnp.where(j * SEL_BLK <= qp, sc, NEG)
    sc = jnp.where(j < n_sel, sc, -jnp.inf)
    rank = jnp.zeros((rows, t), F32)
    for i in range(n_sel):
        ri = sc[i:i + 1, :]
        later = jnp.where(j > i, 1.0, 0.0)
        rank = rank + jnp.where(ri > sc, 1.0, jnp.where(ri == sc, later, 0.0))
    sel = jnp.where((rank < min(SEL_TOPK, n_sel)) & (j < n_sel), 1.0, 0.0).astype(BF16)
    if rows < LANES:
        sel = jnp.concatenate([sel, jnp.zeros((LANES - rows, t), BF16)], axis=0)
    return sel


def _block_to_key(blk0, n):
    blk = blk0 + (_iota((LANES, n), 1) >> SEL_SHIFT)
    return jnp.where(_iota((LANES, n), 0) == blk, 1.0, 0.0).astype(BF16)


def _nsa_prompt_body(q_ref, gate_ref, kc_ref, vc_ref, kv_ref, o_ref, *, seq):
    tq = Q_BLK
    start = pl.program_id(1) * tq
    n_sel = seq // SEL_BLK
    n_cmp = seq // CMP_STRIDE - CMP_LEN // CMP_STRIDE + 1
    qpos = start + _iota((tq, 1), 0)
    gates = gate_ref[...]
    wk = WINDOW + tq
    for g in range(NSA_GROUPS):
        gsl = slice(g * LANES, (g + 1) * LANES)
        qg = jnp.concatenate(
            [q_ref[:, (g * HPG + j) * LANES:(g * HPG + j + 1) * LANES] for j in range(HPG)], axis=0)
        o_c, psum = _compressed_branch(qg, kc_ref[:, gsl], vc_ref[:, gsl], qpos, n_cmp)
        sel_t = _select_sublanes(psum, start, n_sel)

        def sel_step(k, carry):
            m, l, acc = carry
            k0 = pl.multiple_of(k * SEL_KV_TILE, SEL_KV_TILE)
            kt = kv_ref[pl.ds(k0, SEL_KV_TILE), pl.ds(g * LANES, LANES)]
            vt = kv_ref[pl.ds(k0, SEL_KV_TILE), pl.ds(KV_W + g * LANES, LANES)]
            chosen = _dot_tn(sel_t, _block_to_key(k * (SEL_KV_TILE // SEL_BLK), SEL_KV_TILE))
            ok = (chosen > 0.5) & (k0 + _iota((tq, SEL_KV_TILE), 1) <= qpos)
            s3 = _dot_nt(qg, kt).reshape(HPG, tq, SEL_KV_TILE) + jnp.where(ok, 0.0, NEG)[None]
            m_new = jnp.maximum(m, jnp.max(s3, axis=-1, keepdims=True))
            alpha = jnp.exp(m - m_new)
            p = jnp.exp(s3 - m_new)
            l = alpha * l + jnp.sum(p, axis=-1, keepdims=True)
            pv = _dot(p.reshape(HPG * tq, SEL_KV_TILE).astype(BF16), vt)
            return m_new, l, alpha * acc + pv.reshape(HPG, tq, LANES)

        n_kv = (start + tq + SEL_KV_TILE - 1) // SEL_KV_TILE
        init = (jnp.full((HPG, tq, 1), NEG, F32), jnp.zeros((HPG, tq, 1), F32),
                jnp.zeros((HPG, tq, LANES), F32))
        _, l_s, acc_s = lax.fori_loop(0, n_kv, sel_step, init)
        o_s = acc_s * (1.0 / l_s)

        w0 = pl.multiple_of(jnp.maximum(start - WINDOW, 0), Q_BLK)
        kw = kv_ref[pl.ds(w0, wk), pl.ds(2 * KV_W + g * LANES, LANES)]
        vw = kv_ref[pl.ds(w0, wk), pl.ds(3 * KV_W + g * LANES, LANES)]
        dpos = qpos - (w0 + _iota((tq, wk), 1))
        o_w = _bias_softmax_pv(_dot_nt(qg, kw).reshape(HPG, tq, wk), (dpos >= 0) & (dpos < WINDOW), vw)

        for j in range(HPG):
            h = g * HPG + j
            rows = slice(j * tq, (j + 1) * tq)
            o = (o_c[rows] * gates[:, 3 * h:3 * h + 1] + o_s[j] * gates[:, 3 * h + 1:3 * h + 2]
                 + o_w[rows] * gates[:, 3 * h + 2:3 * h + 3])
            o_ref[:, h * LANES:(h + 1) * LANES] = o.astype(o_ref.dtype)


def _nsa_prompt(q, gate, kc, vc, kvb, batch, seq):
    nq = seq // Q_BLK
    nh = seq // CMP_STRIDE
    return pl.pallas_call(
        functools.partial(_nsa_prompt_body, seq=seq),
        grid=(batch, nq),
        in_specs=[pl.BlockSpec((Q_BLK, NSA_Q), lambda b, i: (b * nq + i, 0)),
                  pl.BlockSpec((Q_BLK, LANES), lambda b, i: (b * nq + i, 0)),
                  pl.BlockSpec((None, nh, KV_W), lambda b, i: (b, 0, 0)),
                  pl.BlockSpec((None, nh, KV_W), lambda b, i: (b, 0, 0)),
                  pl.BlockSpec((seq, 4 * KV_W), lambda b, i: (b, 0))],
        out_specs=pl.BlockSpec((Q_BLK, NSA_Q), lambda b, i: (b * nq + i, 0)),
        out_shape=jax.ShapeDtypeStruct((batch * seq, NSA_Q), BF16),
        compiler_params=_cparams("parallel", "parallel"),
        name="nsa_prompt",
    )(q, gate, kc, vc, kvb)


def _nsa_sample_one(q, gates, new_refs, kc_ref, vc_ref, kpages, vpages, kwbuf_ref, vwbuf_ref,
                    kwout_ref, vwout_ref, past):
    t = q.shape[0]
    wb = kwbuf_ref.shape[0] // NSA_GROUPS
    n_past = past // SEL_BLK
    n_sel = n_past + -(-t // SEL_BLK)
    n_cmp = (past + t) // CMP_STRIDE - CMP_LEN // CMP_STRIDE + 1
    qpos = past + _iota((t, 1), 0)
    pad = jnp.zeros((LANES - t, LANES), BF16)
    n_keys = past + LANES
    wk = wb + LANES
    heads = []
    for g in range(NSA_GROUPS):
        gsl = slice(g * LANES, (g + 1) * LANES)

        def new_rows(which):
            return new_refs[which][pl.ds(g, t, stride=NSA_GROUPS), :]

        qg = jnp.concatenate(
            [q[:, (g * HPG + j) * LANES:(g * HPG + j + 1) * LANES] for j in range(HPG)], axis=0).astype(BF16)
        o_c, psum = _compressed_branch(qg, kc_ref[:, gsl], vc_ref[:, gsl], qpos, n_cmp)
        sel = _select_lanes(psum, qpos, n_sel)

        def keys(pages, which):
            past_rows = [p[pl.ds(g, PAGE_SIZE, stride=NSA_GROUPS), :].astype(BF16) for p in pages]
            return jnp.concatenate(past_rows + [new_rows(which).astype(BF16), pad], axis=0)

        ks, vs = keys(kpages, 2), keys(vpages, 3)
        chosen = _dot(sel.astype(BF16), _block_to_key(0, n_keys))
        ok = (chosen > 0.5) & (_iota((t, n_keys), 1) <= qpos)
        o_s = _bias_softmax_pv(_dot_nt(qg, ks).reshape(HPG, t, n_keys), ok, vs)

        def window(buf_ref, which):
            return jnp.concatenate([buf_ref[pl.ds(g, wb, stride=NSA_GROUPS), :].astype(BF16),
                                    new_rows(which).astype(BF16), pad], axis=0)

        kw, vw = window(kwbuf_ref, 4), window(vwbuf_ref, 5)
        dpos = qpos - (past - wb + _iota((t, wk), 1))
        o_w = _bias_softmax_pv(_dot_nt(qg, kw).reshape(HPG, t, wk), (dpos >= 0) & (dpos < WINDOW), vw)

        for j in range(HPG):
            h = g * HPG + j
            rows = slice(j * t, (j + 1) * t)
            heads.append(o_c[rows] * gates[:, 3 * h:3 * h + 1] + o_s[rows] * gates[:, 3 * h + 1:3 * h + 2]
                         + o_w[rows] * gates[:, 3 * h + 2:3 * h + 3])

        kwout_ref[pl.ds((wb - t) * NSA_GROUPS + g, t, stride=NSA_GROUPS), :] = new_rows(4)
        vwout_ref[pl.ds((wb - t) * NSA_GROUPS + g, t, stride=NSA_GROUPS), :] = new_rows(5)
    keep = (wb - t) * NSA_GROUPS
    kwout_ref[pl.ds(0, keep), :] = kwbuf_ref[pl.ds(t * NSA_GROUPS, keep), :]
    vwout_ref[pl.ds(0, keep), :] = vwbuf_ref[pl.ds(t * NSA_GROUPS, keep), :]
    return heads


def _nsa_sample_body(pt_ref, q_ref, gate_ref, ksn_ref, vsn_ref, kwn_ref, vwn_ref, kc_ref, vc_ref, *refs,
                     n_pages, past, nb):
    del pt_ref
    kpages, vpages = refs[:nb * n_pages], refs[nb * n_pages:2 * nb * n_pages]
    kwbuf_ref, vwbuf_ref, o_ref, kwout_ref, vwout_ref = refs[2 * nb * n_pages:]
    t = q_ref.shape[0] // nb
    q = q_ref[...].astype(F32)
    gates = gate_ref[...]
    per_item = []
    for bb in range(nb):
        tok = slice(bb * t, (bb + 1) * t)
        tok2 = pl.ds(bb * NSA_GROUPS * t, NSA_GROUPS * t)
        new_refs = {2: ksn_ref.at[tok2], 3: vsn_ref.at[tok2], 4: kwn_ref.at[tok2], 5: vwn_ref.at[tok2]}
        pages = slice(bb * n_pages, (bb + 1) * n_pages)
        per_item.append(_nsa_sample_one(
            q[tok], gates[tok], new_refs, kc_ref.at[bb], vc_ref.at[bb], kpages[pages], vpages[pages],
            kwbuf_ref.at[bb], vwbuf_ref.at[bb], kwout_ref.at[bb], vwout_ref.at[bb], past))
    for h in range(NSA_HEADS):
        o_ref[:, h * LANES:(h + 1) * LANES] = jnp.concatenate(
            [heads[h] for heads in per_item], axis=0).astype(o_ref.dtype)


def _nsa_sample(q, gate, new, row_block0, kc, vc, pool_k, pool_v, kwbuf, vwbuf, pt_flat, dec_batch, t, n_pages,
                nb=2):
    past = n_pages * PAGE_SIZE
    nh = past // CMP_STRIDE
    wrows = kwbuf.shape[1]

    def rows(r, n):
        return pl.BlockSpec((nb * r, n), lambda b, pt: (row_block0 + b, 0))

    def per_b(r, c):
        return pl.BlockSpec((nb, r, c), lambda b, pt: (b, 0, 0))

    pages = [spec for bb in range(nb) for spec in _page_specs(n_pages, nb, bb)]
    grid_spec = pltpu.PrefetchScalarGridSpec(
        num_scalar_prefetch=1,
        grid=(dec_batch // nb,),
        in_specs=[rows(t, NSA_Q), rows(t, LANES)] + [rows(NSA_GROUPS * t, HEAD_DIM)] * 4
        + [per_b(nh, KV_W), per_b(nh, KV_W)] + pages + pages + [per_b(wrows, HEAD_DIM), per_b(wrows, HEAD_DIM)],
        out_specs=[pl.BlockSpec((nb * t, NSA_Q), lambda b, pt: (b, 0)), per_b(wrows, HEAD_DIM),
                   per_b(wrows, HEAD_DIM)],
    )
    wout = jax.ShapeDtypeStruct(kwbuf.shape, F32)
    pools = [pool_k] * (nb * n_pages) + [pool_v] * (nb * n_pages)
    return pl.pallas_call(
        functools.partial(_nsa_sample_body, n_pages=n_pages, past=past, nb=nb),
        grid_spec=grid_spec,
        out_shape=[jax.ShapeDtypeStruct((dec_batch * t, NSA_Q), BF16), wout, wout],
        compiler_params=_cparams("parallel"),
        name="nsa_sample",
    )(pt_flat, q, gate, *new, kc, vc, *pools, kwbuf, vwbuf)


def _log_decay(h):
    return math.log(1.0 - 2.0 ** (-5.0 - h))


def _ret_finish(o, h, gn_ref, rg):
    sl = slice(h * RET_DV, (h + 1) * RET_DV)
    return _unit_rms(o) * gn_ref[:, sl] * rg[:, sl]


def _ret_prompt_body(rq_ref, rk_ref, rv_ref, rg_ref, gn_ref, o_ref, s_ref):
    c = RET_CHUNK

    @pl.when(pl.program_id(1) == 0)
    def _():
        s_ref[...] = jnp.zeros(s_ref.shape, F32)

    n_col = _iota((c, 1), 0).astype(F32)
    diff = (_iota((c, c), 0) - _iota((c, c), 1)).astype(F32)
    rg = rg_ref[...].astype(F32)
    for h in range(RET_HEADS):
        lg = _log_decay(h)
        q = rq_ref[:, h * RET_DK:(h + 1) * RET_DK]
        k = rk_ref[:, h * RET_DK:(h + 1) * RET_DK]
        v = rv_ref[:, h * RET_DV:(h + 1) * RET_DV]
        s_prev = s_ref[h]
        dmask = jnp.where(diff >= 0.0, jnp.exp(jnp.maximum(diff, 0.0) * lg), 0.0)
        o = _dot((_dot_nt(q, k) * dmask).astype(BF16), v)
        q_dec = (q.astype(F32) * jnp.exp((n_col + 1.0) * lg)).astype(BF16)
        o = o + _dot(q_dec, s_prev.astype(BF16))
        k_dec = (k.astype(F32) * jnp.exp((c - 1.0 - n_col) * lg)).astype(BF16)
        s_ref[h] = math.exp(c * lg) * s_prev + _dot_tn(k_dec, v)
        o_ref[:, h * RET_DV:(h + 1) * RET_DV] = _ret_finish(o, h, gn_ref, rg).astype(o_ref.dtype)


def _ret_prompt(rot, act, gn, batch, seq):
    nch = seq // RET_CHUNK
    qk_w = RET_QK

    def rows(n, col):
        return pl.BlockSpec((RET_CHUNK, n), lambda b, i: (b * nch + i, col))

    return pl.pallas_call(
        _ret_prompt_body,
        grid=(batch, nch),
        in_specs=[rows(qk_w, 0), rows(qk_w, 1), rows(RET_V, 0), rows(RET_V, 1),
                  pl.BlockSpec((1, RET_V), lambda b, i: (0, 0))],
        out_specs=[rows(RET_V, 0),
                   pl.BlockSpec((None, RET_HEADS, RET_DK, RET_DV), lambda b, i: (b, 0, 0, 0))],
        out_shape=[jax.ShapeDtypeStruct((batch * seq, RET_V), BF16),
                   jax.ShapeDtypeStruct((batch, RET_HEADS, RET_DK, RET_DV), F32)],
        compiler_params=_cparams("parallel", "arbitrary"),
        name="retention_prompt",
    )(rot, rot, act, act, gn)


def _ret_sample_body(rq_ref, rk_ref, rv_ref, rg_ref, gn_ref, s_ref, o_ref, so_ref):
    c = rq_ref.shape[0]
    n_col = _iota((c, 1), 0).astype(F32)
    rg = rg_ref[...]
    zk = jnp.zeros((LANES - c, RET_DK), BF16)
    zv = jnp.zeros((LANES - c, RET_DV), BF16)
    for h in range(RET_HEADS):
        lg = _log_decay(h)
        q = rq_ref[:, h * RET_DK:(h + 1) * RET_DK]
        k = rk_ref[:, h * RET_DK:(h + 1) * RET_DK]
        v = rv_ref[:, h * RET_DV:(h + 1) * RET_DV]
        s_prev = s_ref[h]
        o = _dot((q * jnp.exp((n_col + 1.0) * lg)).astype(BF16), s_prev.astype(BF16))
        for j in range(c):
            qk = jnp.sum(q * k[j:j + 1, :], axis=-1, keepdims=True)
            dj = jnp.where(n_col >= j, jnp.exp(jnp.maximum(n_col - j, 0.0) * lg), 0.0)
            o = o + (qk * dj) * v[j:j + 1, :]
        k_dec = jnp.concatenate([(k * jnp.exp((c - 1.0 - n_col) * lg)).astype(BF16), zk], axis=0)
        v_pad = jnp.concatenate([v.astype(BF16), zv], axis=0)
        so_ref[h] = math.exp(c * lg) * s_prev + _dot_tn(k_dec, v_pad)
        o_ref[:, h * RET_DV:(h + 1) * RET_DV] = _ret_finish(o, h, gn_ref, rg)


def _ret_sample(rq, rk, rv, rg, gn, state, dec_batch, t):
    def rows(n):
        return pl.BlockSpec((t, n), lambda b: (b, 0))

    sspec = pl.BlockSpec((None, RET_HEADS, RET_DK, RET_DV), lambda b: (b, 0, 0, 0))
    return pl.pallas_call(
        _ret_sample_body,
        grid=(dec_batch,),
        in_specs=[rows(RET_QK), rows(RET_QK), rows(RET_V), rows(RET_V),
                  pl.BlockSpec((1, RET_V), lambda b: (0, 0)), sspec],
        out_specs=[rows(RET_V), sspec],
        out_shape=[jax.ShapeDtypeStruct((dec_batch * t, RET_V), F32),
                   jax.ShapeDtypeStruct(state.shape, F32)],
        compiler_params=_cparams("parallel"),
        name="retention_sample",
    )(rq, rk, rv, rg, gn, state)


def _merge_body(on_ref, or_ref, wn_ref, wr_ref, ga_ref, gr_ref, o_ref):
    a = _dot(on_ref[...], wn_ref[...])
    r = _dot(or_ref[...], wr_ref[...])
    o_ref[...] = (ga_ref[...].astype(F32) * a + gr_ref[...].astype(F32) * r).astype(o_ref.dtype)


def _merge(o_nsa, o_ret, wn, wr, act, tn=1024):
    t = o_nsa.shape[0]
    d = wn.shape[1]
    ga0 = (RET_V + RET_V) // tn
    gr0 = (RET_V + RET_V + D_MODEL) // tn
    return pl.pallas_call(
        _merge_body,
        grid=(d // tn, t // ROW_TILE),
        in_specs=[pl.BlockSpec((ROW_TILE, NSA_Q), lambda j, i: (i, 0)),
                  pl.BlockSpec((ROW_TILE, RET_V), lambda j, i: (i, 0)),
                  pl.BlockSpec((NSA_Q, tn), lambda j, i: (0, j)),
                  pl.BlockSpec((RET_V, tn), lambda j, i: (0, j)),
                  pl.BlockSpec((ROW_TILE, tn), lambda j, i: (i, ga0 + j)),
                  pl.BlockSpec((ROW_TILE, tn), lambda j, i: (i, gr0 + j))],
        out_specs=pl.BlockSpec((ROW_TILE, tn), lambda j, i: (i, j)),
        out_shape=jax.ShapeDtypeStruct((t, d), BF16),
        compiler_params=_cparams("parallel", "parallel"),
        name="merge",
    )(o_nsa, o_ret, wn, wr, act, act)


def _out_router_body(xp_ref, xs_ref, mix_ref, wo_ref, g_ref, rwh_ref, rwl_ref, rb_ref,
                     x1_ref, h_ref, route_ref, cnt_ref, *, tiles_p):
    @pl.when(pl.program_id(0) == 0)
    def _():
        cnt_ref[...] = jnp.zeros(cnt_ref.shape, F32)

    x = jnp.where(pl.program_id(0) < tiles_p, xp_ref[...], xs_ref[...])
    x1 = x + _dot(mix_ref[...], wo_ref[...])
    x1_ref[...] = x1
    h = _unit_rms(x1) * g_ref[...]
    h_ref[...] = h
    tm = h.shape[0]
    lane = _iota((tm, LANES), 1)
    hi, lo = _split_bf16(h)
    logits = _dot(hi, rwh_ref[...]) + _dot(lo, rwh_ref[...]) + _dot(hi, rwl_ref[...]) + rb_ref[...]
    work = jnp.where(lane < N_EXPERTS, logits, -jnp.inf)
    vals, idxs = [], []
    for _ in range(TOP_K):
        v = jnp.max(work, axis=-1, keepdims=True)
        ix = jnp.min(jnp.where(work == v, lane, LANES), axis=-1, keepdims=True)
        vals.append(v)
        idxs.append(ix)
        work = jnp.where(lane == ix, -jnp.inf, work)
    es = [jnp.exp(v - vals[0]) for v in vals]
    den = es[0] + es[1] + es[2] + es[3]
    hot = jnp.zeros((tm, LANES), F32)
    for ix in idxs:
        hot = hot + jnp.where(lane == ix, 1.0, 0.0)
    before = jnp.where(_iota((tm, tm), 1) < _iota((tm, tm), 0), 1.0, 0.0).astype(BF16)
    ranks = _dot(before, hot.astype(BF16)) + cnt_ref[...]
    route = jnp.zeros((tm, LANES), F32)
    for k in range(TOP_K):
        rk = jnp.sum(jnp.where(lane == idxs[k], ranks, 0.0), axis=-1, keepdims=True)
        route = route + jnp.where(lane == k, idxs[k].astype(F32), 0.0)
        route = route + jnp.where(lane == TOP_K + k, es[k] / den, 0.0)
        route = route + jnp.where(lane == 2 * TOP_K + k, rk, 0.0)
    route_ref[...] = route
    cnt_ref[...] = cnt_ref[...] + jnp.sum(hot, axis=0, keepdims=True)


def _out_router(xp, xs, mixed, wo, g2, rwh, rwl, rb, tm=256):
    d = xp.shape[1]
    t = xp.shape[0] + xs.shape[0]
    tiles_p = xp.shape[0] // tm
    rows = pl.BlockSpec((tm, d), lambda i: (i, 0))
    lanes = pl.BlockSpec((tm, LANES), lambda i: (i, 0))

    def const(r, c):
        return pl.BlockSpec((r, c), lambda i: (0, 0))

    return pl.pallas_call(
        functools.partial(_out_router_body, tiles_p=tiles_p),
        grid=(t // tm,),
        in_specs=_two_group_specs(tm, d, tiles_p)
        + [rows, const(d, d), const(1, d), const(d, LANES), const(d, LANES), const(1, LANES)],
        out_specs=[rows, rows, lanes, const(1, LANES)],
        out_shape=[jax.ShapeDtypeStruct((t, d), F32), jax.ShapeDtypeStruct((t, d), F32),
                   jax.ShapeDtypeStruct((t, LANES), F32), jax.ShapeDtypeStruct((1, LANES), F32)],
        compiler_params=_cparams("arbitrary"),
        name="out_router",
    )(xp, xs, mixed, wo, g2, rwh, rwl, rb)


def _row_copy(src, r_src, dst, r_dst, sem):
    return pltpu.make_async_copy(src.at[pl.ds(r_src, 1), :], dst.at[pl.ds(r_dst, 1), :], sem)


def _dispatch_body(start_ref, cnt_ref, pend_ref, e_ref, rank_ref, h_ref, xs_ref, zero_ref, sem, zsem):
    tm = h_ref.shape[0]

    def slot(r, k):
        return start_ref[e_ref[r * TOP_K + k]] + rank_ref[r * TOP_K + k]

    def issue(r, _):
        for k in range(TOP_K):
            _row_copy(h_ref, r, xs_ref, slot(r, k), sem).start(priority=k % 2)
        return 0

    def drain(r, _):
        for k in range(TOP_K):
            _row_copy(h_ref, r, xs_ref, slot(r, k), sem).wait()
        return 0

    lax.fori_loop(0, tm, issue, 0)

    @pl.when(pl.program_id(0) == 0)
    def _():
        zero_ref[...] = jnp.zeros(zero_ref.shape, F32)
        zrows = zero_ref.shape[0]
        for phase in ("start", "wait"):
            def per_expert(e, _):
                def per_row(s, _):
                    cp = _row_copy(zero_ref, 0, xs_ref, s, zsem)
                    cp.start() if phase == "start" else cp.wait()
                    return 0
                return lax.fori_loop(start_ref[e] + cnt_ref[e], pend_ref[e], per_row, 0)
            lax.fori_loop(0, N_EXPERTS, per_expert, 0)

            def per_chunk(s, _):
                s0 = pl.multiple_of(s * zrows, zrows)
                cp = pltpu.make_async_copy(zero_ref, xs_ref.at[pl.ds(s0, zrows), :], zsem)
                cp.start() if phase == "start" else cp.wait()
                return 0
            lax.fori_loop(pend_ref[N_EXPERTS - 1] // zrows, xs_ref.shape[0] // zrows, per_chunk, 0)

    lax.fori_loop(0, tm, drain, 0)


def _dispatch(h, e_flat, rank_flat, pad_start, counts, pad_end, n_slots, tm=256):
    t, d = h.shape
    smem = pl.BlockSpec((tm * TOP_K,), lambda i, *_: (i,), memory_space=pltpu.SMEM)
    grid_spec = pltpu.PrefetchScalarGridSpec(
        num_scalar_prefetch=3,
        grid=(t // tm,),
        in_specs=[smem, smem, pl.BlockSpec((tm, d), lambda i, *_: (i, 0))],
        out_specs=pl.BlockSpec(memory_space=pl.ANY),
        scratch_shapes=[pltpu.VMEM((MOE_TILE // 8, d), F32), pltpu.SemaphoreType.DMA(()),
                        pltpu.SemaphoreType.DMA(())],
    )
    return pl.pallas_call(
        _dispatch_body,
        grid_spec=grid_spec,
        out_shape=jax.ShapeDtypeStruct((n_slots, d), F32),
        compiler_params=pltpu.CompilerParams(dimension_semantics=("arbitrary",), vmem_limit_bytes=VMEM_LIMIT,
                                             has_side_effects=True),
        name="moe_dispatch",
    )(pad_start, counts, pad_end, e_flat, rank_flat, h)


def _experts_body(te_ref, nu_ref, rows_ref, xs_ref, wg_ref, wu_ref, bg_ref, bu_ref, wd_ref, bd_ref, y_ref,
                  xb_ref, wgb_ref, wub_ref, wdb_ref):
    del te_ref, nu_ref
    i = pl.program_id(0)
    c = pl.program_id(1)
    n_rows = rows_ref[i]

    @pl.when((n_rows > 0) & (c == 0))
    def _():
        xb_ref[...] = xs_ref[...].astype(BF16)

    @pl.when(n_rows > 0)
    def _():
        wgb_ref[...] = wg_ref[...].astype(BF16)
        wub_ref[...] = wu_ref[...].astype(BF16)
        wdb_ref[...] = wd_ref[...].astype(BF16)

    for sb in range(MOE_TILE // MOE_SUB):
        rows = pl.ds(sb * MOE_SUB, MOE_SUB)

        @pl.when(n_rows > sb * MOE_SUB)
        def _():
            x = xb_ref[rows, :]
            g = jnp.minimum(_dot(x, wgb_ref[...]) + bg_ref[...], SWIGLU_LIMIT)
            u = jnp.clip(_dot(x, wub_ref[...]) + bu_ref[...], -SWIGLU_LIMIT, SWIGLU_LIMIT)
            a = (u + 1.0) * g * _sigmoid(g * SWIGLU_ALPHA)
            part = _dot(a.astype(BF16), wdb_ref[...])

            @pl.when(c == 0)
            def _():
                y_ref[rows, :] = part + bd_ref[...]

            @pl.when(c > 0)
            def _():
                y_ref[rows, :] = y_ref[rows, :] + part

        @pl.when((n_rows <= sb * MOE_SUB) & (c == 0))
        def _():
            y_ref[rows, :] = jnp.zeros((MOE_SUB, y_ref.shape[1]), F32)


def _experts(xs, tile_e, n_used, tile_rows, w_gu, b_gu, w_dn, b_dn):
    n_slots, d = xs.shape
    n_tiles = n_slots // MOE_TILE
    n_ff = D_FF // MOE_FF_TILE
    tf = MOE_FF_TILE

    def last_used(i, nu):
        return jnp.minimum(i, nu[0] - 1)

    def chunk(i, c, nu):
        return jnp.where(i < nu[0], c, n_ff - 1)

    grid_spec = pltpu.PrefetchScalarGridSpec(
        num_scalar_prefetch=3,
        grid=(n_tiles, n_ff),
        in_specs=[
            pl.BlockSpec((MOE_TILE, d), lambda i, c, te, nu, nr: (last_used(i, nu), 0)),
            pl.BlockSpec((None, d, tf), lambda i, c, te, nu, nr: (te[i], 0, chunk(i, c, nu))),
            pl.BlockSpec((None, d, tf), lambda i, c, te, nu, nr: (te[i], 0, n_ff + chunk(i, c, nu))),
            pl.BlockSpec((None, 1, tf), lambda i, c, te, nu, nr: (te[i], 0, chunk(i, c, nu))),
            pl.BlockSpec((None, 1, tf), lambda i, c, te, nu, nr: (te[i], 0, n_ff + chunk(i, c, nu))),
            pl.BlockSpec((None, tf, d), lambda i, c, te, nu, nr: (te[i], chunk(i, c, nu), 0)),
            pl.BlockSpec((None, 1, d), lambda i, c, te, nu, nr: (te[i], 0, 0)),
        ],
        out_specs=pl.BlockSpec((MOE_TILE, d), lambda i, c, te, nu, nr: (i, 0)),
        scratch_shapes=[pltpu.VMEM((MOE_TILE, d), BF16), pltpu.VMEM((d, tf), BF16), pltpu.VMEM((d, tf), BF16),
                        pltpu.VMEM((tf, d), BF16)],
    )
    return pl.pallas_call(
        _experts_body,
        grid_spec=grid_spec,
        out_shape=jax.ShapeDtypeStruct((n_slots, d), F32),
        compiler_params=_cparams("arbitrary", "arbitrary"),
        name="moe_experts",
    )(tile_e, n_used, tile_rows, xs, w_gu, w_gu, b_gu, b_gu, w_dn, b_dn)


def _combine_body(start_ref, e_ref, rank_ref, x1_ref, route_ref, y_ref, op_ref, os_ref, buf_ref, sem, *, tiles_p):
    tm = x1_ref.shape[0]

    def slot(r, k):
        return start_ref[e_ref[r * TOP_K + k]] + rank_ref[r * TOP_K + k]

    def issue(r, _):
        for k in range(TOP_K):
            _row_copy(y_ref, slot(r, k), buf_ref.at[k], r, sem).start(priority=k % 2)
        return 0

    def drain(r, _):
        for k in range(TOP_K):
            _row_copy(y_ref, slot(r, k), buf_ref.at[k], r, sem).wait()
        return 0

    lax.fori_loop(0, tm, issue, 0)
    lax.fori_loop(0, tm, drain, 0)
    route = route_ref[...]
    out = x1_ref[...]
    for k in range(TOP_K):
        out = out + route[:, TOP_K + k:TOP_K + k + 1] * buf_ref[k]

    @pl.when(pl.program_id(0) < tiles_p)
    def _():
        op_ref[...] = out

    @pl.when(pl.program_id(0) >= tiles_p)
    def _():
        os_ref[...] = out


def _combine(x1, route, y, e_flat, rank_flat, pad_start, n_prompt, tm=256):
    t, d = x1.shape
    tiles_p = n_prompt // tm
    smem = pl.BlockSpec((tm * TOP_K,), lambda i, *_: (i,), memory_space=pltpu.SMEM)
    grid_spec = pltpu.PrefetchScalarGridSpec(
        num_scalar_prefetch=1,
        grid=(t // tm,),
        in_specs=[smem, smem, pl.BlockSpec((tm, d), lambda i, *_: (i, 0)),
                  pl.BlockSpec((tm, LANES), lambda i, *_: (i, 0)),
                  pl.BlockSpec(memory_space=pl.ANY)],
        out_specs=[pl.BlockSpec((tm, d), lambda i, *_: (jnp.minimum(i, tiles_p - 1), 0)),
                   pl.BlockSpec((tm, d), lambda i, *_: (jnp.maximum(i - tiles_p, 0), 0))],
        scratch_shapes=[pltpu.VMEM((TOP_K, tm, d), F32), pltpu.SemaphoreType.DMA(())],
    )
    return pl.pallas_call(
        functools.partial(_combine_body, tiles_p=tiles_p),
        grid_spec=grid_spec,
        out_shape=[jax.ShapeDtypeStruct((n_prompt, d), F32), jax.ShapeDtypeStruct((t - n_prompt, d), F32)],
        compiler_params=_cparams("arbitrary"),
        name="moe_combine",
    )(pad_start, e_flat, rank_flat, x1, route, y)


def _moe(x1, h, route, counts, w_gu, b_gu, w_dn, b_dn, n_prompt):
    t = x1.shape[0]
    n_tiles = -(-t * TOP_K // MOE_TILE) + N_EXPERTS
    n_slots = n_tiles * MOE_TILE
    cnt = counts[0, :N_EXPERTS].astype(jnp.int32)
    padded = (cnt + MOE_TILE - 1) // MOE_TILE * MOE_TILE
    pad_end = jnp.cumsum(padded)
    pad_start = pad_end - padded
    n_used = (pad_end[-1] // MOE_TILE).reshape(1)
    tile_first = jnp.minimum(jnp.arange(n_tiles, dtype=jnp.int32), n_used[0] - 1) * MOE_TILE
    tile_e = jnp.minimum(jnp.searchsorted(pad_end, tile_first, side="right"), N_EXPERTS - 1).astype(jnp.int32)
    tile_idx = jnp.arange(n_tiles, dtype=jnp.int32)
    tile_rows = jnp.clip(pad_start[tile_e] + cnt[tile_e] - tile_idx * MOE_TILE, 0, MOE_TILE)
    tile_rows = jnp.where(tile_idx < n_used[0], tile_rows, 0).astype(jnp.int32)
    e_flat = route[:, :TOP_K].astype(jnp.int32).reshape(-1)
    rank_flat = route[:, 2 * TOP_K:3 * TOP_K].astype(jnp.int32).reshape(-1)
    xs = _dispatch(h, e_flat, rank_flat, pad_start, cnt, pad_end, n_slots)
    y = _experts(xs, tile_e, n_used, tile_rows, w_gu, b_gu.reshape(N_EXPERTS, 1, -1), w_dn,
                 b_dn.reshape(N_EXPERTS, 1, -1))
    return _combine(x1, route, y, e_flat, rank_flat, pad_start, n_prompt)


def _layer(xp, xs, pools, kwbuf, vwbuf, s_ret, page_table, p):
    batch, seq, d = xp.shape
    dec_batch, t_new, _ = xs.shape
    n_pages = page_table.shape[1]
    past = n_pages * PAGE_SIZE
    n_prompt = batch * seq
    n_sample = dec_batch * t_new
    assert seq & (seq - 1) == 0 and t_new & (t_new - 1) == 0 and seq % SEL_KV_TILE == 0
    assert n_prompt % ROW_TILE == 0 and n_sample % ROW_TILE == 0 and t_new % 8 == 0

    xp2, xs2 = xp.reshape(n_prompt, d), xs.reshape(n_sample, d)
    t_all = n_prompt + n_sample
    xn = _ln1(xp2, xs2, p["ln1_g"])

    w_in = p["w_in"]
    offs = [0]
    for n in (NSA_Q, KV_W, KV_W, KV_W, KV_W, KV_W, KV_W, 3 * NSA_HEADS, RET_QK, RET_QK, RET_V, RET_V, D_MODEL, D_MODEL):
        offs.append(offs[-1] + n)
    w_q = w_in[:, offs[0]:offs[1]].astype(BF16)
    w_kv = jnp.concatenate([w_in[:, offs[1]:offs[8]],
                            jnp.zeros((d, LANES - 3 * NSA_HEADS), F32)], axis=1).astype(BF16)
    w_rot = w_in[:, offs[8]:offs[10]].astype(BF16)
    w_act = w_in[:, offs[10]:offs[14]].astype(BF16)

    def rows(n):
        return pl.BlockSpec((ROW_TILE, n), lambda j, i: (i, 0))

    q = _proj_call(_proj_q_body, xn, w_q, [p["q_norm_g"].reshape(1, HEAD_DIM)], [_vec_spec(HEAD_DIM)],
                   jax.ShapeDtypeStruct((t_all, NSA_Q), BF16), rows(NSA_Q), NSA_Q, "proj_q")
    row2 = jax.ShapeDtypeStruct((NSA_GROUPS * t_all, HEAD_DIM), F32)
    row2_spec = pl.BlockSpec((NSA_GROUPS * ROW_TILE, HEAD_DIM), lambda j, i: (i, 0))
    *kv_rows, kvb, gate = _proj_call(
        _proj_kv_body, xn, w_kv,
        [p["k_sel_norm_g"].reshape(1, HEAD_DIM), p["k_win_norm_g"].reshape(1, HEAD_DIM)],
        [_vec_spec(HEAD_DIM), _vec_spec(HEAD_DIM)],
        [row2] * 6 + [jax.ShapeDtypeStruct((t_all, 4 * KV_W), BF16), jax.ShapeDtypeStruct((t_all, LANES), F32)],
        [row2_spec] * 6 + [rows(4 * KV_W), rows(LANES)], 6 * KV_W + LANES, "proj_kv")
    inv = 1.0 / (ROPE_BASE ** jnp.linspace(0.0, 1.0, RET_DK // 2, dtype=F32))
    rot = _proj_call(
        functools.partial(_proj_rot_body, n_prompt=n_prompt, seq=seq, past=past, dec_seq=t_new),
        xn, w_rot, [jnp.repeat(inv, 2).reshape(1, RET_DK)], [_vec_spec(RET_DK)],
        jax.ShapeDtypeStruct((t_all, 2 * RET_QK), BF16),
        pl.BlockSpec((ROW_TILE, RET_QK), lambda j, i: (i, j)), RET_QK, "proj_rot")
    act_tn = 1024
    act = _proj_call(
        functools.partial(_proj_act_body, tiles_plain=RET_V // act_tn, tiles_silu=RET_V // act_tn),
        xn, w_act, [], [], jax.ShapeDtypeStruct((t_all, 2 * RET_V + 2 * D_MODEL), BF16),
        pl.BlockSpec((ROW_TILE, act_tn), lambda j, i: (i, j)), act_tn, "proj_act")

    cmp_w = (_compress_weights(p["cmp_k_w1"], p["cmp_k_b1"], p["cmp_k_w2"])
             + _compress_weights(p["cmp_v_w1"], p["cmp_v_b1"], p["cmp_v_w2"])
             + (p["k_cmp_norm_g"].reshape(1, HEAD_DIM),))
    gn = p["ret_norm_g"].reshape(1, RET_V)

    kc_p, vc_p = _compress_prompt(kv_rows[0], kv_rows[1], cmp_w, batch, seq)
    o_nsa_p = _nsa_prompt(q, gate, kc_p, vc_p, kvb, batch, seq)
    o_ret_p, ret_p = _ret_prompt(rot, act, gn, batch, seq)

    pool_kc, pool_vc, pool_ks, pool_vs = [
        a.reshape(a.shape[0], NSA_GROUPS * PAGE_SIZE, HEAD_DIM) for a in pools]
    pt_flat = page_table.reshape(-1)
    kc_s, vc_s = _compress_sample(pool_kc, pool_vc, pt_flat, cmp_w, dec_batch, n_pages)
    wb = kwbuf.shape[1]
    o_nsa_s, kw_s, vw_s = _nsa_sample(
        q, gate, kv_rows[2:], n_prompt // (2 * t_new), kc_s, vc_s, pool_ks, pool_vs,
        kwbuf.reshape(dec_batch, wb * NSA_GROUPS, HEAD_DIM), vwbuf.reshape(dec_batch, wb * NSA_GROUPS, HEAD_DIM),
        pt_flat, dec_batch, t_new, n_pages)
    rot_s = rot[n_prompt:].astype(F32)
    act_s = act[n_prompt:, :2 * RET_V].astype(F32)
    o_ret_s, ret_s = _ret_sample(rot_s[:, :RET_QK], rot_s[:, RET_QK:], act_s[:, :RET_V], act_s[:, RET_V:],
                                 gn, s_ret, dec_batch, t_new)

    o_nsa = jnp.concatenate([o_nsa_p, o_nsa_s], axis=0)
    o_ret = jnp.concatenate([o_ret_p, o_ret_s.astype(BF16)], axis=0)
    mixed = _merge(o_nsa, o_ret, p["w_nsa_br"].astype(BF16), p["w_ret_br"].astype(BF16), act)
    rw = jnp.concatenate([p["router_w"], jnp.zeros((d, LANES - N_EXPERTS), F32)], axis=1)
    rwh, rwl = _split_bf16(rw)
    rb = jnp.concatenate([p["router_b"], jnp.zeros((LANES - N_EXPERTS,), F32)]).reshape(1, LANES)
    x1, h, route, counts = _out_router(xp2, xs2, mixed, p["w_out"].astype(BF16), p["ln2_g"].reshape(1, d),
                                       rwh, rwl, rb)
    y_p, y_s = _moe(x1, h, route, counts, p["w_gate_up"], p["b_gate_up"], p["w_down"], p["b_down"], n_prompt)

    kv5 = (NSA_GROUPS, HEAD_DIM)
    wbp = min(WINDOW, seq)
    rows_p = [a[:NSA_GROUPS * n_prompt].reshape(batch, seq, *kv5) for a in kv_rows]
    rows_s = [a[NSA_GROUPS * n_prompt:].reshape(dec_batch, t_new, *kv5) for a in kv_rows[:4]]
    states_p = (*rows_p[:4], rows_p[4][:, seq - wbp:], rows_p[5][:, seq - wbp:], ret_p)
    states_s = (*rows_s, kw_s.reshape(dec_batch, wb, *kv5), vw_s.reshape(dec_batch, wb, *kv5), ret_s)
    return y_p.reshape(batch, seq, d), y_s.reshape(dec_batch, t_new, d), states_p, states_s


def kernel(x_prompt, x_sample, cache_k_cmp, cache_v_cmp, cache_k_sel, cache_v_sel, state_k_win, state_v_win,
           state_ret, page_table, ln1_g, w_in, cmp_k_w1, cmp_k_b1, cmp_k_w2, cmp_v_w1, cmp_v_b1, cmp_v_w2,
           q_norm_g, k_cmp_norm_g, k_sel_norm_g, k_win_norm_g, ret_norm_g, w_nsa_br, w_ret_br, w_out, ln2_g,
           router_w, router_b, w_gate_up, b_gate_up, w_down, b_down):
    depth = w_in.shape[0]
    y_p, y_s = x_prompt, x_sample
    new_p, new_s = [], []
    for l in range(depth):
        p = {
            "ln1_g": ln1_g[l], "w_in": w_in[l],
            "cmp_k_w1": cmp_k_w1[l], "cmp_k_b1": cmp_k_b1[l], "cmp_k_w2": cmp_k_w2[l],
            "cmp_v_w1": cmp_v_w1[l], "cmp_v_b1": cmp_v_b1[l], "cmp_v_w2": cmp_v_w2[l],
            "q_norm_g": q_norm_g[l], "k_cmp_norm_g": k_cmp_norm_g[l], "k_sel_norm_g": k_sel_norm_g[l],
            "k_win_norm_g": k_win_norm_g[l], "ret_norm_g": ret_norm_g[l],
            "w_nsa_br": w_nsa_br[l], "w_ret_br": w_ret_br[l], "w_out": w_out[l], "ln2_g": ln2_g[l],
            "router_w": router_w[l], "router_b": router_b[l], "w_gate_up": w_gate_up[l],
            "b_gate_up": b_gate_up[l], "w_down": w_down[l], "b_down": b_down[l],
        }
        pools = (cache_k_cmp[l], cache_v_cmp[l], cache_k_sel[l], cache_v_sel[l])
        y_p, y_s, sp, ss = _layer(y_p, y_s, pools, state_k_win[l], state_v_win[l], state_ret[l], page_table, p)
        new_p.append(sp)
        new_s.append(ss)
    outs_p = [jnp.stack(a) for a in zip(*new_p)]
    outs_s = [jnp.stack(a) for a in zip(*new_s)]
    return (y_p, y_s, *outs_p, *outs_s)
```

```python
import functools
import math

import jax
import jax.numpy as jnp
from jax import lax
from jax.experimental import pallas as pl
from jax.experimental.pallas import tpu as pltpu

F32 = jnp.float32
BF16 = jnp.bfloat16

D_MODEL = 2048
PAGE_SIZE = 128
NSA_HEADS = 8
NSA_GROUPS = 2
HPG = NSA_HEADS // NSA_GROUPS
HEAD_DIM = 128
CMP_LEN = 32
CMP_STRIDE = 16
CMP_HIDDEN = 2 * HEAD_DIM
SEL_BLK = 64
SEL_SHIFT = 6
SEL_TOPK = 16
WINDOW = 512
Q_BLK = 128
RET_HEADS = 8
RET_DK = 128
RET_DV = 256
RET_CHUNK = 128
ROPE_BASE = 10000.0
N_EXPERTS = 32
TOP_K = 4
D_FF = D_MODEL
SWIGLU_LIMIT = 7.0
SWIGLU_ALPHA = 1.702
EPS = 1e-6
NEG = -1e30
BIG = 1e30

NSA_Q = NSA_HEADS * HEAD_DIM
KV_W = NSA_GROUPS * HEAD_DIM
RET_QK = RET_HEADS * RET_DK
RET_V = RET_HEADS * RET_DV

LANES = 128
ROW_TILE = 512
SEL_KV_TILE = 512
MOE_TILE = 512
STAGE_PITCH = 24
ROW_PITCH = 17
MOE_SUB = 256
MOE_FF_TILE = 512
VMEM_LIMIT = 56 * 1024 * 1024


def _cparams(*sem):
    return pltpu.CompilerParams(dimension_semantics=sem, vmem_limit_bytes=VMEM_LIMIT)


def _dot(a, b):
    return jnp.dot(a, b, preferred_element_type=F32)


def _dot_nt(a, b):
    return lax.dot_general(a, b, (((1,), (1,)), ((), ())), preferred_element_type=F32)


def _dot_tn(a, b):
    return lax.dot_general(a, b, (((0,), (0,)), ((), ())), preferred_element_type=F32)


def _sigmoid(x):
    return 1.0 / (1.0 + jnp.exp(-x))


def _unit_rms(x):
    return x * lax.rsqrt(jnp.mean(x * x, axis=-1, keepdims=True) + EPS)


def _iota(shape, dim):
    return lax.broadcasted_iota(jnp.int32, shape, dim)


def _split_bf16(x):
    hi = x.astype(BF16)
    lo = (x - hi.astype(F32)).astype(BF16)
    return hi, lo


def _two_group_specs(tm, d, tiles_p):
    return [pl.BlockSpec((tm, d), lambda i: (jnp.minimum(i, tiles_p - 1), 0)),
            pl.BlockSpec((tm, d), lambda i: (jnp.maximum(i - tiles_p, 0), 0))]


def _ln1_body(xp_ref, xs_ref, g_ref, o_ref, *, tiles_p):
    def norm(x_ref):
        o_ref[...] = (_unit_rms(x_ref[...]) * g_ref[...]).astype(o_ref.dtype)

    pl.when(pl.program_id(0) < tiles_p)(lambda: norm(xp_ref))
    pl.when(pl.program_id(0) >= tiles_p)(lambda: norm(xs_ref))


def _ln1(xp, xs, g):
    d = xp.shape[1]
    tiles_p = xp.shape[0] // ROW_TILE
    t = xp.shape[0] + xs.shape[0]
    return pl.pallas_call(
        functools.partial(_ln1_body, tiles_p=tiles_p),
        grid=(t // ROW_TILE,),
        in_specs=_two_group_specs(ROW_TILE, d, tiles_p) + [pl.BlockSpec((1, d), lambda i: (0, 0))],
        out_specs=pl.BlockSpec((ROW_TILE, d), lambda i: (i, 0)),
        out_shape=jax.ShapeDtypeStruct((t, d), BF16),
        compiler_params=_cparams("arbitrary"),
        name="ln1",
    )(xp, xs, g.reshape(1, d))


def _proj_q_body(x_ref, w_ref, g_ref, o_ref):
    y = _dot(x_ref[...], w_ref[...])
    g = g_ref[...] * (HEAD_DIM ** -0.5)
    for c in range(NSA_HEADS):
        sl = slice(c * LANES, (c + 1) * LANES)
        o_ref[:, sl] = (_unit_rms(y[:, sl]) * g).astype(o_ref.dtype)


def _proj_kv_body(x_ref, w_ref, gs_ref, gw_ref, kc_ref, vc_ref, ks_ref, vs_ref, kw_ref, vw_ref, kvb_ref, gate_ref):
    y = _dot(x_ref[...], w_ref[...])
    tm = y.shape[0]
    outs = (kc_ref, vc_ref, ks_ref, vs_ref, kw_ref, vw_ref)
    for c in range(12):
        sl = slice(c * LANES, (c + 1) * LANES)
        yc = y[:, sl]
        if c in (4, 5):
            yc = _unit_rms(yc) * gs_ref[...]
        elif c in (8, 9):
            yc = _unit_rms(yc) * gw_ref[...]
        outs[c // NSA_GROUPS][pl.ds(c % NSA_GROUPS, tm, stride=NSA_GROUPS), :] = yc
        if c >= 4:
            kvb_ref[:, (c - 4) * LANES:(c - 3) * LANES] = yc.astype(BF16)
    gate_ref[...] = _sigmoid(y[:, 12 * LANES:13 * LANES])


def _proj_rot_body(x_ref, w_ref, inv_ref, o_ref, *, n_prompt, seq, past, dec_seq):
    j = pl.program_id(0)
    i = pl.program_id(1)
    y = _dot(x_ref[...], w_ref[...])
    tm = y.shape[0]
    row = i * tm + _iota((tm, LANES), 0)
    pos = jnp.where(row < n_prompt, row & (seq - 1), past + ((row - n_prompt) & (dec_seq - 1)))
    ang = pos.astype(F32) * inv_ref[...]
    cos = jnp.cos(ang)
    sin = jnp.sin(ang)
    even = (_iota((tm, LANES), 1) & 1) == 0
    sin = jnp.where(even, -sin, sin)
    scale = jnp.where(j == 0, 1.0, RET_DK ** -0.5).astype(F32)
    for c in range(RET_HEADS):
        sl = slice(c * LANES, (c + 1) * LANES)
        yc = y[:, sl]
        partner = jnp.where(even, pltpu.roll(yc, LANES - 1, 1), pltpu.roll(yc, 1, 1))
        o_ref[:, sl] = ((yc * cos + partner * sin) * scale).astype(o_ref.dtype)


def _proj_act_body(x_ref, w_ref, o_ref, *, act):
    y = _dot(x_ref[...], w_ref[...])
    if act == "silu":
        y = y * _sigmoid(y)
    elif act == "sigmoid":
        y = _sigmoid(y)
    o_ref[...] = y.astype(o_ref.dtype)


def _proj_call(body, xn, w, extra, extra_specs, out_shape, out_specs, tn, name):
    t, k = xn.shape
    n = w.shape[1]
    in_specs = [pl.BlockSpec((ROW_TILE, k), lambda j, i: (i, 0)),
                pl.BlockSpec((k, tn), lambda j, i: (0, j))] + extra_specs
    return pl.pallas_call(
        body,
        grid=(n // tn, t // ROW_TILE),
        in_specs=in_specs,
        out_specs=out_specs,
        out_shape=out_shape,
        compiler_params=_cparams("parallel", "parallel"),
        name=name,
    )(xn, w, *extra)


def _vec_spec(n):
    return pl.BlockSpec((1, n), lambda j, i: (0, 0))


def _compress_mlp(x_cat, w1_ref, b1_ref, w2_ref):
    nh = x_cat.shape[0]
    a = _dot(x_cat, w1_ref[...])
    pre = a[:, :CMP_HIDDEN] + pltpu.roll(a[:, CMP_HIDDEN:], nh - 1, 0) + b1_ref[...]
    hid = pre * (0.5 * (1.0 + jnp.tanh(math.sqrt(2.0 / math.pi) * (pre + 0.044715 * (pre * pre * pre)))))
    out = _dot(hid.astype(BF16), w2_ref[...])
    return jnp.where(_iota(out.shape, 0) < nh - 1, out, 0.0)


def _compress_finish(xk, xv, wrefs, kc_ref, vc_ref):
    w1k, b1k, w2k, w1v, b1v, w2v, gk = wrefs
    for g in range(NSA_GROUPS):
        sl = slice(g * LANES, (g + 1) * LANES)
        kc = _compress_mlp(xk[g], w1k, b1k, w2k)
        kc_ref[:, sl] = (_unit_rms(kc) * gk[...]).astype(kc_ref.dtype)
        vc_ref[:, sl] = _compress_mlp(xv[g], w1v, b1v, w2v).astype(vc_ref.dtype)


def _compress_prompt_body(krows_ref, vrows_ref, *refs, nh):
    wrefs, (kc_ref, vc_ref) = refs[:7], refs[7:]

    def gather(rows_ref, g):
        return jnp.concatenate(
            [rows_ref[pl.ds(NSA_GROUPS * s + g, nh, stride=NSA_GROUPS * CMP_STRIDE), :].astype(BF16)
             for s in range(CMP_STRIDE)], axis=1)

    xk = [gather(krows_ref, g) for g in range(NSA_GROUPS)]
    xv = [gather(vrows_ref, g) for g in range(NSA_GROUPS)]
    _compress_finish(xk, xv, wrefs, kc_ref, vc_ref)


def _compress_sample_body(pt_ref, *refs, n_pages):
    del pt_ref
    kpages, vpages = refs[:n_pages], refs[n_pages:2 * n_pages]
    wrefs = refs[2 * n_pages:2 * n_pages + 7]
    kc_ref, vc_ref, stage_ref = refs[2 * n_pages + 7:]
    per_page = PAGE_SIZE // CMP_STRIDE
    nh = n_pages * per_page

    def gather(pages, g, slab):
        stage = stage_ref.at[slab]
        for p, page in enumerate(pages):
            for m in range(per_page):
                half = page[pl.ds(NSA_GROUPS * CMP_STRIDE * m + g, CMP_STRIDE, stride=NSA_GROUPS), :]
                stage[pl.ds((p * per_page + m) * STAGE_PITCH, CMP_STRIDE), :] = half
        return jnp.concatenate(
            [stage[pl.ds(s, nh, stride=STAGE_PITCH), :].astype(BF16) for s in range(CMP_STRIDE)], axis=1)

    xk = [gather(kpages, g, g) for g in range(NSA_GROUPS)]
    xv = [gather(vpages, g, NSA_GROUPS + g) for g in range(NSA_GROUPS)]
    _compress_finish(xk, xv, wrefs, kc_ref, vc_ref)


def _const_spec(shape, nargs):
    zeros = (0,) * len(shape)
    return pl.BlockSpec(shape, lambda *a: zeros)


def _compress_weights(w1, b1, w2):
    r_n = CMP_LEN // CMP_STRIDE
    w1r = w1.reshape(r_n, CMP_STRIDE * HEAD_DIM, CMP_HIDDEN)
    w1cat = jnp.concatenate([w1r[r] for r in range(r_n)], axis=1).astype(BF16)
    return w1cat, b1.reshape(1, CMP_HIDDEN), w2.astype(BF16)


def _compress_weight_specs():
    k16 = CMP_STRIDE * HEAD_DIM
    one = [_const_spec((k16, 2 * CMP_HIDDEN), 0), _const_spec((1, CMP_HIDDEN), 0),
           _const_spec((CMP_HIDDEN, HEAD_DIM), 0)]
    return one + one + [_const_spec((1, HEAD_DIM), 0)]


def _compress_prompt(krows, vrows, weights, batch, seq):
    nh = seq // CMP_STRIDE
    out = jax.ShapeDtypeStruct((batch, nh, KV_W), BF16)
    ospec = pl.BlockSpec((None, nh, KV_W), lambda b: (b, 0, 0))
    return pl.pallas_call(
        functools.partial(_compress_prompt_body, nh=nh),
        grid=(batch,),
        in_specs=[pl.BlockSpec((NSA_GROUPS * seq, LANES), lambda b: (b, 0))] * 2 + _compress_weight_specs(),
        out_specs=[ospec, ospec],
        out_shape=[out, out],
        compiler_params=_cparams("parallel"),
        name="compress_prompt",
    )(krows, vrows, *weights)


def _page_specs(n_pages, nb=1, bb=0):
    def spec(p):
        return pl.BlockSpec((None, NSA_GROUPS * PAGE_SIZE, HEAD_DIM),
                            lambda b, pt: (pt[(b * nb + bb) * n_pages + p], 0, 0))
    return [spec(p) for p in range(n_pages)]


def _compress_sample(pool_k, pool_v, pt_flat, weights, dec_batch, n_pages):
    nh = n_pages * PAGE_SIZE // CMP_STRIDE
    out = jax.ShapeDtypeStruct((dec_batch, nh, KV_W), BF16)
    ospec = pl.BlockSpec((None, nh, KV_W), lambda b, pt: (b, 0, 0))
    grid_spec = pltpu.PrefetchScalarGridSpec(
        num_scalar_prefetch=1,
        grid=(dec_batch,),
        in_specs=_page_specs(n_pages) + _page_specs(n_pages) + _compress_weight_specs(),
        out_specs=[ospec, ospec],
        scratch_shapes=[pltpu.VMEM((2 * NSA_GROUPS, nh * STAGE_PITCH, HEAD_DIM), F32)],
    )
    return pl.pallas_call(
        functools.partial(_compress_sample_body, n_pages=n_pages),
        grid_spec=grid_spec,
        out_shape=[out, out],
        compiler_params=_cparams("parallel"),
        name="compress_sample",
    )(pt_flat, *([pool_k] * n_pages), *([pool_v] * n_pages), *weights)


def _masked_softmax(s3, ok):
    s3 = jnp.where(ok[None], s3, NEG)
    m = jnp.max(s3, axis=-1, keepdims=True)
    e = jnp.where(ok[None], jnp.exp(s3 - m), 0.0)
    den = jnp.sum(e, axis=-1, keepdims=True)
    return e / jnp.where(den > 0.0, den, 1.0)


def _topk_lanes(score, n_sel, k):
    lane = _iota(score.shape, 1)
    sc = jnp.where(lane < n_sel, score, -jnp.inf)
    rank = jnp.zeros(score.shape, F32)
    for i in range(n_sel):
        ci = sc[:, i:i + 1]
        later = jnp.where(lane > i, 1.0, 0.0)
        rank = rank + jnp.where(ci > sc, 1.0, jnp.where(ci == sc, later, 0.0))
    return jnp.where((rank < k) & (lane < n_sel), 1.0, 0.0)


def _bias_softmax_pv(s3, ok, v):
    h, t, n = s3.shape
    s3 = s3 + jnp.where(ok, 0.0, NEG)[None]
    e = jnp.exp(s3 - jnp.max(s3, axis=-1, keepdims=True))
    inv = 1.0 / jnp.sum(e, axis=-1, keepdims=True)
    return _dot(e.reshape(h * t, n).astype(BF16), v) * inv.reshape(h * t, 1)


def _compressed_branch(qg, kcg, vcg, qpos, n_cmp):
    t = qpos.shape[0]
    nc_pad = kcg.shape[0]
    c_idx = _iota((t, nc_pad), 1)
    vis = (c_idx * CMP_STRIDE + (CMP_LEN - 1) <= qpos) & (c_idx < n_cmp)
    p3 = _masked_softmax(_dot_nt(qg, kcg).reshape(HPG, t, nc_pad), vis)
    o_c = _dot(p3.reshape(HPG * t, nc_pad).astype(BF16), vcg)
    return o_c, jnp.sum(p3, axis=0)


def _overlap(nc_pad, blocks_first):
    shape = (LANES, nc_pad) if blocks_first else (nc_pad, LANES)
    ci = _iota(shape, 1 if blocks_first else 0) * CMP_STRIDE
    sj = _iota(shape, 0 if blocks_first else 1) * SEL_BLK
    return jnp.where((ci < sj + SEL_BLK) & (ci + CMP_LEN > sj), 1.0, 0.0).astype(BF16)


def _select_lanes(psum, qpos, n_sel):
    t, nc_pad = psum.shape
    hi, lo = _split_bf16(psum)
    ov = _overlap(nc_pad, False)
    score = _dot(hi, ov) + _dot(lo, ov)
    j = _iota((t, LANES), 1)
    cur = qpos >> SEL_SHIFT
    forced = (j == 0) | (j == cur) | (j == cur - 1)
    score = jnp.where(forced, BIG, score)
    score = jnp.where(j * SEL_BLK <= qpos, score, NEG)
    return _topk_lanes(score, n_sel, min(SEL_TOPK, n_sel))


def _select_sublanes(psum, start, n_sel):
    t, nc_pad = psum.shape
    hi, lo = _split_bf16(psum)
    ov = _overlap(nc_pad, True)
    rows = -(-n_sel // 8) * 8
    sc = (_dot_nt(ov, hi) + _dot_nt(ov, lo))[:rows]
    j = _iota((rows, t), 0)
    qp = start + _iota((rows, t), 1)
    cur = qp >> SEL_SHIFT
    forced = (j == 0) | (j == cur) | (j == cur - 1)
    sc = jnp.where(forced, BIG, sc)
    sc = jnp.where(j * SEL_BLK <= qp, sc, NEG)
    sc = jnp.where(j < n_sel, sc, -jnp.inf)
    rank = jnp.zeros((rows, t), F32)
    for i in range(n_sel):
        ri = sc[i:i + 1, :]
        later = jnp.where(j > i, 1.0, 0.0)
        rank = rank + jnp.where(ri > sc, 1.0, jnp.where(ri == sc, later, 0.0))
    sel = jnp.where((rank < min(SEL_TOPK, n_sel)) & (j < n_sel), 1.0, 0.0).astype(BF16)
    if rows < LANES:
        sel = jnp.concatenate([sel, jnp.zeros((LANES - rows, t), BF16)], axis=0)
    return sel


def _block_to_key(blk0, n):
    blk = blk0 + (_iota((LANES, n), 1) >> SEL_SHIFT)
    return jnp.where(_iota((LANES, n), 0) == blk, 1.0, 0.0).astype(BF16)


def _nsa_prompt_body(q_ref, gate_ref, kc_ref, vc_ref, kv_ref, o_ref, *, seq):
    tq = Q_BLK
    start = pl.program_id(1) * tq
    n_sel = seq // SEL_BLK
    n_cmp = seq // CMP_STRIDE - CMP_LEN // CMP_STRIDE + 1
    qpos = start + _iota((tq, 1), 0)
    gates = gate_ref[...]
    wk = WINDOW + tq
    for g in range(NSA_GROUPS):
        gsl = slice(g * LANES, (g + 1) * LANES)
        qg = jnp.concatenate(
            [q_ref[:, (g * HPG + j) * LANES:(g * HPG + j + 1) * LANES] for j in range(HPG)], axis=0)
        o_c, psum = _compressed_branch(qg, kc_ref[:, gsl], vc_ref[:, gsl], qpos, n_cmp)
        sel_t = _select_sublanes(psum, start, n_sel)

        def sel_step(k, carry):
            m, l, acc = carry
            k0 = pl.multiple_of(k * SEL_KV_TILE, SEL_KV_TILE)
            kt = kv_ref[pl.ds(k0, SEL_KV_TILE), pl.ds(g * LANES, LANES)]
            vt = kv_ref[pl.ds(k0, SEL_KV_TILE), pl.ds(KV_W + g * LANES, LANES)]
            chosen = _dot_tn(sel_t, _block_to_key(k * (SEL_KV_TILE // SEL_BLK), SEL_KV_TILE))
            ok = (chosen > 0.5) & (k0 + _iota((tq, SEL_KV_TILE), 1) <= qpos)
            s3 = _dot_nt(qg, kt).reshape(HPG, tq, SEL_KV_TILE) + jnp.where(ok, 0.0, NEG)[None]
            m_new = jnp.maximum(m, jnp.max(s3, axis=-1, keepdims=True))
            alpha = jnp.exp(m - m_new)
            p = jnp.exp(s3 - m_new)
            l = alpha * l + jnp.sum(p, axis=-1, keepdims=True)
            pv = _dot(p.reshape(HPG * tq, SEL_KV_TILE).astype(BF16), vt)
            return m_new, l, alpha * acc + pv.reshape(HPG, tq, LANES)

        n_kv = (start + tq + SEL_KV_TILE - 1) // SEL_KV_TILE
        init = (jnp.full((HPG, tq, 1), NEG, F32), jnp.zeros((HPG, tq, 1), F32),
                jnp.zeros((HPG, tq, LANES), F32))
        _, l_s, acc_s = lax.fori_loop(0, n_kv, sel_step, init)
        o_s = acc_s * (1.0 / l_s)

        w0 = pl.multiple_of(jnp.maximum(start - WINDOW, 0), Q_BLK)
        kw = kv_ref[pl.ds(w0, wk), pl.ds(2 * KV_W + g * LANES, LANES)]
        vw = kv_ref[pl.ds(w0, wk), pl.ds(3 * KV_W + g * LANES, LANES)]
        dpos = qpos - (w0 + _iota((tq, wk), 1))
        o_w = _bias_softmax_pv(_dot_nt(qg, kw).reshape(HPG, tq, wk), (dpos >= 0) & (dpos < WINDOW), vw)

        for j in range(HPG):
            h = g * HPG + j
            rows = slice(j * tq, (j + 1) * tq)
            o = (o_c[rows] * gates[:, 3 * h:3 * h + 1] + o_s[j] * gates[:, 3 * h + 1:3 * h + 2]
                 + o_w[rows] * gates[:, 3 * h + 2:3 * h + 3])
            o_ref[:, h * LANES:(h + 1) * LANES] = o.astype(o_ref.dtype)


def _nsa_prompt(q, gate, kc, vc, kvb, batch, seq):
    nq = seq // Q_BLK
    nh = seq // CMP_STRIDE
    return pl.pallas_call(
        functools.partial(_nsa_prompt_body, seq=seq),
        grid=(batch, nq),
        in_specs=[pl.BlockSpec((Q_BLK, NSA_Q), lambda b, i: (b * nq + i, 0)),
                  pl.BlockSpec((Q_BLK, LANES), lambda b, i: (b * nq + i, 0)),
                  pl.BlockSpec((None, nh, KV_W), lambda b, i: (b, 0, 0)),
                  pl.BlockSpec((None, nh, KV_W), lambda b, i: (b, 0, 0)),
                  pl.BlockSpec((seq, 4 * KV_W), lambda b, i: (b, 0))],
        out_specs=pl.BlockSpec((Q_BLK, NSA_Q), lambda b, i: (b * nq + i, 0)),
        out_shape=jax.ShapeDtypeStruct((batch * seq, NSA_Q), BF16),
        compiler_params=_cparams("parallel", "parallel"),
        name="nsa_prompt",
    )(q, gate, kc, vc, kvb)


def _nsa_sample_one(q, gates, new_refs, kc_ref, vc_ref, kpages, vpages, kwbuf_ref, vwbuf_ref,
                    kwout_ref, vwout_ref, past):
    t = q.shape[0]
    wb = kwbuf_ref.shape[0] // NSA_GROUPS
    n_past = past // SEL_BLK
    n_sel = n_past + -(-t // SEL_BLK)
    n_cmp = (past + t) // CMP_STRIDE - CMP_LEN // CMP_STRIDE + 1
    qpos = past + _iota((t, 1), 0)
    pad = jnp.zeros((LANES - t, LANES), BF16)
    n_keys = past + LANES
    wk = wb + LANES
    heads = []
    for g in range(NSA_GROUPS):
        gsl = slice(g * LANES, (g + 1) * LANES)

        def new_rows(which):
            return new_refs[which][pl.ds(g, t, stride=NSA_GROUPS), :]

        qg = jnp.concatenate(
            [q[:, (g * HPG + j) * LANES:(g * HPG + j + 1) * LANES] for j in range(HPG)], axis=0).astype(BF16)
        o_c, psum = _compressed_branch(qg, kc_ref[:, gsl], vc_ref[:, gsl], qpos, n_cmp)
        sel = _select_lanes(psum, qpos, n_sel)

        def keys(pages, which):
            past_rows = [p[pl.ds(g, PAGE_SIZE, stride=NSA_GROUPS), :].astype(BF16) for p in pages]
            return jnp.concatenate(past_rows + [new_rows(which).astype(BF16), pad], axis=0)

        ks, vs = keys(kpages, 2), keys(vpages, 3)
        chosen = _dot(sel.astype(BF16), _block_to_key(0, n_keys))
        ok = (chosen > 0.5) & (_iota((t, n_keys), 1) <= qpos)
        o_s = _bias_softmax_pv(_dot_nt(qg, ks).reshape(HPG, t, n_keys), ok, vs)

        def window(buf_ref, which):
            return jnp.concatenate([buf_ref[pl.ds(g, wb, stride=NSA_GROUPS), :].astype(BF16),
                                    new_rows(which).astype(BF16), pad], axis=0)

        kw, vw = window(kwbuf_ref, 4), window(vwbuf_ref, 5)
        dpos = qpos - (past - wb + _iota((t, wk), 1))
        o_w = _bias_softmax_pv(_dot_nt(qg, kw).reshape(HPG, t, wk), (dpos >= 0) & (dpos < WINDOW), vw)

        for j in range(HPG):
            h = g * HPG + j
            rows = slice(j * t, (j + 1) * t)
            heads.append(o_c[rows] * gates[:, 3 * h:3 * h + 1] + o_s[rows] * gates[:, 3 * h + 1:3 * h + 2]
                         + o_w[rows] * gates[:, 3 * h + 2:3 * h + 3])

        kwout_ref[pl.ds((wb - t) * NSA_GROUPS + g, t, stride=NSA_GROUPS), :] = new_rows(4)
        vwout_ref[pl.ds((wb - t) * NSA_GROUPS + g, t, stride=NSA_GROUPS), :] = new_rows(5)
    keep = (wb - t) * NSA_GROUPS
    kwout_ref[pl.ds(0, keep), :] = kwbuf_ref[pl.ds(t * NSA_GROUPS, keep), :]
    vwout_ref[pl.ds(0, keep), :] = vwbuf_ref[pl.ds(t * NSA_GROUPS, keep), :]
    return heads


def _nsa_sample_body(pt_ref, q_ref, gate_ref, ksn_ref, vsn_ref, kwn_ref, vwn_ref, kc_ref, vc_ref, *refs,
                     n_pages, past, nb):
    del pt_ref
    kpages, vpages = refs[:nb * n_pages], refs[nb * n_pages:2 * nb * n_pages]
    kwbuf_ref, vwbuf_ref, o_ref, kwout_ref, vwout_ref = refs[2 * nb * n_pages:]
    t = q_ref.shape[0] // nb
    q = q_ref[...].astype(F32)
    gates = gate_ref[...]
    per_item = []
    for bb in range(nb):
        tok = slice(bb * t, (bb + 1) * t)
        tok2 = pl.ds(bb * NSA_GROUPS * t, NSA_GROUPS * t)
        new_refs = {2: ksn_ref.at[tok2], 3: vsn_ref.at[tok2], 4: kwn_ref.at[tok2], 5: vwn_ref.at[tok2]}
        pages = slice(bb * n_pages, (bb + 1) * n_pages)
        per_item.append(_nsa_sample_one(
            q[tok], gates[tok], new_refs, kc_ref.at[bb], vc_ref.at[bb], kpages[pages], vpages[pages],
            kwbuf_ref.at[bb], vwbuf_ref.at[bb], kwout_ref.at[bb], vwout_ref.at[bb], past))
    for h in range(NSA_HEADS):
        o_ref[:, h * LANES:(h + 1) * LANES] = jnp.concatenate(
            [heads[h] for heads in per_item], axis=0).astype(o_ref.dtype)


def _nsa_sample(q, gate, new, row_block0, kc, vc, pool_k, pool_v, kwbuf, vwbuf, pt_flat, dec_batch, t, n_pages,
                nb=2):
    past = n_pages * PAGE_SIZE
    nh = past // CMP_STRIDE
    wrows = kwbuf.shape[1]

    def rows(r, n):
        return pl.BlockSpec((nb * r, n), lambda b, pt: (row_block0 + b, 0))

    def per_b(r, c):
        return pl.BlockSpec((nb, r, c), lambda b, pt: (b, 0, 0))

    pages = [spec for bb in range(nb) for spec in _page_specs(n_pages, nb, bb)]
    grid_spec = pltpu.PrefetchScalarGridSpec(
        num_scalar_prefetch=1,
        grid=(dec_batch // nb,),
        in_specs=[rows(t, NSA_Q), rows(t, LANES)] + [rows(NSA_GROUPS * t, HEAD_DIM)] * 4
        + [per_b(nh, KV_W), per_b(nh, KV_W)] + pages + pages + [per_b(wrows, HEAD_DIM), per_b(wrows, HEAD_DIM)],
        out_specs=[pl.BlockSpec((nb * t, NSA_Q), lambda b, pt: (b, 0)), per_b(wrows, HEAD_DIM),
                   per_b(wrows, HEAD_DIM)],
    )
    wout = jax.ShapeDtypeStruct(kwbuf.shape, F32)
    pools = [pool_k] * (nb * n_pages) + [pool_v] * (nb * n_pages)
    return pl.pallas_call(
        functools.partial(_nsa_sample_body, n_pages=n_pages, past=past, nb=nb),
        grid_spec=grid_spec,
        out_shape=[jax.ShapeDtypeStruct((dec_batch * t, NSA_Q), BF16), wout, wout],
        compiler_params=_cparams("parallel"),
        name="nsa_sample",
    )(pt_flat, q, gate, *new, kc, vc, *pools, kwbuf, vwbuf)


def _log_decay(h):
    return math.log(1.0 - 2.0 ** (-5.0 - h))


def _ret_finish(o, h, gn_ref, rg):
    sl = slice(h * RET_DV, (h + 1) * RET_DV)
    return _unit_rms(o) * gn_ref[:, sl] * rg[:, sl]


def _ret_prompt_body(rq_ref, rk_ref, rv_ref, rg_ref, gn_ref, o_ref, s_ref):
    c = RET_CHUNK

    @pl.when(pl.program_id(1) == 0)
    def _():
        s_ref[...] = jnp.zeros(s_ref.shape, F32)

    n_col = _iota((c, 1), 0).astype(F32)
    diff = (_iota((c, c), 0) - _iota((c, c), 1)).astype(F32)
    rg = rg_ref[...].astype(F32)
    for h in range(RET_HEADS):
        lg = _log_decay(h)
        q = rq_ref[:, h * RET_DK:(h + 1) * RET_DK]
        k = rk_ref[:, h * RET_DK:(h + 1) * RET_DK]
        v = rv_ref[:, h * RET_DV:(h + 1) * RET_DV]
        s_prev = s_ref[h]
        dmask = jnp.where(diff >= 0.0, jnp.exp(jnp.maximum(diff, 0.0) * lg), 0.0)
        o = _dot((_dot_nt(q, k) * dmask).astype(BF16), v)
        q_dec = (q.astype(F32) * jnp.exp((n_col + 1.0) * lg)).astype(BF16)
        o = o + _dot(q_dec, s_prev.astype(BF16))
        k_dec = (k.astype(F32) * jnp.exp((c - 1.0 - n_col) * lg)).astype(BF16)
        s_ref[h] = math.exp(c * lg) * s_prev + _dot_tn(k_dec, v)
        o_ref[:, h * RET_DV:(h + 1) * RET_DV] = _ret_finish(o, h, gn_ref, rg).astype(o_ref.dtype)


def _ret_prompt(rot, rv, rg, gn, batch, seq):
    nch = seq // RET_CHUNK
    qk_w = RET_QK

    def rows(n, col):
        return pl.BlockSpec((RET_CHUNK, n), lambda b, i: (b * nch + i, col))

    return pl.pallas_call(
        _ret_prompt_body,
        grid=(batch, nch),
        in_specs=[rows(qk_w, 0), rows(qk_w, 1), rows(RET_V, 0), rows(RET_V, 0),
                  pl.BlockSpec((1, RET_V), lambda b, i: (0, 0))],
        out_specs=[rows(RET_V, 0),
                   pl.BlockSpec((None, RET_HEADS, RET_DK, RET_DV), lambda b, i: (b, 0, 0, 0))],
        out_shape=[jax.ShapeDtypeStruct((batch * seq, RET_V), BF16),
                   jax.ShapeDtypeStruct((batch, RET_HEADS, RET_DK, RET_DV), F32)],
        compiler_params=_cparams("parallel", "arbitrary"),
        name="retention_prompt",
    )(rot, rot, rv, rg, gn)


def _ret_sample_body(rq_ref, rk_ref, rv_ref, rg_ref, gn_ref, s_ref, o_ref, so_ref):
    c = rq_ref.shape[0]
    n_col = _iota((c, 1), 0).astype(F32)
    rg = rg_ref[...]
    zk = jnp.zeros((LANES - c, RET_DK), BF16)
    zv = jnp.zeros((LANES - c, RET_DV), BF16)
    for h in range(RET_HEADS):
        lg = _log_decay(h)
        q = rq_ref[:, h * RET_DK:(h + 1) * RET_DK]
        k = rk_ref[:, h * RET_DK:(h + 1) * RET_DK]
        v = rv_ref[:, h * RET_DV:(h + 1) * RET_DV]
        s_prev = s_ref[h]
        o = _dot((q * jnp.exp((n_col + 1.0) * lg)).astype(BF16), s_prev.astype(BF16))
        for j in range(c):
            qk = jnp.sum(q * k[j:j + 1, :], axis=-1, keepdims=True)
            dj = jnp.where(n_col >= j, jnp.exp(jnp.maximum(n_col - j, 0.0) * lg), 0.0)
            o = o + (qk * dj) * v[j:j + 1, :]
        k_dec = jnp.concatenate([(k * jnp.exp((c - 1.0 - n_col) * lg)).astype(BF16), zk], axis=0)
        v_pad = jnp.concatenate([v.astype(BF16), zv], axis=0)
        so_ref[h] = math.exp(c * lg) * s_prev + _dot_tn(k_dec, v_pad)
        o_ref[:, h * RET_DV:(h + 1) * RET_DV] = _ret_finish(o, h, gn_ref, rg)


def _ret_sample(rq, rk, rv, rg, gn, state, dec_batch, t):
    def rows(n):
        return pl.BlockSpec((t, n), lambda b: (b, 0))

    sspec = pl.BlockSpec((None, RET_HEADS, RET_DK, RET_DV), lambda b: (b, 0, 0, 0))
    return pl.pallas_call(
        _ret_sample_body,
        grid=(dec_batch,),
        in_specs=[rows(RET_QK), rows(RET_QK), rows(RET_V), rows(RET_V),
                  pl.BlockSpec((1, RET_V), lambda b: (0, 0)), sspec],
        out_specs=[rows(RET_V), sspec],
        out_shape=[jax.ShapeDtypeStruct((dec_batch * t, RET_V), F32),
                   jax.ShapeDtypeStruct(state.shape, F32)],
        compiler_params=_cparams("parallel"),
        name="retention_sample",
    )(rq, rk, rv, rg, gn, state)


def _merge_body(on_ref, or_ref, wn_ref, wr_ref, ga_ref, gr_ref, o_ref):
    a = _dot(on_ref[...], wn_ref[...])
    r = _dot(or_ref[...], wr_ref[...])
    o_ref[...] = (ga_ref[...].astype(F32) * a + gr_ref[...].astype(F32) * r).astype(o_ref.dtype)


def _merge(o_nsa, o_ret, wn, wr, gates, tn=1024):
    t = o_nsa.shape[0]
    d = wn.shape[1]
    ga0 = 0
    gr0 = d // tn
    return pl.pallas_call(
        _merge_body,
        grid=(d // tn, t // ROW_TILE),
        in_specs=[pl.BlockSpec((ROW_TILE, NSA_Q), lambda j, i: (i, 0)),
                  pl.BlockSpec((ROW_TILE, RET_V), lambda j, i: (i, 0)),
                  pl.BlockSpec((NSA_Q, tn), lambda j, i: (0, j)),
                  pl.BlockSpec((RET_V, tn), lambda j, i: (0, j)),
                  pl.BlockSpec((ROW_TILE, tn), lambda j, i: (i, ga0 + j)),
                  pl.BlockSpec((ROW_TILE, tn), lambda j, i: (i, gr0 + j))],
        out_specs=pl.BlockSpec((ROW_TILE, tn), lambda j, i: (i, j)),
        out_shape=jax.ShapeDtypeStruct((t, d), BF16),
        compiler_params=_cparams("parallel", "parallel"),
        name="merge",
    )(o_nsa, o_ret, wn, wr, gates, gates)


def _out_router_body(xp_ref, xs_ref, mix_ref, wo_ref, g_ref, rwh_ref, rwl_ref, rb_ref,
                     x1_ref, h_ref, route_ref, cnt_ref, *, tiles_p):
    @pl.when(pl.program_id(0) == 0)
    def _():
        cnt_ref[...] = jnp.zeros(cnt_ref.shape, F32)

    x = jnp.where(pl.program_id(0) < tiles_p, xp_ref[...], xs_ref[...])
    x1 = x + _dot(mix_ref[...], wo_ref[...])
    x1_ref[...] = x1
    h = _unit_rms(x1) * g_ref[...]
    tm = h.shape[0]
    for cc in range(h.shape[1] // LANES):
        h_ref[_flat_idx(0, tm, cc)] = h[:, cc * LANES:(cc + 1) * LANES]
    for cc in range(h.shape[1] // LANES, ROW_PITCH):
        h_ref[_flat_idx(0, tm, cc)] = jnp.zeros((tm, LANES), F32)
    tm = h.shape[0]
    lane = _iota((tm, LANES), 1)
    hi, lo = _split_bf16(h)
    logits = _dot(hi, rwh_ref[...]) + _dot(lo, rwh_ref[...]) + _dot(hi, rwl_ref[...]) + rb_ref[...]
    work = jnp.where(lane < N_EXPERTS, logits, -jnp.inf)
    vals, idxs = [], []
    for _ in range(TOP_K):
        v = jnp.max(work, axis=-1, keepdims=True)
        ix = jnp.min(jnp.where(work == v, lane, LANES), axis=-1, keepdims=True)
        vals.append(v)
        idxs.append(ix)
        work = jnp.where(lane == ix, -jnp.inf, work)
    es = [jnp.exp(v - vals[0]) for v in vals]
    den = es[0] + es[1] + es[2] + es[3]
    hot = jnp.zeros((tm, LANES), F32)
    for ix in idxs:
        hot = hot + jnp.where(lane == ix, 1.0, 0.0)
    before = jnp.where(_iota((tm, tm), 1) < _iota((tm, tm), 0), 1.0, 0.0).astype(BF16)
    ranks = _dot(before, hot.astype(BF16)) + cnt_ref[...]
    route = jnp.zeros((tm, LANES), F32)
    for k in range(TOP_K):
        rk = jnp.sum(jnp.where(lane == idxs[k], ranks, 0.0), axis=-1, keepdims=True)
        route = route + jnp.where(lane == k, idxs[k].astype(F32), 0.0)
        route = route + jnp.where(lane == TOP_K + k, es[k] / den, 0.0)
        route = route + jnp.where(lane == 2 * TOP_K + k, rk, 0.0)
    route_ref[...] = route
    cnt_ref[...] = cnt_ref[...] + jnp.sum(hot, axis=0, keepdims=True)


def _out_router(xp, xs, mixed, wo, g2, rwh, rwl, rb, tm=256):
    d = xp.shape[1]
    t = xp.shape[0] + xs.shape[0]
    tiles_p = xp.shape[0] // tm
    rows = pl.BlockSpec((tm, d), lambda i: (i, 0))
    lanes = pl.BlockSpec((tm, LANES), lambda i: (i, 0))

    def const(r, c):
        return pl.BlockSpec((r, c), lambda i: (0, 0))

    return pl.pallas_call(
        functools.partial(_out_router_body, tiles_p=tiles_p),
        grid=(t // tm,),
        in_specs=_two_group_specs(tm, d, tiles_p)
        + [rows, const(d, d), const(1, d), const(d, LANES), const(d, LANES), const(1, LANES)],
        out_specs=[rows, pl.BlockSpec((tm * ROW_PITCH, LANES), lambda i: (i, 0)), lanes, const(1, LANES)],
        out_shape=[jax.ShapeDtypeStruct((t, d), F32), jax.ShapeDtypeStruct((t * ROW_PITCH, LANES), F32),
                   jax.ShapeDtypeStruct((t, LANES), F32), jax.ShapeDtypeStruct((1, LANES), F32)],
        compiler_params=_cparams("arbitrary"),
        name="out_router",
    )(xp, xs, mixed, wo, g2, rwh, rwl, rb)


def _row_copy(src, r_src, dst, r_dst, sem):
    return pltpu.make_async_copy(src.at[pl.ds(r_src * ROW_PITCH, ROW_PITCH), :],
                                 dst.at[pl.ds(r_dst * ROW_PITCH, ROW_PITCH), :], sem)


def _flat_idx(row0, n, c):
    return (pl.ds(row0 * ROW_PITCH + c, n, stride=ROW_PITCH), slice(None))


def _dispatch_body(start_ref, cnt_ref, pend_ref, e_ref, rank_ref, h_ref, xs_ref, zero_ref, sem, zsem):
    tm = h_ref.shape[0] // ROW_PITCH

    def slot(r, k):
        return start_ref[e_ref[r * TOP_K + k]] + rank_ref[r * TOP_K + k]

    def issue(r, _):
        for k in range(TOP_K):
            _row_copy(h_ref, r, xs_ref, slot(r, k), sem).start(priority=k % 2)
        return 0

    def drain(r, _):
        for k in range(TOP_K):
            _row_copy(h_ref, r, xs_ref, slot(r, k), sem).wait()
        return 0

    lax.fori_loop(0, tm, issue, 0)

    @pl.when(pl.program_id(0) == 0)
    def _():
        zero_ref[...] = jnp.zeros(zero_ref.shape, F32)
        zrows = zero_ref.shape[0] // ROW_PITCH
        for phase in ("start", "wait"):
            def per_expert(e, _):
                def per_row(s, _):
                    cp = _row_copy(zero_ref, 0, xs_ref, s, zsem)
                    cp.start() if phase == "start" else cp.wait()
                    return 0
                return lax.fori_loop(start_ref[e] + cnt_ref[e], pend_ref[e], per_row, 0)
            lax.fori_loop(0, N_EXPERTS, per_expert, 0)

            def per_chunk(s, _):
                s0 = pl.multiple_of(s * (zrows * ROW_PITCH), zrows * ROW_PITCH)
                cp = pltpu.make_async_copy(zero_ref, xs_ref.at[pl.ds(s0, zrows * ROW_PITCH), :], zsem)
                cp.start() if phase == "start" else cp.wait()
                return 0
            lax.fori_loop(pend_ref[N_EXPERTS - 1] // zrows, xs_ref.shape[0] // (zrows * ROW_PITCH), per_chunk, 0)

    lax.fori_loop(0, tm, drain, 0)


def _dispatch(h_flat, e_flat, rank_flat, pad_start, counts, pad_end, n_slots, tm=256):
    t = h_flat.shape[0] // ROW_PITCH
    smem = pl.BlockSpec((tm * TOP_K,), lambda i, *_: (i,), memory_space=pltpu.SMEM)
    grid_spec = pltpu.PrefetchScalarGridSpec(
        num_scalar_prefetch=3,
        grid=(t // tm,),
        in_specs=[smem, smem, pl.BlockSpec((tm * ROW_PITCH, LANES), lambda i, *_: (i, 0))],
        out_specs=pl.BlockSpec(memory_space=pl.ANY),
        scratch_shapes=[pltpu.VMEM((MOE_TILE // 8 * ROW_PITCH, LANES), F32), pltpu.SemaphoreType.DMA(()),
                        pltpu.SemaphoreType.DMA(())],
    )
    return pl.pallas_call(
        _dispatch_body,
        grid_spec=grid_spec,
        out_shape=jax.ShapeDtypeStruct((n_slots * ROW_PITCH, LANES), F32),
        compiler_params=pltpu.CompilerParams(dimension_semantics=("arbitrary",), vmem_limit_bytes=VMEM_LIMIT,
                                             has_side_effects=True),
        name="moe_dispatch",
    )(pad_start, counts, pad_end, e_flat, rank_flat, h_flat)


def _experts_body(te_ref, nu_ref, rows_ref, xs_ref, wg_ref, wu_ref, bg_ref, bu_ref, wd_ref, bd_ref, y_ref,
                  xb_ref, wgb_ref, wub_ref, wdb_ref):
    del te_ref, nu_ref
    i = pl.program_id(0)
    c = pl.program_id(1)
    n_rows = rows_ref[i]

    n_chunks = xb_ref.shape[1] // LANES

    @pl.when((n_rows > 0) & (c == 0))
    def _():
        for cc in range(n_chunks):
            xb_ref[:, cc * LANES:(cc + 1) * LANES] = xs_ref[_flat_idx(0, MOE_TILE, cc)].astype(BF16)

    @pl.when(n_rows > 0)
    def _():
        wgb_ref[...] = wg_ref[...].astype(BF16)
        wub_ref[...] = wu_ref[...].astype(BF16)
        wdb_ref[...] = wd_ref[...].astype(BF16)

    for sb in range(MOE_TILE // MOE_SUB):
        @pl.when(n_rows > sb * MOE_SUB)
        def _():
            x = xb_ref[pl.ds(sb * MOE_SUB, MOE_SUB), :]
            g = jnp.minimum(_dot(x, wgb_ref[...]) + bg_ref[...], SWIGLU_LIMIT)
            u = jnp.clip(_dot(x, wub_ref[...]) + bu_ref[...], -SWIGLU_LIMIT, SWIGLU_LIMIT)
            a = (u + 1.0) * g * _sigmoid(g * SWIGLU_ALPHA)
            part = _dot(a.astype(BF16), wdb_ref[...])

            @pl.when(c == 0)
            def _():
                first = part + bd_ref[...]
                for cc in range(n_chunks):
                    y_ref[_flat_idx(sb * MOE_SUB, MOE_SUB, cc)] = first[:, cc * LANES:(cc + 1) * LANES]
                for cc in range(n_chunks, ROW_PITCH):
                    y_ref[_flat_idx(sb * MOE_SUB, MOE_SUB, cc)] = jnp.zeros((MOE_SUB, LANES), F32)

            @pl.when(c > 0)
            def _():
                for cc in range(n_chunks):
                    idx = _flat_idx(sb * MOE_SUB, MOE_SUB, cc)
                    y_ref[idx] = y_ref[idx] + part[:, cc * LANES:(cc + 1) * LANES]

        @pl.when((n_rows <= sb * MOE_SUB) & (c == 0))
        def _():
            y_ref[pl.ds(sb * MOE_SUB * ROW_PITCH, MOE_SUB * ROW_PITCH), :] = jnp.zeros(
                (MOE_SUB * ROW_PITCH, LANES), F32)


def _experts(xs, tile_e, n_used, tile_rows, w_gu, b_gu, w_dn, b_dn):
    n_slots = xs.shape[0] // ROW_PITCH
    d = w_dn.shape[2]
    n_tiles = n_slots // MOE_TILE
    flat_tile = (MOE_TILE * ROW_PITCH, LANES)
    n_ff = D_FF // MOE_FF_TILE
    tf = MOE_FF_TILE

    def last_used(i, nu):
        return jnp.minimum(i, nu[0] - 1)

    def chunk(i, c, nu):
        return jnp.where(i < nu[0], c, n_ff - 1)

    grid_spec = pltpu.PrefetchScalarGridSpec(
        num_scalar_prefetch=3,
        grid=(n_tiles, n_ff),
        in_specs=[
            pl.BlockSpec(flat_tile, lambda i, c, te, nu, nr: (last_used(i, nu), 0)),
            pl.BlockSpec((None, d, tf), lambda i, c, te, nu, nr: (te[i], 0, chunk(i, c, nu))),
            pl.BlockSpec((None, d, tf), lambda i, c, te, nu, nr: (te[i], 0, n_ff + chunk(i, c, nu))),
            pl.BlockSpec((None, 1, tf), lambda i, c, te, nu, nr: (te[i], 0, chunk(i, c, nu))),
            pl.BlockSpec((None, 1, tf), lambda i, c, te, nu, nr: (te[i], 0, n_ff + chunk(i, c, nu))),
            pl.BlockSpec((None, tf, d), lambda i, c, te, nu, nr: (te[i], chunk(i, c, nu), 0)),
            pl.BlockSpec((None, 1, d), lambda i, c, te, nu, nr: (te[i], 0, 0)),
        ],
        out_specs=pl.BlockSpec(flat_tile, lambda i, c, te, nu, nr: (i, 0)),
        scratch_shapes=[pltpu.VMEM((MOE_TILE, d), BF16), pltpu.VMEM((d, tf), BF16), pltpu.VMEM((d, tf), BF16),
                        pltpu.VMEM((tf, d), BF16)],
    )
    return pl.pallas_call(
        _experts_body,
        grid_spec=grid_spec,
        out_shape=jax.ShapeDtypeStruct((n_slots * ROW_PITCH, LANES), F32),
        compiler_params=_cparams("arbitrary", "arbitrary"),
        name="moe_experts",
    )(tile_e, n_used, tile_rows, xs, w_gu, w_gu, b_gu, b_gu, w_dn, b_dn)


def _combine_body(start_ref, e_ref, rank_ref, x1_ref, route_ref, y_ref, op_ref, os_ref, buf_ref, sem, *, tiles_p):
    tm = x1_ref.shape[0]

    def slot(r, k):
        return start_ref[e_ref[r * TOP_K + k]] + rank_ref[r * TOP_K + k]

    def issue(r, _):
        for k in range(TOP_K):
            _row_copy(y_ref, slot(r, k), buf_ref.at[k], r, sem).start(priority=k % 2)
        return 0

    def drain(r, _):
        for k in range(TOP_K):
            _row_copy(y_ref, slot(r, k), buf_ref.at[k], r, sem).wait()
        return 0

    lax.fori_loop(0, tm, issue, 0)
    lax.fori_loop(0, tm, drain, 0)
    route = route_ref[...]
    gate = [route[:, TOP_K + k:TOP_K + k + 1] for k in range(TOP_K)]

    def write(o_ref):
        for cc in range(o_ref.shape[1] // LANES):
            sl = slice(cc * LANES, (cc + 1) * LANES)
            out = x1_ref[:, sl]
            for k in range(TOP_K):
                out = out + gate[k] * buf_ref.at[k][_flat_idx(0, tm, cc)]
            o_ref[:, sl] = out

    pl.when(pl.program_id(0) < tiles_p)(lambda: write(op_ref))
    pl.when(pl.program_id(0) >= tiles_p)(lambda: write(os_ref))


def _combine(x1, route, y, e_flat, rank_flat, pad_start, n_prompt, tm=256):
    t, d = x1.shape
    tiles_p = n_prompt // tm
    smem = pl.BlockSpec((tm * TOP_K,), lambda i, *_: (i,), memory_space=pltpu.SMEM)
    grid_spec = pltpu.PrefetchScalarGridSpec(
        num_scalar_prefetch=1,
        grid=(t // tm,),
        in_specs=[smem, smem, pl.BlockSpec((tm, d), lambda i, *_: (i, 0)),
                  pl.BlockSpec((tm, LANES), lambda i, *_: (i, 0)),
                  pl.BlockSpec(memory_space=pl.ANY)],
        out_specs=[pl.BlockSpec((tm, d), lambda i, *_: (jnp.minimum(i, tiles_p - 1), 0)),
                   pl.BlockSpec((tm, d), lambda i, *_: (jnp.maximum(i - tiles_p, 0), 0))],
        scratch_shapes=[pltpu.VMEM((TOP_K, tm * ROW_PITCH, LANES), F32), pltpu.SemaphoreType.DMA(())],
    )
    return pl.pallas_call(
        functools.partial(_combine_body, tiles_p=tiles_p),
        grid_spec=grid_spec,
        out_shape=[jax.ShapeDtypeStruct((n_prompt, d), F32), jax.ShapeDtypeStruct((t - n_prompt, d), F32)],
        compiler_params=_cparams("arbitrary"),
        name="moe_combine",
    )(pad_start, e_flat, rank_flat, x1, route, y)


def _moe(x1, h, route, counts, w_gu, b_gu, w_dn, b_dn, n_prompt):
    t = x1.shape[0]
    n_tiles = -(-t * TOP_K // MOE_TILE) + N_EXPERTS
    n_slots = n_tiles * MOE_TILE
    cnt = counts[0, :N_EXPERTS].astype(jnp.int32)
    padded = (cnt + MOE_TILE - 1) // MOE_TILE * MOE_TILE
    pad_end = jnp.cumsum(padded)
    pad_start = pad_end - padded
    n_used = (pad_end[-1] // MOE_TILE).reshape(1)
    tile_first = jnp.minimum(jnp.arange(n_tiles, dtype=jnp.int32), n_used[0] - 1) * MOE_TILE
    tile_e = jnp.minimum(jnp.searchsorted(pad_end, tile_first, side="right"), N_EXPERTS - 1).astype(jnp.int32)
    tile_idx = jnp.arange(n_tiles, dtype=jnp.int32)
    tile_rows = jnp.clip(pad_start[tile_e] + cnt[tile_e] - tile_idx * MOE_TILE, 0, MOE_TILE)
    tile_rows = jnp.where(tile_idx < n_used[0], tile_rows, 0).astype(jnp.int32)
    e_flat = route[:, :TOP_K].astype(jnp.int32).reshape(-1)
    rank_flat = route[:, 2 * TOP_K:3 * TOP_K].astype(jnp.int32).reshape(-1)
    xs = _dispatch(h, e_flat, rank_flat, pad_start, cnt, pad_end, n_slots)
    y = _experts(xs, tile_e, n_used, tile_rows, w_gu, b_gu.reshape(N_EXPERTS, 1, -1), w_dn,
                 b_dn.reshape(N_EXPERTS, 1, -1))
    return _combine(x1, route, y, e_flat, rank_flat, pad_start, n_prompt)


def _layer(xp, xs, pools, kwbuf, vwbuf, s_ret, page_table, p):
    batch, seq, d = xp.shape
    dec_batch, t_new, _ = xs.shape
    n_pages = page_table.shape[1]
    past = n_pages * PAGE_SIZE
    n_prompt = batch * seq
    n_sample = dec_batch * t_new
    assert seq & (seq - 1) == 0 and t_new & (t_new - 1) == 0 and seq % SEL_KV_TILE == 0
    assert n_prompt % ROW_TILE == 0 and n_sample % ROW_TILE == 0 and t_new % 8 == 0

    xp2, xs2 = xp.reshape(n_prompt, d), xs.reshape(n_sample, d)
    t_all = n_prompt + n_sample
    xn = _ln1(xp2, xs2, p["ln1_g"])

    w_in = p["w_in"]
    offs = [0]
    for n in (NSA_Q, KV_W, KV_W, KV_W, KV_W, KV_W, KV_W, 3 * NSA_HEADS, RET_QK, RET_QK, RET_V, RET_V, D_MODEL, D_MODEL):
        offs.append(offs[-1] + n)
    w_q = w_in[:, offs[0]:offs[1]].astype(BF16)
    w_kv = jnp.concatenate([w_in[:, offs[1]:offs[8]],
                            jnp.zeros((d, LANES - 3 * NSA_HEADS), F32)], axis=1).astype(BF16)
    w_rot = w_in[:, offs[8]:offs[10]].astype(BF16)
    w_act = w_in[:, offs[10]:offs[14]].astype(BF16)

    def rows(n):
        return pl.BlockSpec((ROW_TILE, n), lambda j, i: (i, 0))

    q = _proj_call(_proj_q_body, xn, w_q, [p["q_norm_g"].reshape(1, HEAD_DIM)], [_vec_spec(HEAD_DIM)],
                   jax.ShapeDtypeStruct((t_all, NSA_Q), BF16), rows(NSA_Q), NSA_Q, "proj_q")
    row2 = jax.ShapeDtypeStruct((NSA_GROUPS * t_all, HEAD_DIM), F32)
    row2_spec = pl.BlockSpec((NSA_GROUPS * ROW_TILE, HEAD_DIM), lambda j, i: (i, 0))
    *kv_rows, kvb, gate = _proj_call(
        _proj_kv_body, xn, w_kv,
        [p["k_sel_norm_g"].reshape(1, HEAD_DIM), p["k_win_norm_g"].reshape(1, HEAD_DIM)],
        [_vec_spec(HEAD_DIM), _vec_spec(HEAD_DIM)],
        [row2] * 6 + [jax.ShapeDtypeStruct((t_all, 4 * KV_W), BF16), jax.ShapeDtypeStruct((t_all, LANES), F32)],
        [row2_spec] * 6 + [rows(4 * KV_W), rows(LANES)], 6 * KV_W + LANES, "proj_kv")
    inv = 1.0 / (ROPE_BASE ** jnp.linspace(0.0, 1.0, RET_DK // 2, dtype=F32))
    rot = _proj_call(
        functools.partial(_proj_rot_body, n_prompt=n_prompt, seq=seq, past=past, dec_seq=t_new),
        xn, w_rot, [jnp.repeat(inv, 2).reshape(1, RET_DK)], [_vec_spec(RET_DK)],
        jax.ShapeDtypeStruct((t_all, 2 * RET_QK), BF16),
        pl.BlockSpec((ROW_TILE, RET_QK), lambda j, i: (i, j)), RET_QK, "proj_rot")
    act_tn = 1024

    def proj_act(w, act, name):
        return _proj_call(functools.partial(_proj_act_body, act=act), xn, w, [], [],
                          jax.ShapeDtypeStruct((t_all, w.shape[1]), BF16),
                          pl.BlockSpec((ROW_TILE, act_tn), lambda j, i: (i, j)), act_tn, name)

    rv = proj_act(w_act[:, :RET_V], "none", "proj_rv")
    rg = proj_act(w_act[:, RET_V:2 * RET_V], "silu", "proj_rg")
    merge_gates = proj_act(w_act[:, 2 * RET_V:], "sigmoid", "proj_gates")

    cmp_w = (_compress_weights(p["cmp_k_w1"], p["cmp_k_b1"], p["cmp_k_w2"])
             + _compress_weights(p["cmp_v_w1"], p["cmp_v_b1"], p["cmp_v_w2"])
             + (p["k_cmp_norm_g"].reshape(1, HEAD_DIM),))
    gn = p["ret_norm_g"].reshape(1, RET_V)

    kc_p, vc_p = _compress_prompt(kv_rows[0], kv_rows[1], cmp_w, batch, seq)
    o_nsa_p = _nsa_prompt(q, gate, kc_p, vc_p, kvb, batch, seq)
    o_ret_p, ret_p = _ret_prompt(rot, rv, rg, gn, batch, seq)

    pool_kc, pool_vc, pool_ks, pool_vs = [
        a.reshape(a.shape[0], NSA_GROUPS * PAGE_SIZE, HEAD_DIM) for a in pools]
    pt_flat = page_table.reshape(-1)
    kc_s, vc_s = _compress_sample(pool_kc, pool_vc, pt_flat, cmp_w, dec_batch, n_pages)
    wb = kwbuf.shape[1]
    o_nsa_s, kw_s, vw_s = _nsa_sample(
        q, gate, kv_rows[2:], n_prompt // (2 * t_new), kc_s, vc_s, pool_ks, pool_vs,
        kwbuf.reshape(dec_batch, wb * NSA_GROUPS, HEAD_DIM), vwbuf.reshape(dec_batch, wb * NSA_GROUPS, HEAD_DIM),
        pt_flat, dec_batch, t_new, n_pages)
    rot_s = rot[n_prompt:].astype(F32)
    o_ret_s, ret_s = _ret_sample(rot_s[:, :RET_QK], rot_s[:, RET_QK:], rv[n_prompt:].astype(F32),
                                 rg[n_prompt:].astype(F32),
                                 gn, s_ret, dec_batch, t_new)

    o_nsa = jnp.concatenate([o_nsa_p, o_nsa_s], axis=0)
    o_ret = jnp.concatenate([o_ret_p, o_ret_s.astype(BF16)], axis=0)
    mixed = _merge(o_nsa, o_ret, p["w_nsa_br"].astype(BF16), p["w_ret_br"].astype(BF16), merge_gates)
    rw = jnp.concatenate([p["router_w"], jnp.zeros((d, LANES - N_EXPERTS), F32)], axis=1)
    rwh, rwl = _split_bf16(rw)
    rb = jnp.concatenate([p["router_b"], jnp.zeros((LANES - N_EXPERTS,), F32)]).reshape(1, LANES)
    x1, h, route, counts = _out_router(xp2, xs2, mixed, p["w_out"].astype(BF16), p["ln2_g"].reshape(1, d),
                                       rwh, rwl, rb)
    y_p, y_s = _moe(x1, h, route, counts, p["w_gate_up"], p["b_gate_up"], p["w_down"], p["b_down"], n_prompt)

    kv5 = (NSA_GROUPS, HEAD_DIM)
    wbp = min(WINDOW, seq)
    rows_p = [a[:NSA_GROUPS * n_prompt].reshape(batch, seq, *kv5) for a in kv_rows]
    rows_s = [a[NSA_GROUPS * n_prompt:].reshape(dec_batch, t_new, *kv5) for a in kv_rows[:4]]
    states_p = (*rows_p[:4], rows_p[4][:, seq - wbp:], rows_p[5][:, seq - wbp:], ret_p)
    states_s = (*rows_s, kw_s.reshape(dec_batch, wb, *kv5), vw_s.reshape(dec_batch, wb, *kv5), ret_s)
    return y_p.reshape(batch, seq, d), y_s.reshape(dec_batch, t_new, d), states_p, states_s


def kernel(x_prompt, x_sample, cache_k_cmp, cache_v_cmp, cache_k_sel, cache_v_sel, state_k_win, state_v_win,
           state_ret, page_table, ln1_g, w_in, cmp_k_w1, cmp_k_b1, cmp_k_w2, cmp_v_w1, cmp_v_b1, cmp_v_w2,
           q_norm_g, k_cmp_norm_g, k_sel_norm_g, k_win_norm_g, ret_norm_g, w_nsa_br, w_ret_br, w_out, ln2_g,
           router_w, router_b, w_gate_up, b_gate_up, w_down, b_down):
    depth = w_in.shape[0]
    y_p, y_s = x_prompt, x_sample
    new_p, new_s = [], []
    for l in range(depth):
        p = {
            "ln1_g": ln1_g[l], "w_in": w_in[l],
            "cmp_k_w1": cmp_k_w1[l], "cmp_k_b1": cmp_k_b1[l], "cmp_k_w2": cmp_k_w2[l],
            "cmp_v_w1": cmp_v_w1[l], "cmp_v_b1": cmp_v_b1[l], "cmp_v_w2": cmp_v_w2[l],
            "q_norm_g": q_norm_g[l], "k_cmp_norm_g": k_cmp_norm_g[l], "k_sel_norm_g": k_sel_norm_g[l],
            "k_win_norm_g": k_win_norm_g[l], "ret_norm_g": ret_norm_g[l],
            "w_nsa_br": w_nsa_br[l], "w_ret_br": w_ret_br[l], "w_out": w_out[l], "ln2_g": ln2_g[l],
            "router_w": router_w[l], "router_b": router_b[l], "w_gate_up": w_gate_up[l],
            "b_gate_up": b_gate_up[l], "w_down": w_down[l], "b_down": b_down[l],
        }
        pools = (cache_k_cmp[l], cache_v_cmp[l], cache_k_sel[l], cache_v_sel[l])
        y_p, y_s, sp, ss = _layer(y_p, y_s, pools, state_k_win[l], state_v_win[l], state_ret[l], page_table, p)
        new_p.append(sp)
        new_s.append(ss)
    outs_p = [jnp.stack(a) for a in zip(*new_p)]
    outs_s = [jnp.stack(a) for a in zip(*new_s)]
    return (y_p, y_s, *outs_p, *outs_s)
```

```python
import functools
import math

import jax
import jax.numpy as jnp
from jax import lax
from jax.experimental import pallas as pl
from jax.experimental.pallas import tpu as pltpu

F32 = jnp.float32
BF16 = jnp.bfloat16

D_MODEL = 2048
PAGE_SIZE = 128
NSA_HEADS = 8
NSA_GROUPS = 2
HPG = NSA_HEADS // NSA_GROUPS
HEAD_DIM = 128
CMP_LEN = 32
CMP_STRIDE = 16
CMP_HIDDEN = 2 * HEAD_DIM
SEL_BLK = 64
SEL_SHIFT = 6
SEL_TOPK = 16
WINDOW = 512
Q_BLK = 128
RET_HEADS = 8
RET_DK = 128
RET_DV = 256
RET_CHUNK = 128
ROPE_BASE = 10000.0
N_EXPERTS = 32
TOP_K = 4
D_FF = D_MODEL
SWIGLU_LIMIT = 7.0
SWIGLU_ALPHA = 1.702
EPS = 1e-6
NEG = -1e30
BIG = 1e30

NSA_Q = NSA_HEADS * HEAD_DIM
KV_W = NSA_GROUPS * HEAD_DIM
RET_QK = RET_HEADS * RET_DK
RET_V = RET_HEADS * RET_DV

LANES = 128
ROW_TILE = 512
SEL_KV_TILE = 512
MOE_TILE = 512
STAGE_PITCH = 24
ROW_PITCH = 17
MOE_FF_TILE = 512
MOE_DOWN_TILE = 1024
VMEM_LIMIT = 56 * 1024 * 1024


def _cparams(*sem):
    return pltpu.CompilerParams(dimension_semantics=sem, vmem_limit_bytes=VMEM_LIMIT)


def _dot(a, b):
    return jnp.dot(a, b, preferred_element_type=F32)


def _dot_nt(a, b):
    return lax.dot_general(a, b, (((1,), (1,)), ((), ())), preferred_element_type=F32)


def _dot_tn(a, b):
    return lax.dot_general(a, b, (((0,), (0,)), ((), ())), preferred_element_type=F32)


def _sigmoid(x):
    return 1.0 / (1.0 + jnp.exp(-x))


def _unit_rms(x):
    return x * lax.rsqrt(jnp.mean(x * x, axis=-1, keepdims=True) + EPS)


def _iota(shape, dim):
    return lax.broadcasted_iota(jnp.int32, shape, dim)


def _split_bf16(x):
    hi = x.astype(BF16)
    lo = (x - hi.astype(F32)).astype(BF16)
    return hi, lo


def _two_group_specs(tm, d, tiles_p):
    return [pl.BlockSpec((tm, d), lambda i: (jnp.minimum(i, tiles_p - 1), 0)),
            pl.BlockSpec((tm, d), lambda i: (jnp.maximum(i - tiles_p, 0), 0))]


def _ln1_body(xp_ref, xs_ref, g_ref, o_ref, *, tiles_p):
    def norm(x_ref):
        o_ref[...] = (_unit_rms(x_ref[...]) * g_ref[...]).astype(o_ref.dtype)

    pl.when(pl.program_id(0) < tiles_p)(lambda: norm(xp_ref))
    pl.when(pl.program_id(0) >= tiles_p)(lambda: norm(xs_ref))


def _ln1(xp, xs, g):
    d = xp.shape[1]
    tiles_p = xp.shape[0] // ROW_TILE
    t = xp.shape[0] + xs.shape[0]
    return pl.pallas_call(
        functools.partial(_ln1_body, tiles_p=tiles_p),
        grid=(t // ROW_TILE,),
        in_specs=_two_group_specs(ROW_TILE, d, tiles_p) + [pl.BlockSpec((1, d), lambda i: (0, 0))],
        out_specs=pl.BlockSpec((ROW_TILE, d), lambda i: (i, 0)),
        out_shape=jax.ShapeDtypeStruct((t, d), BF16),
        compiler_params=_cparams("arbitrary"),
        name="ln1",
    )(xp, xs, g.reshape(1, d))


def _proj_q_body(x_ref, w_ref, g_ref, o_ref):
    y = _dot(x_ref[...], w_ref[...])
    g = g_ref[...] * (HEAD_DIM ** -0.5)
    for c in range(NSA_HEADS):
        sl = slice(c * LANES, (c + 1) * LANES)
        o_ref[:, sl] = (_unit_rms(y[:, sl]) * g).astype(o_ref.dtype)


def _proj_kv_body(x_ref, w_ref, gs_ref, gw_ref, kc_ref, vc_ref, ks_ref, vs_ref, kw_ref, vw_ref, kvb_ref, gate_ref):
    y = _dot(x_ref[...], w_ref[...])
    tm = y.shape[0]
    outs = (kc_ref, vc_ref, ks_ref, vs_ref, kw_ref, vw_ref)
    for c in range(12):
        sl = slice(c * LANES, (c + 1) * LANES)
        yc = y[:, sl]
        if c in (4, 5):
            yc = _unit_rms(yc) * gs_ref[...]
        elif c in (8, 9):
            yc = _unit_rms(yc) * gw_ref[...]
        outs[c // NSA_GROUPS][pl.ds(c % NSA_GROUPS, tm, stride=NSA_GROUPS), :] = yc
        if c >= 4:
            kvb_ref[:, (c - 4) * LANES:(c - 3) * LANES] = yc.astype(BF16)
    gate_ref[...] = _sigmoid(y[:, 12 * LANES:13 * LANES])


def _proj_rot_body(x_ref, w_ref, inv_ref, o_ref, *, n_prompt, seq, past, dec_seq):
    j = pl.program_id(0)
    i = pl.program_id(1)
    y = _dot(x_ref[...], w_ref[...])
    tm = y.shape[0]
    row = i * tm + _iota((tm, LANES), 0)
    pos = jnp.where(row < n_prompt, row & (seq - 1), past + ((row - n_prompt) & (dec_seq - 1)))
    ang = pos.astype(F32) * inv_ref[...]
    cos = jnp.cos(ang)
    sin = jnp.sin(ang)
    even = (_iota((tm, LANES), 1) & 1) == 0
    sin = jnp.where(even, -sin, sin)
    scale = jnp.where(j == 0, 1.0, RET_DK ** -0.5).astype(F32)
    for c in range(RET_HEADS):
        sl = slice(c * LANES, (c + 1) * LANES)
        yc = y[:, sl]
        partner = jnp.where(even, pltpu.roll(yc, LANES - 1, 1), pltpu.roll(yc, 1, 1))
        o_ref[:, sl] = ((yc * cos + partner * sin) * scale).astype(o_ref.dtype)


def _proj_act_body(x_ref, w_ref, o_ref, *, act):
    y = _dot(x_ref[...], w_ref[...])
    if act == "silu":
        y = y * _sigmoid(y)
    elif act == "sigmoid":
        y = _sigmoid(y)
    o_ref[...] = y.astype(o_ref.dtype)


def _proj_call(body, xn, w, extra, extra_specs, out_shape, out_specs, tn, name):
    t, k = xn.shape
    n = w.shape[1]
    in_specs = [pl.BlockSpec((ROW_TILE, k), lambda j, i: (i, 0)),
                pl.BlockSpec((k, tn), lambda j, i: (0, j))] + extra_specs
    return pl.pallas_call(
        body,
        grid=(n // tn, t // ROW_TILE),
        in_specs=in_specs,
        out_specs=out_specs,
        out_shape=out_shape,
        compiler_params=_cparams("parallel", "parallel"),
        name=name,
    )(xn, w, *extra)


def _vec_spec(n):
    return pl.BlockSpec((1, n), lambda j, i: (0, 0))


def _compress_mlp(x_cat, w1_ref, b1_ref, w2_ref):
    nh = x_cat.shape[0]
    a = _dot(x_cat, w1_ref[...])
    pre = a[:, :CMP_HIDDEN] + pltpu.roll(a[:, CMP_HIDDEN:], nh - 1, 0) + b1_ref[...]
    hid = pre * (0.5 * (1.0 + jnp.tanh(math.sqrt(2.0 / math.pi) * (pre + 0.044715 * (pre * pre * pre)))))
    out = _dot(hid.astype(BF16), w2_ref[...])
    return jnp.where(_iota(out.shape, 0) < nh - 1, out, 0.0)


def _compress_finish(xk, xv, wrefs, kc_ref, vc_ref):
    w1k, b1k, w2k, w1v, b1v, w2v, gk = wrefs
    for g in range(NSA_GROUPS):
        sl = slice(g * LANES, (g + 1) * LANES)
        kc = _compress_mlp(xk[g], w1k, b1k, w2k)
        kc_ref[:, sl] = (_unit_rms(kc) * gk[...]).astype(kc_ref.dtype)
        vc_ref[:, sl] = _compress_mlp(xv[g], w1v, b1v, w2v).astype(vc_ref.dtype)


def _compress_prompt_body(krows_ref, vrows_ref, *refs, nh):
    wrefs, (kc_ref, vc_ref) = refs[:7], refs[7:]

    def gather(rows_ref, g):
        return jnp.concatenate(
            [rows_ref[pl.ds(NSA_GROUPS * s + g, nh, stride=NSA_GROUPS * CMP_STRIDE), :].astype(BF16)
             for s in range(CMP_STRIDE)], axis=1)

    xk = [gather(krows_ref, g) for g in range(NSA_GROUPS)]
    xv = [gather(vrows_ref, g) for g in range(NSA_GROUPS)]
    _compress_finish(xk, xv, wrefs, kc_ref, vc_ref)


def _compress_sample_body(pt_ref, *refs, n_pages):
    del pt_ref
    kpages, vpages = refs[:n_pages], refs[n_pages:2 * n_pages]
    wrefs = refs[2 * n_pages:2 * n_pages + 7]
    kc_ref, vc_ref, stage_ref = refs[2 * n_pages + 7:]
    per_page = PAGE_SIZE // CMP_STRIDE
    nh = n_pages * per_page

    def gather(pages, g, slab):
        stage = stage_ref.at[slab]
        for p, page in enumerate(pages):
            for m in range(per_page):
                half = page[pl.ds(NSA_GROUPS * CMP_STRIDE * m + g, CMP_STRIDE, stride=NSA_GROUPS), :]
                stage[pl.ds((p * per_page + m) * STAGE_PITCH, CMP_STRIDE), :] = half
        return jnp.concatenate(
            [stage[pl.ds(s, nh, stride=STAGE_PITCH), :].astype(BF16) for s in range(CMP_STRIDE)], axis=1)

    xk = [gather(kpages, g, g) for g in range(NSA_GROUPS)]
    xv = [gather(vpages, g, NSA_GROUPS + g) for g in range(NSA_GROUPS)]
    _compress_finish(xk, xv, wrefs, kc_ref, vc_ref)


def _const_spec(shape, nargs):
    zeros = (0,) * len(shape)
    return pl.BlockSpec(shape, lambda *a: zeros)


def _compress_weights(w1, b1, w2):
    r_n = CMP_LEN // CMP_STRIDE
    w1r = w1.reshape(r_n, CMP_STRIDE * HEAD_DIM, CMP_HIDDEN)
    w1cat = jnp.concatenate([w1r[r] for r in range(r_n)], axis=1).astype(BF16)
    return w1cat, b1.reshape(1, CMP_HIDDEN), w2.astype(BF16)


def _compress_weight_specs():
    k16 = CMP_STRIDE * HEAD_DIM
    one = [_const_spec((k16, 2 * CMP_HIDDEN), 0), _const_spec((1, CMP_HIDDEN), 0),
           _const_spec((CMP_HIDDEN, HEAD_DIM), 0)]
    return one + one + [_const_spec((1, HEAD_DIM), 0)]


def _compress_prompt(krows, vrows, weights, batch, seq):
    nh = seq // CMP_STRIDE
    out = jax.ShapeDtypeStruct((batch, nh, KV_W), BF16)
    ospec = pl.BlockSpec((None, nh, KV_W), lambda b: (b, 0, 0))
    return pl.pallas_call(
        functools.partial(_compress_prompt_body, nh=nh),
        grid=(batch,),
        in_specs=[pl.BlockSpec((NSA_GROUPS * seq, LANES), lambda b: (b, 0))] * 2 + _compress_weight_specs(),
        out_specs=[ospec, ospec],
        out_shape=[out, out],
        compiler_params=_cparams("parallel"),
        name="compress_prompt",
    )(krows, vrows, *weights)


def _page_specs(n_pages, nb=1, bb=0):
    def spec(p):
        return pl.BlockSpec((None, NSA_GROUPS * PAGE_SIZE, HEAD_DIM),
                            lambda b, pt: (pt[(b * nb + bb) * n_pages + p], 0, 0))
    return [spec(p) for p in range(n_pages)]


def _compress_sample(pool_k, pool_v, pt_flat, weights, dec_batch, n_pages):
    nh = n_pages * PAGE_SIZE // CMP_STRIDE
    out = jax.ShapeDtypeStruct((dec_batch, nh, KV_W), BF16)
    ospec = pl.BlockSpec((None, nh, KV_W), lambda b, pt: (b, 0, 0))
    grid_spec = pltpu.PrefetchScalarGridSpec(
        num_scalar_prefetch=1,
        grid=(dec_batch,),
        in_specs=_page_specs(n_pages) + _page_specs(n_pages) + _compress_weight_specs(),
        out_specs=[ospec, ospec],
        scratch_shapes=[pltpu.VMEM((2 * NSA_GROUPS, nh * STAGE_PITCH, HEAD_DIM), F32)],
    )
    return pl.pallas_call(
        functools.partial(_compress_sample_body, n_pages=n_pages),
        grid_spec=grid_spec,
        out_shape=[out, out],
        compiler_params=_cparams("parallel"),
        name="compress_sample",
    )(pt_flat, *([pool_k] * n_pages), *([pool_v] * n_pages), *weights)


def _masked_softmax(s3, ok):
    s3 = jnp.where(ok[None], s3, NEG)
    m = jnp.max(s3, axis=-1, keepdims=True)
    e = jnp.where(ok[None], jnp.exp(s3 - m), 0.0)
    den = jnp.sum(e, axis=-1, keepdims=True)
    return e / jnp.where(den > 0.0, den, 1.0)


def _topk_lanes(score, n_sel, k):
    lane = _iota(score.shape, 1)
    sc = jnp.where(lane < n_sel, score, -jnp.inf)
    rank = jnp.zeros(score.shape, F32)
    for i in range(n_sel):
        ci = sc[:, i:i + 1]
        later = jnp.where(lane > i, 1.0, 0.0)
        rank = rank + jnp.where(ci > sc, 1.0, jnp.where(ci == sc, later, 0.0))
    return jnp.where((rank < k) & (lane < n_sel), 1.0, 0.0)


def _bias_softmax_pv(s3, ok, v):
    h, t, n = s3.shape
    s3 = s3 + jnp.where(ok, 0.0, NEG)[None]
    e = jnp.exp(s3 - jnp.max(s3, axis=-1, keepdims=True))
    inv = 1.0 / jnp.sum(e, axis=-1, keepdims=True)
    return _dot(e.reshape(h * t, n).astype(BF16), v) * inv.reshape(h * t, 1)


def _compressed_branch(qg, kcg, vcg, qpos, n_cmp):
    t = qpos.shape[0]
    nc_pad = kcg.shape[0]
    c_idx = _iota((t, nc_pad), 1)
    vis = (c_idx * CMP_STRIDE + (CMP_LEN - 1) <= qpos) & (c_idx < n_cmp)
    p3 = _masked_softmax(_dot_nt(qg, kcg).reshape(HPG, t, nc_pad), vis)
    o_c = _dot(p3.reshape(HPG * t, nc_pad).astype(BF16), vcg)
    return o_c, jnp.sum(p3, axis=0)


def _overlap(nc_pad, blocks_first):
    shape = (LANES, nc_pad) if blocks_first else (nc_pad, LANES)
    ci = _iota(shape, 1 if blocks_first else 0) * CMP_STRIDE
    sj = _iota(shape, 0 if blocks_first else 1) * SEL_BLK
    return jnp.where((ci < sj + SEL_BLK) & (ci + CMP_LEN > sj), 1.0, 0.0).astype(BF16)


def _select_lanes(psum, qpos, n_sel):
    t, nc_pad = psum.shape
    hi, lo = _split_bf16(psum)
    ov = _overlap(nc_pad, False)
    score = _dot(hi, ov) + _dot(lo, ov)
    j = _iota((t, LANES), 1)
    cur = qpos >> SEL_SHIFT
    forced = (j == 0) | (j == cur) | (j == cur - 1)
    score = jnp.where(forced, BIG, score)
    score = jnp.where(j * SEL_BLK <= qpos, score, NEG)
    return _topk_lanes(score, n_sel, min(SEL_TOPK, n_sel))


def _select_sublanes(psum, start, n_sel):
    t, nc_pad = psum.shape
    hi, lo = _split_bf16(psum)
    ov = _overlap(nc_pad, True)
    rows = -(-n_sel // 8) * 8
    sc = (_dot_nt(ov, hi) + _dot_nt(ov, lo))[:rows]
    j = _iota((rows, t), 0)
    qp = start + _iota((rows, t), 1)
    cur = qp >> SEL_SHIFT
    forced = (j == 0) | (j == cur) | (j == cur - 1)
    sc = jnp.where(forced, BIG, sc)
    sc = jnp.where(j * SEL_BLK <= qp, sc, NEG)
    sc = jnp.where(j < n_sel, sc, -jnp.inf)
    rank = jnp.zeros((rows, t), F32)
    for i in range(n_sel):
        ri = sc[i:i + 1, :]
        later = jnp.where(j > i, 1.0, 0.0)
        rank = rank + jnp.where(ri > sc, 1.0, jnp.where(ri == sc, later, 0.0))
    sel = jnp.where((rank < min(SEL_TOPK, n_sel)) & (j < n_sel), 1.0, 0.0).astype(BF16)
    if rows < LANES:
        sel = jnp.concatenate([sel, jnp.zeros((LANES - rows, t), BF16)], axis=0)
    return sel


def _block_to_key(blk0, n):
    blk = blk0 + (_iota((LANES, n), 1) >> SEL_SHIFT)
    return jnp.where(_iota((LANES, n), 0) == blk, 1.0, 0.0).astype(BF16)


def _nsa_prompt_body(q_ref, gate_ref, kc_ref, vc_ref, kv_ref, o_ref, *, seq):
    tq = Q_BLK
    start = pl.program_id(1) * tq
    n_sel = seq // SEL_BLK
    n_cmp = seq // CMP_STRIDE - CMP_LEN // CMP_STRIDE + 1
    qpos = start + _iota((tq, 1), 0)
    gates = gate_ref[...]
    wk = WINDOW + tq
    for g in range(NSA_GROUPS):
        gsl = slice(g * LANES, (g + 1) * LANES)
        qg = jnp.concatenate(
            [q_ref[:, (g * HPG + j) * LANES:(g * HPG + j + 1) * LANES] for j in range(HPG)], axis=0)
        o_c, psum = _compressed_branch(qg, kc_ref[:, gsl], vc_ref[:, gsl], qpos, n_cmp)
        sel_t = _select_sublanes(psum, start, n_sel)

        def sel_step(k, carry):
            m, l, acc = carry
            k0 = pl.multiple_of(k * SEL_KV_TILE, SEL_KV_TILE)
            kt = kv_ref[pl.ds(k0, SEL_KV_TILE), pl.ds(g * LANES, LANES)]
            vt = kv_ref[pl.ds(k0, SEL_KV_TILE), pl.ds(KV_W + g * LANES, LANES)]
            chosen = _dot_tn(sel_t, _block_to_key(k * (SEL_KV_TILE // SEL_BLK), SEL_KV_TILE))
            ok = (chosen > 0.5) & (k0 + _iota((tq, SEL_KV_TILE), 1) <= qpos)
            s3 = _dot_nt(qg, kt).reshape(HPG, tq, SEL_KV_TILE) + jnp.where(ok, 0.0, NEG)[None]
            m_new = jnp.maximum(m, jnp.max(s3, axis=-1, keepdims=True))
            alpha = jnp.exp(m - m_new)
            p = jnp.exp(s3 - m_new)
            l = alpha * l + jnp.sum(p, axis=-1, keepdims=True)
            pv = _dot(p.reshape(HPG * tq, SEL_KV_TILE).astype(BF16), vt)
            return m_new, l, alpha * acc + pv.reshape(HPG, tq, LANES)

        n_kv = (start + tq + SEL_KV_TILE - 1) // SEL_KV_TILE
        init = (jnp.full((HPG, tq, 1), NEG, F32), jnp.zeros((HPG, tq, 1), F32),
                jnp.zeros((HPG, tq, LANES), F32))
        _, l_s, acc_s = lax.fori_loop(0, n_kv, sel_step, init)
        o_s = acc_s * (1.0 / l_s)

        w0 = pl.multiple_of(jnp.maximum(start - WINDOW, 0), Q_BLK)
        kw = kv_ref[pl.ds(w0, wk), pl.ds(2 * KV_W + g * LANES, LANES)]
        vw = kv_ref[pl.ds(w0, wk), pl.ds(3 * KV_W + g * LANES, LANES)]
        dpos = qpos - (w0 + _iota((tq, wk), 1))
        o_w = _bias_softmax_pv(_dot_nt(qg, kw).reshape(HPG, tq, wk), (dpos >= 0) & (dpos < WINDOW), vw)

        for j in range(HPG):
            h = g * HPG + j
            rows = slice(j * tq, (j + 1) * tq)
            o = (o_c[rows] * gates[:, 3 * h:3 * h + 1] + o_s[j] * gates[:, 3 * h + 1:3 * h + 2]
                 + o_w[rows] * gates[:, 3 * h + 2:3 * h + 3])
            o_ref[:, h * LANES:(h + 1) * LANES] = o.astype(o_ref.dtype)


def _nsa_prompt(q, gate, kc, vc, kvb, batch, seq):
    nq = seq // Q_BLK
    nh = seq // CMP_STRIDE
    return pl.pallas_call(
        functools.partial(_nsa_prompt_body, seq=seq),
        grid=(batch, nq),
        in_specs=[pl.BlockSpec((Q_BLK, NSA_Q), lambda b, i: (b * nq + i, 0)),
                  pl.BlockSpec((Q_BLK, LANES), lambda b, i: (b * nq + i, 0)),
                  pl.BlockSpec((None, nh, KV_W), lambda b, i: (b, 0, 0)),
                  pl.BlockSpec((None, nh, KV_W), lambda b, i: (b, 0, 0)),
                  pl.BlockSpec((seq, 4 * KV_W), lambda b, i: (b, 0))],
        out_specs=pl.BlockSpec((Q_BLK, NSA_Q), lambda b, i: (b * nq + i, 0)),
        out_shape=jax.ShapeDtypeStruct((batch * seq, NSA_Q), BF16),
        compiler_params=_cparams("parallel", "parallel"),
        name="nsa_prompt",
    )(q, gate, kc, vc, kvb)


def _nsa_sample_one(q, gates, new_refs, kc_ref, vc_ref, kpages, vpages, kwbuf_ref, vwbuf_ref,
                    kwout_ref, vwout_ref, past):
    t = q.shape[0]
    wb = kwbuf_ref.shape[0] // NSA_GROUPS
    n_past = past // SEL_BLK
    n_sel = n_past + -(-t // SEL_BLK)
    n_cmp = (past + t) // CMP_STRIDE - CMP_LEN // CMP_STRIDE + 1
    qpos = past + _iota((t, 1), 0)
    pad = jnp.zeros((LANES - t, LANES), BF16)
    n_keys = past + LANES
    wk = wb + LANES
    heads = []
    for g in range(NSA_GROUPS):
        gsl = slice(g * LANES, (g + 1) * LANES)

        def new_rows(which):
            return new_refs[which][pl.ds(g, t, stride=NSA_GROUPS), :]

        qg = jnp.concatenate(
            [q[:, (g * HPG + j) * LANES:(g * HPG + j + 1) * LANES] for j in range(HPG)], axis=0).astype(BF16)
        o_c, psum = _compressed_branch(qg, kc_ref[:, gsl], vc_ref[:, gsl], qpos, n_cmp)
        sel = _select_lanes(psum, qpos, n_sel)

        def keys(pages, which):
            past_rows = [p[pl.ds(g, PAGE_SIZE, stride=NSA_GROUPS), :].astype(BF16) for p in pages]
            return jnp.concatenate(past_rows + [new_rows(which).astype(BF16), pad], axis=0)

        ks, vs = keys(kpages, 2), keys(vpages, 3)
        chosen = _dot(sel.astype(BF16), _block_to_key(0, n_keys))
        ok = (chosen > 0.5) & (_iota((t, n_keys), 1) <= qpos)
        o_s = _bias_softmax_pv(_dot_nt(qg, ks).reshape(HPG, t, n_keys), ok, vs)

        def window(buf_ref, which):
            return jnp.concatenate([buf_ref[pl.ds(g, wb, stride=NSA_GROUPS), :].astype(BF16),
                                    new_rows(which).astype(BF16), pad], axis=0)

        kw, vw = window(kwbuf_ref, 4), window(vwbuf_ref, 5)
        dpos = qpos - (past - wb + _iota((t, wk), 1))
        o_w = _bias_softmax_pv(_dot_nt(qg, kw).reshape(HPG, t, wk), (dpos >= 0) & (dpos < WINDOW), vw)

        for j in range(HPG):
            h = g * HPG + j
            rows = slice(j * t, (j + 1) * t)
            heads.append(o_c[rows] * gates[:, 3 * h:3 * h + 1] + o_s[rows] * gates[:, 3 * h + 1:3 * h + 2]
                         + o_w[rows] * gates[:, 3 * h + 2:3 * h + 3])

        kwout_ref[pl.ds((wb - t) * NSA_GROUPS + g, t, stride=NSA_GROUPS), :] = new_rows(4)
        vwout_ref[pl.ds((wb - t) * NSA_GROUPS + g, t, stride=NSA_GROUPS), :] = new_rows(5)
    keep = (wb - t) * NSA_GROUPS
    kwout_ref[pl.ds(0, keep), :] = kwbuf_ref[pl.ds(t * NSA_GROUPS, keep), :]
    vwout_ref[pl.ds(0, keep), :] = vwbuf_ref[pl.ds(t * NSA_GROUPS, keep), :]
    return heads


def _nsa_sample_body(pt_ref, q_ref, gate_ref, ksn_ref, vsn_ref, kwn_ref, vwn_ref, kc_ref, vc_ref, *refs,
                     n_pages, past, nb):
    del pt_ref
    kpages, vpages = refs[:nb * n_pages], refs[nb * n_pages:2 * nb * n_pages]
    kwbuf_ref, vwbuf_ref, o_ref, kwout_ref, vwout_ref = refs[2 * nb * n_pages:]
    t = q_ref.shape[0] // nb
    q = q_ref[...].astype(F32)
    gates = gate_ref[...]
    per_item = []
    for bb in range(nb):
        tok = slice(bb * t, (bb + 1) * t)
        tok2 = pl.ds(bb * NSA_GROUPS * t, NSA_GROUPS * t)
        new_refs = {2: ksn_ref.at[tok2], 3: vsn_ref.at[tok2], 4: kwn_ref.at[tok2], 5: vwn_ref.at[tok2]}
        pages = slice(bb * n_pages, (bb + 1) * n_pages)
        per_item.append(_nsa_sample_one(
            q[tok], gates[tok], new_refs, kc_ref.at[bb], vc_ref.at[bb], kpages[pages], vpages[pages],
            kwbuf_ref.at[bb], vwbuf_ref.at[bb], kwout_ref.at[bb], vwout_ref.at[bb], past))
    for h in range(NSA_HEADS):
        o_ref[:, h * LANES:(h + 1) * LANES] = jnp.concatenate(
            [heads[h] for heads in per_item], axis=0).astype(o_ref.dtype)


def _nsa_sample(q, gate, new, row_block0, kc, vc, pool_k, pool_v, kwbuf, vwbuf, pt_flat, dec_batch, t, n_pages,
                nb=2):
    past = n_pages * PAGE_SIZE
    nh = past // CMP_STRIDE
    wrows = kwbuf.shape[1]

    def rows(r, n):
        return pl.BlockSpec((nb * r, n), lambda b, pt: (row_block0 + b, 0))

    def per_b(r, c):
        return pl.BlockSpec((nb, r, c), lambda b, pt: (b, 0, 0))

    pages = [spec for bb in range(nb) for spec in _page_specs(n_pages, nb, bb)]
    grid_spec = pltpu.PrefetchScalarGridSpec(
        num_scalar_prefetch=1,
        grid=(dec_batch // nb,),
        in_specs=[rows(t, NSA_Q), rows(t, LANES)] + [rows(NSA_GROUPS * t, HEAD_DIM)] * 4
        + [per_b(nh, KV_W), per_b(nh, KV_W)] + pages + pages + [per_b(wrows, HEAD_DIM), per_b(wrows, HEAD_DIM)],
        out_specs=[pl.BlockSpec((nb * t, NSA_Q), lambda b, pt: (b, 0)), per_b(wrows, HEAD_DIM),
                   per_b(wrows, HEAD_DIM)],
    )
    wout = jax.ShapeDtypeStruct(kwbuf.shape, F32)
    pools = [pool_k] * (nb * n_pages) + [pool_v] * (nb * n_pages)
    return pl.pallas_call(
        functools.partial(_nsa_sample_body, n_pages=n_pages, past=past, nb=nb),
        grid_spec=grid_spec,
        out_shape=[jax.ShapeDtypeStruct((dec_batch * t, NSA_Q), BF16), wout, wout],
        compiler_params=_cparams("parallel"),
        name="nsa_sample",
    )(pt_flat, q, gate, *new, kc, vc, *pools, kwbuf, vwbuf)


def _log_decay(h):
    return math.log(1.0 - 2.0 ** (-5.0 - h))


def _ret_finish(o, h, gn_ref, rg):
    sl = slice(h * RET_DV, (h + 1) * RET_DV)
    return _unit_rms(o) * gn_ref[:, sl] * rg[:, sl]


def _ret_prompt_body(rq_ref, rk_ref, rv_ref, rg_ref, gn_ref, o_ref, s_ref):
    c = RET_CHUNK

    @pl.when(pl.program_id(1) == 0)
    def _():
        s_ref[...] = jnp.zeros(s_ref.shape, F32)

    n_col = _iota((c, 1), 0).astype(F32)
    diff = (_iota((c, c), 0) - _iota((c, c), 1)).astype(F32)
    rg = rg_ref[...].astype(F32)
    for h in range(RET_HEADS):
        lg = _log_decay(h)
        q = rq_ref[:, h * RET_DK:(h + 1) * RET_DK]
        k = rk_ref[:, h * RET_DK:(h + 1) * RET_DK]
        v = rv_ref[:, h * RET_DV:(h + 1) * RET_DV]
        s_prev = s_ref[h]
        dmask = jnp.where(diff >= 0.0, jnp.exp(jnp.maximum(diff, 0.0) * lg), 0.0)
        o = _dot((_dot_nt(q, k) * dmask).astype(BF16), v)
        q_dec = (q.astype(F32) * jnp.exp((n_col + 1.0) * lg)).astype(BF16)
        o = o + _dot(q_dec, s_prev.astype(BF16))
        k_dec = (k.astype(F32) * jnp.exp((c - 1.0 - n_col) * lg)).astype(BF16)
        s_ref[h] = math.exp(c * lg) * s_prev + _dot_tn(k_dec, v)
        o_ref[:, h * RET_DV:(h + 1) * RET_DV] = _ret_finish(o, h, gn_ref, rg).astype(o_ref.dtype)


def _ret_prompt(rot, rv, rg, gn, batch, seq):
    nch = seq // RET_CHUNK
    qk_w = RET_QK

    def rows(n, col):
        return pl.BlockSpec((RET_CHUNK, n), lambda b, i: (b * nch + i, col))

    return pl.pallas_call(
        _ret_prompt_body,
        grid=(batch, nch),
        in_specs=[rows(qk_w, 0), rows(qk_w, 1), rows(RET_V, 0), rows(RET_V, 0),
                  pl.BlockSpec((1, RET_V), lambda b, i: (0, 0))],
        out_specs=[rows(RET_V, 0),
                   pl.BlockSpec((None, RET_HEADS, RET_DK, RET_DV), lambda b, i: (b, 0, 0, 0))],
        out_shape=[jax.ShapeDtypeStruct((batch * seq, RET_V), BF16),
                   jax.ShapeDtypeStruct((batch, RET_HEADS, RET_DK, RET_DV), F32)],
        compiler_params=_cparams("parallel", "arbitrary"),
        name="retention_prompt",
    )(rot, rot, rv, rg, gn)


def _ret_sample_body(rq_ref, rk_ref, rv_ref, rg_ref, gn_ref, s_ref, o_ref, so_ref):
    c = rq_ref.shape[0]
    n_col = _iota((c, 1), 0).astype(F32)
    rg = rg_ref[...]
    zk = jnp.zeros((LANES - c, RET_DK), BF16)
    zv = jnp.zeros((LANES - c, RET_DV), BF16)
    for h in range(RET_HEADS):
        lg = _log_decay(h)
        q = rq_ref[:, h * RET_DK:(h + 1) * RET_DK]
        k = rk_ref[:, h * RET_DK:(h + 1) * RET_DK]
        v = rv_ref[:, h * RET_DV:(h + 1) * RET_DV]
        s_prev = s_ref[h]
        o = _dot((q * jnp.exp((n_col + 1.0) * lg)).astype(BF16), s_prev.astype(BF16))
        for j in range(c):
            qk = jnp.sum(q * k[j:j + 1, :], axis=-1, keepdims=True)
            dj = jnp.where(n_col >= j, jnp.exp(jnp.maximum(n_col - j, 0.0) * lg), 0.0)
            o = o + (qk * dj) * v[j:j + 1, :]
        k_dec = jnp.concatenate([(k * jnp.exp((c - 1.0 - n_col) * lg)).astype(BF16), zk], axis=0)
        v_pad = jnp.concatenate([v.astype(BF16), zv], axis=0)
        so_ref[h] = math.exp(c * lg) * s_prev + _dot_tn(k_dec, v_pad)
        o_ref[:, h * RET_DV:(h + 1) * RET_DV] = _ret_finish(o, h, gn_ref, rg)


def _ret_sample(rq, rk, rv, rg, gn, state, dec_batch, t):
    def rows(n):
        return pl.BlockSpec((t, n), lambda b: (b, 0))

    sspec = pl.BlockSpec((None, RET_HEADS, RET_DK, RET_DV), lambda b: (b, 0, 0, 0))
    return pl.pallas_call(
        _ret_sample_body,
        grid=(dec_batch,),
        in_specs=[rows(RET_QK), rows(RET_QK), rows(RET_V), rows(RET_V),
                  pl.BlockSpec((1, RET_V), lambda b: (0, 0)), sspec],
        out_specs=[rows(RET_V), sspec],
        out_shape=[jax.ShapeDtypeStruct((dec_batch * t, RET_V), F32),
                   jax.ShapeDtypeStruct(state.shape, F32)],
        compiler_params=_cparams("parallel"),
        name="retention_sample",
    )(rq, rk, rv, rg, gn, state)


def _merge_body(on_ref, or_ref, wn_ref, wr_ref, ga_ref, gr_ref, o_ref):
    a = _dot(on_ref[...], wn_ref[...])
    r = _dot(or_ref[...], wr_ref[...])
    o_ref[...] = (ga_ref[...].astype(F32) * a + gr_ref[...].astype(F32) * r).astype(o_ref.dtype)


def _merge(o_nsa, o_ret, wn, wr, gates, tn=1024):
    t = o_nsa.shape[0]
    d = wn.shape[1]
    ga0 = 0
    gr0 = d // tn
    return pl.pallas_call(
        _merge_body,
        grid=(d // tn, t // ROW_TILE),
        in_specs=[pl.BlockSpec((ROW_TILE, NSA_Q), lambda j, i: (i, 0)),
                  pl.BlockSpec((ROW_TILE, RET_V), lambda j, i: (i, 0)),
                  pl.BlockSpec((NSA_Q, tn), lambda j, i: (0, j)),
                  pl.BlockSpec((RET_V, tn), lambda j, i: (0, j)),
                  pl.BlockSpec((ROW_TILE, tn), lambda j, i: (i, ga0 + j)),
                  pl.BlockSpec((ROW_TILE, tn), lambda j, i: (i, gr0 + j))],
        out_specs=pl.BlockSpec((ROW_TILE, tn), lambda j, i: (i, j)),
        out_shape=jax.ShapeDtypeStruct((t, d), BF16),
        compiler_params=_cparams("parallel", "parallel"),
        name="merge",
    )(o_nsa, o_ret, wn, wr, gates, gates)


def _out_router_body(xp_ref, xs_ref, mix_ref, wo_ref, g_ref, rwh_ref, rwl_ref, rb_ref,
                     x1_ref, h_ref, route_ref, cnt_ref, *, tiles_p):
    @pl.when(pl.program_id(0) == 0)
    def _():
        cnt_ref[...] = jnp.zeros(cnt_ref.shape, F32)

    x = jnp.where(pl.program_id(0) < tiles_p, xp_ref[...], xs_ref[...])
    x1 = x + _dot(mix_ref[...], wo_ref[...])
    x1_ref[...] = x1
    h = _unit_rms(x1) * g_ref[...]
    tm = h.shape[0]
    for cc in range(h.shape[1] // LANES):
        h_ref[_flat_idx(0, tm, cc)] = h[:, cc * LANES:(cc + 1) * LANES]
    for cc in range(h.shape[1] // LANES, ROW_PITCH):
        h_ref[_flat_idx(0, tm, cc)] = jnp.zeros((tm, LANES), F32)
    tm = h.shape[0]
    lane = _iota((tm, LANES), 1)
    hi, lo = _split_bf16(h)
    logits = _dot(hi, rwh_ref[...]) + _dot(lo, rwh_ref[...]) + _dot(hi, rwl_ref[...]) + rb_ref[...]
    work = jnp.where(lane < N_EXPERTS, logits, -jnp.inf)
    vals, idxs = [], []
    for _ in range(TOP_K):
        v = jnp.max(work, axis=-1, keepdims=True)
        ix = jnp.min(jnp.where(work == v, lane, LANES), axis=-1, keepdims=True)
        vals.append(v)
        idxs.append(ix)
        work = jnp.where(lane == ix, -jnp.inf, work)
    es = [jnp.exp(v - vals[0]) for v in vals]
    den = es[0] + es[1] + es[2] + es[3]
    hot = jnp.zeros((tm, LANES), F32)
    for ix in idxs:
        hot = hot + jnp.where(lane == ix, 1.0, 0.0)
    before = jnp.where(_iota((tm, tm), 1) < _iota((tm, tm), 0), 1.0, 0.0).astype(BF16)
    ranks = _dot(before, hot.astype(BF16)) + cnt_ref[...]
    route = jnp.zeros((tm, LANES), F32)
    for k in range(TOP_K):
        rk = jnp.sum(jnp.where(lane == idxs[k], ranks, 0.0), axis=-1, keepdims=True)
        route = route + jnp.where(lane == k, idxs[k].astype(F32), 0.0)
        route = route + jnp.where(lane == TOP_K + k, es[k] / den, 0.0)
        route = route + jnp.where(lane == 2 * TOP_K + k, rk, 0.0)
    route_ref[...] = route
    cnt_ref[...] = cnt_ref[...] + jnp.sum(hot, axis=0, keepdims=True)


def _out_router(xp, xs, mixed, wo, g2, rwh, rwl, rb, tm=256):
    d = xp.shape[1]
    t = xp.shape[0] + xs.shape[0]
    tiles_p = xp.shape[0] // tm
    rows = pl.BlockSpec((tm, d), lambda i: (i, 0))
    lanes = pl.BlockSpec((tm, LANES), lambda i: (i, 0))

    def const(r, c):
        return pl.BlockSpec((r, c), lambda i: (0, 0))

    return pl.pallas_call(
        functools.partial(_out_router_body, tiles_p=tiles_p),
        grid=(t // tm,),
        in_specs=_two_group_specs(tm, d, tiles_p)
        + [rows, const(d, d), const(1, d), const(d, LANES), const(d, LANES), const(1, LANES)],
        out_specs=[rows, pl.BlockSpec((tm * ROW_PITCH, LANES), lambda i: (i, 0)), lanes, const(1, LANES)],
        out_shape=[jax.ShapeDtypeStruct((t, d), F32), jax.ShapeDtypeStruct((t * ROW_PITCH, LANES), F32),
                   jax.ShapeDtypeStruct((t, LANES), F32), jax.ShapeDtypeStruct((1, LANES), F32)],
        compiler_params=_cparams("arbitrary"),
        name="out_router",
    )(xp, xs, mixed, wo, g2, rwh, rwl, rb)


def _row_copy(src, r_src, dst, r_dst, sem):
    return pltpu.make_async_copy(src.at[pl.ds(r_src * ROW_PITCH, ROW_PITCH), :],
                                 dst.at[pl.ds(r_dst * ROW_PITCH, ROW_PITCH), :], sem)


def _flat_idx(row0, n, c):
    return (pl.ds(row0 * ROW_PITCH + c, n, stride=ROW_PITCH), slice(None))


def _dispatch_body(start_ref, cnt_ref, pend_ref, e_ref, rank_ref, h_ref, xs_ref, zero_ref, sem, zsem):
    tm = h_ref.shape[0] // ROW_PITCH

    def slot(r, k):
        return start_ref[e_ref[r * TOP_K + k]] + rank_ref[r * TOP_K + k]

    def issue(r, _):
        for k in range(TOP_K):
            _row_copy(h_ref, r, xs_ref, slot(r, k), sem).start(priority=k % 2)
        return 0

    def drain(r, _):
        for k in range(TOP_K):
            _row_copy(h_ref, r, xs_ref, slot(r, k), sem).wait()
        return 0

    lax.fori_loop(0, tm, issue, 0)

    @pl.when(pl.program_id(0) == 0)
    def _():
        zero_ref[...] = jnp.zeros(zero_ref.shape, F32)
        zrows = zero_ref.shape[0] // ROW_PITCH
        for phase in ("start", "wait"):
            def per_expert(e, _):
                def per_row(s, _):
                    cp = _row_copy(zero_ref, 0, xs_ref, s, zsem)
                    cp.start() if phase == "start" else cp.wait()
                    return 0
                return lax.fori_loop(start_ref[e] + cnt_ref[e], pend_ref[e], per_row, 0)
            lax.fori_loop(0, N_EXPERTS, per_expert, 0)

            def per_chunk(s, _):
                s0 = pl.multiple_of(s * (zrows * ROW_PITCH), zrows * ROW_PITCH)
                cp = pltpu.make_async_copy(zero_ref, xs_ref.at[pl.ds(s0, zrows * ROW_PITCH), :], zsem)
                cp.start() if phase == "start" else cp.wait()
                return 0
            lax.fori_loop(pend_ref[N_EXPERTS - 1] // zrows, xs_ref.shape[0] // (zrows * ROW_PITCH), per_chunk, 0)

    lax.fori_loop(0, tm, drain, 0)


def _dispatch(h_flat, e_flat, rank_flat, pad_start, counts, pad_end, n_slots, tm=256):
    t = h_flat.shape[0] // ROW_PITCH
    smem = pl.BlockSpec((tm * TOP_K,), lambda i, *_: (i,), memory_space=pltpu.SMEM)
    grid_spec = pltpu.PrefetchScalarGridSpec(
        num_scalar_prefetch=3,
        grid=(t // tm,),
        in_specs=[smem, smem, pl.BlockSpec((tm * ROW_PITCH, LANES), lambda i, *_: (i, 0))],
        out_specs=pl.BlockSpec(memory_space=pl.ANY),
        scratch_shapes=[pltpu.VMEM((MOE_TILE // 8 * ROW_PITCH, LANES), F32), pltpu.SemaphoreType.DMA(()),
                        pltpu.SemaphoreType.DMA(())],
    )
    return pl.pallas_call(
        _dispatch_body,
        grid_spec=grid_spec,
        out_shape=jax.ShapeDtypeStruct((n_slots * ROW_PITCH, LANES), F32),
        compiler_params=pltpu.CompilerParams(dimension_semantics=("arbitrary",), vmem_limit_bytes=VMEM_LIMIT,
                                             has_side_effects=True),
        name="moe_dispatch",
    )(pad_start, counts, pad_end, e_flat, rank_flat, h_flat)


def _combine_body(start_ref, e_ref, rank_ref, x1_ref, route_ref, y_ref, op_ref, os_ref, buf_ref, sem, *, tiles_p):
    tm = x1_ref.shape[0]

    def slot(r, k):
        return start_ref[e_ref[r * TOP_K + k]] + rank_ref[r * TOP_K + k]

    def copy(r, k):
        return pltpu.make_async_copy(y_ref.at[pl.ds(slot(r, k), 1), :], buf_ref.at[k, pl.ds(r, 1), :], sem)

    def issue(r, _):
        for k in range(TOP_K):
            copy(r, k).start(priority=k % 2)
        return 0

    def drain(r, _):
        for k in range(TOP_K):
            copy(r, k).wait()
        return 0

    lax.fori_loop(0, tm, issue, 0)
    lax.fori_loop(0, tm, drain, 0)
    route = route_ref[...]
    out = x1_ref[...]
    for k in range(TOP_K):
        out = out + route[:, TOP_K + k:TOP_K + k + 1] * buf_ref[k]

    @pl.when(pl.program_id(0) < tiles_p)
    def _():
        op_ref[...] = out

    @pl.when(pl.program_id(0) >= tiles_p)
    def _():
        os_ref[...] = out


def _combine(x1, route, y, e_flat, rank_flat, pad_start, n_prompt, tm=256):
    t, d = x1.shape
    tiles_p = n_prompt // tm
    smem = pl.BlockSpec((tm * TOP_K,), lambda i, *_: (i,), memory_space=pltpu.SMEM)
    grid_spec = pltpu.PrefetchScalarGridSpec(
        num_scalar_prefetch=1,
        grid=(t // tm,),
        in_specs=[smem, smem, pl.BlockSpec((tm, d), lambda i, *_: (i, 0)),
                  pl.BlockSpec((tm, LANES), lambda i, *_: (i, 0)),
                  pl.BlockSpec(memory_space=pl.ANY)],
        out_specs=[pl.BlockSpec((tm, d), lambda i, *_: (jnp.minimum(i, tiles_p - 1), 0)),
                   pl.BlockSpec((tm, d), lambda i, *_: (jnp.maximum(i - tiles_p, 0), 0))],
        scratch_shapes=[pltpu.VMEM((TOP_K, tm, d), F32), pltpu.SemaphoreType.DMA(())],
    )
    return pl.pallas_call(
        functools.partial(_combine_body, tiles_p=tiles_p),
        grid_spec=grid_spec,
        out_shape=[jax.ShapeDtypeStruct((n_prompt, d), F32), jax.ShapeDtypeStruct((t - n_prompt, d), F32)],
        compiler_params=_cparams("arbitrary"),
        name="moe_combine",
    )(pad_start, e_flat, rank_flat, x1, route, y)


def _work_list(tiles_e, tile_start, n_tiles, n_chunks):
    steps_e = tiles_e * n_chunks
    cum = jnp.cumsum(steps_e)
    total = cum[-1]
    s = jnp.arange(n_tiles * n_chunks, dtype=jnp.int32)
    sv = jnp.minimum(s, total - 1)
    e = jnp.minimum(jnp.searchsorted(cum, sv, side="right"), N_EXPERTS - 1).astype(jnp.int32)
    local = sv - (cum[e] - steps_e[e])
    per = jnp.maximum(tiles_e[e], 1)
    c = local // per
    r = local - c * per
    tile = tile_start[e] + r
    real = s < total
    k = s - total
    out_tile = jnp.where(real, tile, total // n_chunks + k // n_chunks)
    out_chunk = jnp.where(real, c, k % n_chunks)
    flag = real.astype(jnp.int32) + 2 * (real & (r == 0)).astype(jnp.int32)
    return [a.astype(jnp.int32) for a in (e, c, tile, out_tile, out_chunk, flag)]


def _expert_up_body(e_ref, wc_ref, tin_ref, tout_ref, oc_ref, flag_ref, xs_ref, wg_ref, wu_ref, bg_ref, bu_ref,
                    a_ref, wgb_ref, wub_ref):
    del e_ref, wc_ref, tin_ref, tout_ref, oc_ref
    flag = flag_ref[pl.program_id(0)]

    @pl.when(flag >= 2)
    def _():
        wgb_ref[...] = wg_ref[...].astype(BF16)
        wub_ref[...] = wu_ref[...].astype(BF16)

    @pl.when(flag >= 1)
    def _():
        x = jnp.concatenate([xs_ref[_flat_idx(0, MOE_TILE, cc)].astype(BF16)
                             for cc in range(wgb_ref.shape[0] // LANES)], axis=1)
        g = jnp.minimum(_dot(x, wgb_ref[...]) + bg_ref[...], SWIGLU_LIMIT)
        u = jnp.clip(_dot(x, wub_ref[...]) + bu_ref[...], -SWIGLU_LIMIT, SWIGLU_LIMIT)
        a_ref[...] = ((u + 1.0) * g * _sigmoid(g * SWIGLU_ALPHA)).astype(a_ref.dtype)

    @pl.when(flag == 0)
    def _():
        a_ref[...] = jnp.zeros(a_ref.shape, a_ref.dtype)


def _expert_down_body(e_ref, wc_ref, tin_ref, tout_ref, oc_ref, flag_ref, a_ref, wd_ref, bd_ref, y_ref, wdb_ref):
    del e_ref, wc_ref, tin_ref, tout_ref, oc_ref
    flag = flag_ref[pl.program_id(0)]

    @pl.when(flag >= 2)
    def _():
        wdb_ref[...] = wd_ref[...].astype(BF16)

    @pl.when(flag >= 1)
    def _():
        y_ref[...] = _dot(a_ref[...], wdb_ref[...]) + bd_ref[...]

    @pl.when(flag == 0)
    def _():
        y_ref[...] = jnp.zeros(y_ref.shape, F32)


def _experts_two_pass(xs, tiles_e, tile_start, n_tiles, w_gu, b_gu, w_dn, b_dn):
    d = w_dn.shape[2]
    n_slots = n_tiles * MOE_TILE
    n_ff = D_FF // MOE_FF_TILE
    tf = MOE_FF_TILE
    up = _work_list(tiles_e, tile_start, n_tiles, n_ff)
    up_spec = pltpu.PrefetchScalarGridSpec(
        num_scalar_prefetch=6,
        grid=(n_tiles * n_ff,),
        in_specs=[
            pl.BlockSpec((MOE_TILE * ROW_PITCH, LANES), lambda s, e, wc, ti, to, oc, fl: (ti[s], 0)),
            pl.BlockSpec((None, d, tf), lambda s, e, wc, ti, to, oc, fl: (e[s], 0, wc[s])),
            pl.BlockSpec((None, d, tf), lambda s, e, wc, ti, to, oc, fl: (e[s], 0, n_ff + wc[s])),
            pl.BlockSpec((None, 1, tf), lambda s, e, wc, ti, to, oc, fl: (e[s], 0, wc[s])),
            pl.BlockSpec((None, 1, tf), lambda s, e, wc, ti, to, oc, fl: (e[s], 0, n_ff + wc[s])),
        ],
        out_specs=pl.BlockSpec((MOE_TILE, tf), lambda s, e, wc, ti, to, oc, fl: (to[s], oc[s])),
        scratch_shapes=[pltpu.VMEM((d, tf), BF16), pltpu.VMEM((d, tf), BF16)],
    )
    a = pl.pallas_call(
        _expert_up_body,
        grid_spec=up_spec,
        out_shape=jax.ShapeDtypeStruct((n_slots, D_FF), BF16),
        compiler_params=_cparams("arbitrary"),
        name="moe_up",
    )(*up, xs, w_gu, w_gu, b_gu, b_gu)

    tn = MOE_DOWN_TILE
    n_dn = d // tn
    down = _work_list(tiles_e, tile_start, n_tiles, n_dn)
    down_spec = pltpu.PrefetchScalarGridSpec(
        num_scalar_prefetch=6,
        grid=(n_tiles * n_dn,),
        in_specs=[
            pl.BlockSpec((MOE_TILE, D_FF), lambda s, e, wc, ti, to, oc, fl: (ti[s], 0)),
            pl.BlockSpec((None, D_FF, tn), lambda s, e, wc, ti, to, oc, fl: (e[s], 0, wc[s])),
            pl.BlockSpec((None, 1, tn), lambda s, e, wc, ti, to, oc, fl: (e[s], 0, wc[s])),
        ],
        out_specs=pl.BlockSpec((MOE_TILE, tn), lambda s, e, wc, ti, to, oc, fl: (to[s], oc[s])),
        scratch_shapes=[pltpu.VMEM((D_FF, tn), BF16)],
    )
    return pl.pallas_call(
        _expert_down_body,
        grid_spec=down_spec,
        out_shape=jax.ShapeDtypeStruct((n_slots, d), F32),
        compiler_params=_cparams("arbitrary"),
        name="moe_down",
    )(*down, a, w_dn, b_dn)


def _moe(x1, h, route, counts, w_gu, b_gu, w_dn, b_dn, n_prompt):
    t = x1.shape[0]
    n_tiles = -(-t * TOP_K // MOE_TILE) + N_EXPERTS
    n_slots = n_tiles * MOE_TILE
    cnt = counts[0, :N_EXPERTS].astype(jnp.int32)
    padded = (cnt + MOE_TILE - 1) // MOE_TILE * MOE_TILE
    pad_end = jnp.cumsum(padded)
    pad_start = pad_end - padded
    e_flat = route[:, :TOP_K].astype(jnp.int32).reshape(-1)
    rank_flat = route[:, 2 * TOP_K:3 * TOP_K].astype(jnp.int32).reshape(-1)
    xs = _dispatch(h, e_flat, rank_flat, pad_start, cnt, pad_end, n_slots)
    y = _experts_two_pass(xs, padded // MOE_TILE, pad_start // MOE_TILE, n_tiles, w_gu,
                          b_gu.reshape(N_EXPERTS, 1, -1), w_dn, b_dn.reshape(N_EXPERTS, 1, -1))
    return _combine(x1, route, y, e_flat, rank_flat, pad_start, n_prompt)


def _layer(xp, xs, pools, kwbuf, vwbuf, s_ret, page_table, p):
    batch, seq, d = xp.shape
    dec_batch, t_new, _ = xs.shape
    n_pages = page_table.shape[1]
    past = n_pages * PAGE_SIZE
    n_prompt = batch * seq
    n_sample = dec_batch * t_new
    assert seq & (seq - 1) == 0 and t_new & (t_new - 1) == 0 and seq % SEL_KV_TILE == 0
    assert n_prompt % ROW_TILE == 0 and n_sample % ROW_TILE == 0 and t_new % 8 == 0

    xp2, xs2 = xp.reshape(n_prompt, d), xs.reshape(n_sample, d)
    t_all = n_prompt + n_sample
    xn = _ln1(xp2, xs2, p["ln1_g"])

    w_in = p["w_in"]
    offs = [0]
    for n in (NSA_Q, KV_W, KV_W, KV_W, KV_W, KV_W, KV_W, 3 * NSA_HEADS, RET_QK, RET_QK, RET_V, RET_V, D_MODEL, D_MODEL):
        offs.append(offs[-1] + n)
    w_q = w_in[:, offs[0]:offs[1]].astype(BF16)
    w_kv = jnp.concatenate([w_in[:, offs[1]:offs[8]],
                            jnp.zeros((d, LANES - 3 * NSA_HEADS), F32)], axis=1).astype(BF16)
    w_rot = w_in[:, offs[8]:offs[10]].astype(BF16)
    w_act = w_in[:, offs[10]:offs[14]].astype(BF16)

    def rows(n):
        return pl.BlockSpec((ROW_TILE, n), lambda j, i: (i, 0))

    q = _proj_call(_proj_q_body, xn, w_q, [p["q_norm_g"].reshape(1, HEAD_DIM)], [_vec_spec(HEAD_DIM)],
                   jax.ShapeDtypeStruct((t_all, NSA_Q), BF16), rows(NSA_Q), NSA_Q, "proj_q")
    row2 = jax.ShapeDtypeStruct((NSA_GROUPS * t_all, HEAD_DIM), F32)
    row2_spec = pl.BlockSpec((NSA_GROUPS * ROW_TILE, HEAD_DIM), lambda j, i: (i, 0))
    *kv_rows, kvb, gate = _proj_call(
        _proj_kv_body, xn, w_kv,
        [p["k_sel_norm_g"].reshape(1, HEAD_DIM), p["k_win_norm_g"].reshape(1, HEAD_DIM)],
        [_vec_spec(HEAD_DIM), _vec_spec(HEAD_DIM)],
        [row2] * 6 + [jax.ShapeDtypeStruct((t_all, 4 * KV_W), BF16), jax.ShapeDtypeStruct((t_all, LANES), F32)],
        [row2_spec] * 6 + [rows(4 * KV_W), rows(LANES)], 6 * KV_W + LANES, "proj_kv")
    inv = 1.0 / (ROPE_BASE ** jnp.linspace(0.0, 1.0, RET_DK // 2, dtype=F32))
    rot = _proj_call(
        functools.partial(_proj_rot_body, n_prompt=n_prompt, seq=seq, past=past, dec_seq=t_new),
        xn, w_rot, [jnp.repeat(inv, 2).reshape(1, RET_DK)], [_vec_spec(RET_DK)],
        jax.ShapeDtypeStruct((t_all, 2 * RET_QK), BF16),
        pl.BlockSpec((ROW_TILE, RET_QK), lambda j, i: (i, j)), RET_QK, "proj_rot")
    act_tn = 1024

    def proj_act(w, act, name):
        return _proj_call(functools.partial(_proj_act_body, act=act), xn, w, [], [],
                          jax.ShapeDtypeStruct((t_all, w.shape[1]), BF16),
                          pl.BlockSpec((ROW_TILE, act_tn), lambda j, i: (i, j)), act_tn, name)

    rv = proj_act(w_act[:, :RET_V], "none", "proj_rv")
    rg = proj_act(w_act[:, RET_V:2 * RET_V], "silu", "proj_rg")
    merge_gates = proj_act(w_act[:, 2 * RET_V:], "sigmoid", "proj_gates")

    cmp_w = (_compress_weights(p["cmp_k_w1"], p["cmp_k_b1"], p["cmp_k_w2"])
             + _compress_weights(p["cmp_v_w1"], p["cmp_v_b1"], p["cmp_v_w2"])
             + (p["k_cmp_norm_g"].reshape(1, HEAD_DIM),))
    gn = p["ret_norm_g"].reshape(1, RET_V)

    kc_p, vc_p = _compress_prompt(kv_rows[0], kv_rows[1], cmp_w, batch, seq)
    o_nsa_p = _nsa_prompt(q, gate, kc_p, vc_p, kvb, batch, seq)
    o_ret_p, ret_p = _ret_prompt(rot, rv, rg, gn, batch, seq)

    pool_kc, pool_vc, pool_ks, pool_vs = [
        a.reshape(a.shape[0], NSA_GROUPS * PAGE_SIZE, HEAD_DIM) for a in pools]
    pt_flat = page_table.reshape(-1)
    kc_s, vc_s = _compress_sample(pool_kc, pool_vc, pt_flat, cmp_w, dec_batch, n_pages)
    wb = kwbuf.shape[1]
    o_nsa_s, kw_s, vw_s = _nsa_sample(
        q, gate, kv_rows[2:], n_prompt // (2 * t_new), kc_s, vc_s, pool_ks, pool_vs,
        kwbuf.reshape(dec_batch, wb * NSA_GROUPS, HEAD_DIM), vwbuf.reshape(dec_batch, wb * NSA_GROUPS, HEAD_DIM),
        pt_flat, dec_batch, t_new, n_pages)
    rot_s = rot[n_prompt:].astype(F32)
    o_ret_s, ret_s = _ret_sample(rot_s[:, :RET_QK], rot_s[:, RET_QK:], rv[n_prompt:].astype(F32),
                                 rg[n_prompt:].astype(F32),
                                 gn, s_ret, dec_batch, t_new)

    o_nsa = jnp.concatenate([o_nsa_p, o_nsa_s], axis=0)
    o_ret = jnp.concatenate([o_ret_p, o_ret_s.astype(BF16)], axis=0)
    mixed = _merge(o_nsa, o_ret, p["w_nsa_br"].astype(BF16), p["w_ret_br"].astype(BF16), merge_gates)
    rw = jnp.concatenate([p["router_w"], jnp.zeros((d, LANES - N_EXPERTS), F32)], axis=1)
    rwh, rwl = _split_bf16(rw)
    rb = jnp.concatenate([p["router_b"], jnp.zeros((LANES - N_EXPERTS,), F32)]).reshape(1, LANES)
    x1, h, route, counts = _out_router(xp2, xs2, mixed, p["w_out"].astype(BF16), p["ln2_g"].reshape(1, d),
                                       rwh, rwl, rb)
    y_p, y_s = _moe(x1, h, route, counts, p["w_gate_up"], p["b_gate_up"], p["w_down"], p["b_down"], n_prompt)

    kv5 = (NSA_GROUPS, HEAD_DIM)
    wbp = min(WINDOW, seq)
    rows_p = [a[:NSA_GROUPS * n_prompt].reshape(batch, seq, *kv5) for a in kv_rows]
    rows_s = [a[NSA_GROUPS * n_prompt:].reshape(dec_batch, t_new, *kv5) for a in kv_rows[:4]]
    states_p = (*rows_p[:4], rows_p[4][:, seq - wbp:], rows_p[5][:, seq - wbp:], ret_p)
    states_s = (*rows_s, kw_s.reshape(dec_batch, wb, *kv5), vw_s.reshape(dec_batch, wb, *kv5), ret_s)
    return y_p.reshape(batch, seq, d), y_s.reshape(dec_batch, t_new, d), states_p, states_s


def kernel(x_prompt, x_sample, cache_k_cmp, cache_v_cmp, cache_k_sel, cache_v_sel, state_k_win, state_v_win,
           state_ret, page_table, ln1_g, w_in, cmp_k_w1, cmp_k_b1, cmp_k_w2, cmp_v_w1, cmp_v_b1, cmp_v_w2,
           q_norm_g, k_cmp_norm_g, k_sel_norm_g, k_win_norm_g, ret_norm_g, w_nsa_br, w_ret_br, w_out, ln2_g,
           router_w, router_b, w_gate_up, b_gate_up, w_down, b_down):
    depth = w_in.shape[0]
    y_p, y_s = x_prompt, x_sample
    new_p, new_s = [], []
    for l in range(depth):
        p = {
            "ln1_g": ln1_g[l], "w_in": w_in[l],
            "cmp_k_w1": cmp_k_w1[l], "cmp_k_b1": cmp_k_b1[l], "cmp_k_w2": cmp_k_w2[l],
            "cmp_v_w1": cmp_v_w1[l], "cmp_v_b1": cmp_v_b1[l], "cmp_v_w2": cmp_v_w2[l],
            "q_norm_g": q_norm_g[l], "k_cmp_norm_g": k_cmp_norm_g[l], "k_sel_norm_g": k_sel_norm_g[l],
            "k_win_norm_g": k_win_norm_g[l], "ret_norm_g": ret_norm_g[l],
            "w_nsa_br": w_nsa_br[l], "w_ret_br": w_ret_br[l], "w_out": w_out[l], "ln2_g": ln2_g[l],
            "router_w": router_w[l], "router_b": router_b[l], "w_gate_up": w_gate_up[l],
            "b_gate_up": b_gate_up[l], "w_down": w_down[l], "b_down": b_down[l],
        }
        pools = (cache_k_cmp[l], cache_v_cmp[l], cache_k_sel[l], cache_v_sel[l])
        y_p, y_s, sp, ss = _layer(y_p, y_s, pools, state_k_win[l], state_v_win[l], state_ret[l], page_table, p)
        new_p.append(sp)
        new_s.append(ss)
    outs_p = [jnp.stack(a) for a in zip(*new_p)]
    outs_s = [jnp.stack(a) for a in zip(*new_s)]
    return (y_p, y_s, *outs_p, *outs_s)
```

```python
import functools
import math

import jax
import jax.numpy as jnp
from jax import lax
from jax.experimental import pallas as pl
from jax.experimental.pallas import tpu as pltpu

F32 = jnp.float32
BF16 = jnp.bfloat16

D_MODEL = 2048
PAGE_SIZE = 128
NSA_HEADS = 8
NSA_GROUPS = 2
HPG = NSA_HEADS // NSA_GROUPS
HEAD_DIM = 128
CMP_LEN = 32
CMP_STRIDE = 16
CMP_HIDDEN = 2 * HEAD_DIM
SEL_BLK = 64
SEL_SHIFT = 6
SEL_TOPK = 16
WINDOW = 512
Q_BLK = 128
RET_HEADS = 8
RET_DK = 128
RET_DV = 256
RET_CHUNK = 128
ROPE_BASE = 10000.0
N_EXPERTS = 32
TOP_K = 4
D_FF = D_MODEL
SWIGLU_LIMIT = 7.0
SWIGLU_ALPHA = 1.702
EPS = 1e-6
NEG = -1e30
BIG = 1e30

NSA_Q = NSA_HEADS * HEAD_DIM
KV_W = NSA_GROUPS * HEAD_DIM
RET_QK = RET_HEADS * RET_DK
RET_V = RET_HEADS * RET_DV

LANES = 128
ROW_TILE = 512
SEL_KV_TILE = 512
MOE_TILE = 512
STAGE_PITCH = 24
ROW_PITCH = 17
MOE_FF_TILE = 512
MOE_DOWN_TILE = 1024
MOE_CACHE_TILES = 5
VMEM_LIMIT = 56 * 1024 * 1024


def _cparams(*sem):
    return pltpu.CompilerParams(dimension_semantics=sem, vmem_limit_bytes=VMEM_LIMIT)


def _dot(a, b):
    return jnp.dot(a, b, preferred_element_type=F32)


def _dot_nt(a, b):
    return lax.dot_general(a, b, (((1,), (1,)), ((), ())), preferred_element_type=F32)


def _dot_tn(a, b):
    return lax.dot_general(a, b, (((0,), (0,)), ((), ())), preferred_element_type=F32)


def _sigmoid(x):
    return 1.0 / (1.0 + jnp.exp(-x))


def _unit_rms(x):
    return x * lax.rsqrt(jnp.mean(x * x, axis=-1, keepdims=True) + EPS)


def _iota(shape, dim):
    return lax.broadcasted_iota(jnp.int32, shape, dim)


def _split_bf16(x):
    hi = x.astype(BF16)
    lo = (x - hi.astype(F32)).astype(BF16)
    return hi, lo


def _two_group_specs(tm, d, tiles_p):
    return [pl.BlockSpec((tm, d), lambda i: (jnp.minimum(i, tiles_p - 1), 0)),
            pl.BlockSpec((tm, d), lambda i: (jnp.maximum(i - tiles_p, 0), 0))]


def _ln1_body(xp_ref, xs_ref, g_ref, o_ref, *, tiles_p):
    def norm(x_ref):
        o_ref[...] = (_unit_rms(x_ref[...]) * g_ref[...]).astype(o_ref.dtype)

    pl.when(pl.program_id(0) < tiles_p)(lambda: norm(xp_ref))
    pl.when(pl.program_id(0) >= tiles_p)(lambda: norm(xs_ref))


def _ln1(xp, xs, g):
    d = xp.shape[1]
    tiles_p = xp.shape[0] // ROW_TILE
    t = xp.shape[0] + xs.shape[0]
    return pl.pallas_call(
        functools.partial(_ln1_body, tiles_p=tiles_p),
        grid=(t // ROW_TILE,),
        in_specs=_two_group_specs(ROW_TILE, d, tiles_p) + [pl.BlockSpec((1, d), lambda i: (0, 0))],
        out_specs=pl.BlockSpec((ROW_TILE, d), lambda i: (i, 0)),
        out_shape=jax.ShapeDtypeStruct((t, d), BF16),
        compiler_params=_cparams("arbitrary"),
        name="ln1",
    )(xp, xs, g.reshape(1, d))


def _proj_q_body(x_ref, w_ref, g_ref, o_ref):
    y = _dot(x_ref[...], w_ref[...])
    g = g_ref[...] * (HEAD_DIM ** -0.5)
    for c in range(NSA_HEADS):
        sl = slice(c * LANES, (c + 1) * LANES)
        o_ref[:, sl] = (_unit_rms(y[:, sl]) * g).astype(o_ref.dtype)


def _proj_kv_body(x_ref, w_ref, gs_ref, gw_ref, kc_ref, vc_ref, ks_ref, vs_ref, kw_ref, vw_ref, kvb_ref, gate_ref):
    y = _dot(x_ref[...], w_ref[...])
    tm = y.shape[0]
    outs = (kc_ref, vc_ref, ks_ref, vs_ref, kw_ref, vw_ref)
    for c in range(12):
        sl = slice(c * LANES, (c + 1) * LANES)
        yc = y[:, sl]
        if c in (4, 5):
            yc = _unit_rms(yc) * gs_ref[...]
        elif c in (8, 9):
            yc = _unit_rms(yc) * gw_ref[...]
        outs[c // NSA_GROUPS][pl.ds(c % NSA_GROUPS, tm, stride=NSA_GROUPS), :] = yc
        if c >= 4:
            kvb_ref[:, (c - 4) * LANES:(c - 3) * LANES] = yc.astype(BF16)
    gate_ref[...] = _sigmoid(y[:, 12 * LANES:13 * LANES])


def _proj_rot_body(x_ref, w_ref, inv_ref, o_ref, *, n_prompt, seq, past, dec_seq):
    j = pl.program_id(0)
    i = pl.program_id(1)
    y = _dot(x_ref[...], w_ref[...])
    tm = y.shape[0]
    row = i * tm + _iota((tm, LANES), 0)
    pos = jnp.where(row < n_prompt, row & (seq - 1), past + ((row - n_prompt) & (dec_seq - 1)))
    ang = pos.astype(F32) * inv_ref[...]
    cos = jnp.cos(ang)
    sin = jnp.sin(ang)
    even = (_iota((tm, LANES), 1) & 1) == 0
    sin = jnp.where(even, -sin, sin)
    scale = jnp.where(j == 0, 1.0, RET_DK ** -0.5).astype(F32)
    for c in range(RET_HEADS):
        sl = slice(c * LANES, (c + 1) * LANES)
        yc = y[:, sl]
        partner = jnp.where(even, pltpu.roll(yc, LANES - 1, 1), pltpu.roll(yc, 1, 1))
        o_ref[:, sl] = ((yc * cos + partner * sin) * scale).astype(o_ref.dtype)


def _proj_act_body(x_ref, w_ref, o_ref, *, act):
    y = _dot(x_ref[...], w_ref[...])
    if act == "silu":
        y = y * _sigmoid(y)
    elif act == "sigmoid":
        y = _sigmoid(y)
    o_ref[...] = y.astype(o_ref.dtype)


def _proj_call(body, xn, w, extra, extra_specs, out_shape, out_specs, tn, name):
    t, k = xn.shape
    n = w.shape[1]
    in_specs = [pl.BlockSpec((ROW_TILE, k), lambda j, i: (i, 0)),
                pl.BlockSpec((k, tn), lambda j, i: (0, j))] + extra_specs
    return pl.pallas_call(
        body,
        grid=(n // tn, t // ROW_TILE),
        in_specs=in_specs,
        out_specs=out_specs,
        out_shape=out_shape,
        compiler_params=_cparams("parallel", "parallel"),
        name=name,
    )(xn, w, *extra)


def _vec_spec(n):
    return pl.BlockSpec((1, n), lambda j, i: (0, 0))


def _compress_mlp(x_cat, w1_ref, b1_ref, w2_ref):
    nh = x_cat.shape[0]
    a = _dot(x_cat, w1_ref[...])
    pre = a[:, :CMP_HIDDEN] + pltpu.roll(a[:, CMP_HIDDEN:], nh - 1, 0) + b1_ref[...]
    hid = pre * (0.5 * (1.0 + jnp.tanh(math.sqrt(2.0 / math.pi) * (pre + 0.044715 * (pre * pre * pre)))))
    out = _dot(hid.astype(BF16), w2_ref[...])
    return jnp.where(_iota(out.shape, 0) < nh - 1, out, 0.0)


def _compress_finish(xk, xv, wrefs, kc_ref, vc_ref):
    w1k, b1k, w2k, w1v, b1v, w2v, gk = wrefs
    for g in range(NSA_GROUPS):
        sl = slice(g * LANES, (g + 1) * LANES)
        kc = _compress_mlp(xk[g], w1k, b1k, w2k)
        kc_ref[:, sl] = (_unit_rms(kc) * gk[...]).astype(kc_ref.dtype)
        vc_ref[:, sl] = _compress_mlp(xv[g], w1v, b1v, w2v).astype(vc_ref.dtype)


def _compress_prompt_body(krows_ref, vrows_ref, *refs, nh):
    wrefs, (kc_ref, vc_ref) = refs[:7], refs[7:]

    def gather(rows_ref, g):
        return jnp.concatenate(
            [rows_ref[pl.ds(NSA_GROUPS * s + g, nh, stride=NSA_GROUPS * CMP_STRIDE), :].astype(BF16)
             for s in range(CMP_STRIDE)], axis=1)

    xk = [gather(krows_ref, g) for g in range(NSA_GROUPS)]
    xv = [gather(vrows_ref, g) for g in range(NSA_GROUPS)]
    _compress_finish(xk, xv, wrefs, kc_ref, vc_ref)


def _compress_sample_body(pt_ref, *refs, n_pages):
    del pt_ref
    kpages, vpages = refs[:n_pages], refs[n_pages:2 * n_pages]
    wrefs = refs[2 * n_pages:2 * n_pages + 7]
    kc_ref, vc_ref, stage_ref = refs[2 * n_pages + 7:]
    per_page = PAGE_SIZE // CMP_STRIDE
    nh = n_pages * per_page

    def gather(pages, g, slab):
        stage = stage_ref.at[slab]
        for p, page in enumerate(pages):
            for m in range(per_page):
                half = page[pl.ds(NSA_GROUPS * CMP_STRIDE * m + g, CMP_STRIDE, stride=NSA_GROUPS), :]
                stage[pl.ds((p * per_page + m) * STAGE_PITCH, CMP_STRIDE), :] = half
        return jnp.concatenate(
            [stage[pl.ds(s, nh, stride=STAGE_PITCH), :].astype(BF16) for s in range(CMP_STRIDE)], axis=1)

    xk = [gather(kpages, g, g) for g in range(NSA_GROUPS)]
    xv = [gather(vpages, g, NSA_GROUPS + g) for g in range(NSA_GROUPS)]
    _compress_finish(xk, xv, wrefs, kc_ref, vc_ref)


def _const_spec(shape, nargs):
    zeros = (0,) * len(shape)
    return pl.BlockSpec(shape, lambda *a: zeros)


def _compress_weights(w1, b1, w2):
    r_n = CMP_LEN // CMP_STRIDE
    w1r = w1.reshape(r_n, CMP_STRIDE * HEAD_DIM, CMP_HIDDEN)
    w1cat = jnp.concatenate([w1r[r] for r in range(r_n)], axis=1).astype(BF16)
    return w1cat, b1.reshape(1, CMP_HIDDEN), w2.astype(BF16)


def _compress_weight_specs():
    k16 = CMP_STRIDE * HEAD_DIM
    one = [_const_spec((k16, 2 * CMP_HIDDEN), 0), _const_spec((1, CMP_HIDDEN), 0),
           _const_spec((CMP_HIDDEN, HEAD_DIM), 0)]
    return one + one + [_const_spec((1, HEAD_DIM), 0)]


def _compress_prompt(krows, vrows, weights, batch, seq):
    nh = seq // CMP_STRIDE
    out = jax.ShapeDtypeStruct((batch, nh, KV_W), BF16)
    ospec = pl.BlockSpec((None, nh, KV_W), lambda b: (b, 0, 0))
    return pl.pallas_call(
        functools.partial(_compress_prompt_body, nh=nh),
        grid=(batch,),
        in_specs=[pl.BlockSpec((NSA_GROUPS * seq, LANES), lambda b: (b, 0))] * 2 + _compress_weight_specs(),
        out_specs=[ospec, ospec],
        out_shape=[out, out],
        compiler_params=_cparams("parallel"),
        name="compress_prompt",
    )(krows, vrows, *weights)


def _page_specs(n_pages, nb=1, bb=0):
    def spec(p):
        return pl.BlockSpec((None, NSA_GROUPS * PAGE_SIZE, HEAD_DIM),
                            lambda b, pt: (pt[(b * nb + bb) * n_pages + p], 0, 0))
    return [spec(p) for p in range(n_pages)]


def _compress_sample(pool_k, pool_v, pt_flat, weights, dec_batch, n_pages):
    nh = n_pages * PAGE_SIZE // CMP_STRIDE
    out = jax.ShapeDtypeStruct((dec_batch, nh, KV_W), BF16)
    ospec = pl.BlockSpec((None, nh, KV_W), lambda b, pt: (b, 0, 0))
    grid_spec = pltpu.PrefetchScalarGridSpec(
        num_scalar_prefetch=1,
        grid=(dec_batch,),
        in_specs=_page_specs(n_pages) + _page_specs(n_pages) + _compress_weight_specs(),
        out_specs=[ospec, ospec],
        scratch_shapes=[pltpu.VMEM((2 * NSA_GROUPS, nh * STAGE_PITCH, HEAD_DIM), F32)],
    )
    return pl.pallas_call(
        functools.partial(_compress_sample_body, n_pages=n_pages),
        grid_spec=grid_spec,
        out_shape=[out, out],
        compiler_params=_cparams("parallel"),
        name="compress_sample",
    )(pt_flat, *([pool_k] * n_pages), *([pool_v] * n_pages), *weights)


def _masked_softmax(s3, ok):
    s3 = jnp.where(ok[None], s3, NEG)
    m = jnp.max(s3, axis=-1, keepdims=True)
    e = jnp.where(ok[None], jnp.exp(s3 - m), 0.0)
    den = jnp.sum(e, axis=-1, keepdims=True)
    return e / jnp.where(den > 0.0, den, 1.0)


def _topk_lanes(score, n_sel, k):
    lane = _iota(score.shape, 1)
    sc = jnp.where(lane < n_sel, score, -jnp.inf)
    rank = jnp.zeros(score.shape, F32)
    for i in range(n_sel):
        ci = sc[:, i:i + 1]
        later = jnp.where(lane > i, 1.0, 0.0)
        rank = rank + jnp.where(ci > sc, 1.0, jnp.where(ci == sc, later, 0.0))
    return jnp.where((rank < k) & (lane < n_sel), 1.0, 0.0)


def _bias_softmax_pv(s3, ok, v):
    h, t, n = s3.shape
    s3 = s3 + jnp.where(ok, 0.0, NEG)[None]
    e = jnp.exp(s3 - jnp.max(s3, axis=-1, keepdims=True))
    inv = 1.0 / jnp.sum(e, axis=-1, keepdims=True)
    return _dot(e.reshape(h * t, n).astype(BF16), v) * inv.reshape(h * t, 1)


def _compressed_branch(qg, kcg, vcg, qpos, n_cmp):
    t = qpos.shape[0]
    nc_pad = kcg.shape[0]
    c_idx = _iota((t, nc_pad), 1)
    vis = (c_idx * CMP_STRIDE + (CMP_LEN - 1) <= qpos) & (c_idx < n_cmp)
    p3 = _masked_softmax(_dot_nt(qg, kcg).reshape(HPG, t, nc_pad), vis)
    o_c = _dot(p3.reshape(HPG * t, nc_pad).astype(BF16), vcg)
    return o_c, jnp.sum(p3, axis=0)


def _overlap(nc_pad, blocks_first):
    shape = (LANES, nc_pad) if blocks_first else (nc_pad, LANES)
    ci = _iota(shape, 1 if blocks_first else 0) * CMP_STRIDE
    sj = _iota(shape, 0 if blocks_first else 1) * SEL_BLK
    return jnp.where((ci < sj + SEL_BLK) & (ci + CMP_LEN > sj), 1.0, 0.0).astype(BF16)


def _select_lanes(psum, qpos, n_sel):
    t, nc_pad = psum.shape
    hi, lo = _split_bf16(psum)
    ov = _overlap(nc_pad, False)
    score = _dot(hi, ov) + _dot(lo, ov)
    j = _iota((t, LANES), 1)
    cur = qpos >> SEL_SHIFT
    forced = (j == 0) | (j == cur) | (j == cur - 1)
    score = jnp.where(forced, BIG, score)
    score = jnp.where(j * SEL_BLK <= qpos, score, NEG)
    return _topk_lanes(score, n_sel, min(SEL_TOPK, n_sel))


def _select_sublanes(psum, start, n_sel):
    t, nc_pad = psum.shape
    hi, lo = _split_bf16(psum)
    ov = _overlap(nc_pad, True)
    rows = -(-n_sel // 8) * 8
    sc = (_dot_nt(ov, hi) + _dot_nt(ov, lo))[:rows]
    j = _iota((rows, t), 0)
    qp = start + _iota((rows, t), 1)
    cur = qp >> SEL_SHIFT
    forced = (j == 0) | (j == cur) | (j == cur - 1)
    sc = jnp.where(forced, BIG, sc)
    sc = jnp.where(j * SEL_BLK <= qp, sc, NEG)
    sc = jnp.where(j < n_sel, sc, -jnp.inf)
    rank = jnp.zeros((rows, t), F32)
    for i in range(n_sel):
        ri = sc[i:i + 1, :]
        later = jnp.where(j > i, 1.0, 0.0)
        rank = rank + jnp.where(ri > sc, 1.0, jnp.where(ri == sc, later, 0.0))
    sel = jnp.where((rank < min(SEL_TOPK, n_sel)) & (j < n_sel), 1.0, 0.0).astype(BF16)
    if rows < LANES:
        sel = jnp.concatenate([sel, jnp.zeros((LANES - rows, t), BF16)], axis=0)
    return sel


def _block_to_key(blk0, n):
    blk = blk0 + (_iota((LANES, n), 1) >> SEL_SHIFT)
    return jnp.where(_iota((LANES, n), 0) == blk, 1.0, 0.0).astype(BF16)


def _nsa_prompt_body(q_ref, gate_ref, kc_ref, vc_ref, kv_ref, o_ref, *, seq):
    tq = Q_BLK
    start = pl.program_id(1) * tq
    n_sel = seq // SEL_BLK
    n_cmp = seq // CMP_STRIDE - CMP_LEN // CMP_STRIDE + 1
    qpos = start + _iota((tq, 1), 0)
    gates = gate_ref[...]
    wk = WINDOW + tq
    for g in range(NSA_GROUPS):
        gsl = slice(g * LANES, (g + 1) * LANES)
        qg = jnp.concatenate(
            [q_ref[:, (g * HPG + j) * LANES:(g * HPG + j + 1) * LANES] for j in range(HPG)], axis=0)
        o_c, psum = _compressed_branch(qg, kc_ref[:, gsl], vc_ref[:, gsl], qpos, n_cmp)
        sel_t = _select_sublanes(psum, start, n_sel)

        def sel_step(k, carry):
            m, l, acc = carry
            k0 = pl.multiple_of(k * SEL_KV_TILE, SEL_KV_TILE)
            kt = kv_ref[pl.ds(k0, SEL_KV_TILE), pl.ds(g * LANES, LANES)]
            vt = kv_ref[pl.ds(k0, SEL_KV_TILE), pl.ds(KV_W + g * LANES, LANES)]
            chosen = _dot_tn(sel_t, _block_to_key(k * (SEL_KV_TILE // SEL_BLK), SEL_KV_TILE))
            ok = (chosen > 0.5) & (k0 + _iota((tq, SEL_KV_TILE), 1) <= qpos)
            s3 = _dot_nt(qg, kt).reshape(HPG, tq, SEL_KV_TILE) + jnp.where(ok, 0.0, NEG)[None]
            m_new = jnp.maximum(m, jnp.max(s3, axis=-1, keepdims=True))
            alpha = jnp.exp(m - m_new)
            p = jnp.exp(s3 - m_new)
            l = alpha * l + jnp.sum(p, axis=-1, keepdims=True)
            pv = _dot(p.reshape(HPG * tq, SEL_KV_TILE).astype(BF16), vt)
            return m_new, l, alpha * acc + pv.reshape(HPG, tq, LANES)

        n_kv = (start + tq + SEL_KV_TILE - 1) // SEL_KV_TILE
        init = (jnp.full((HPG, tq, 1), NEG, F32), jnp.zeros((HPG, tq, 1), F32),
                jnp.zeros((HPG, tq, LANES), F32))
        _, l_s, acc_s = lax.fori_loop(0, n_kv, sel_step, init)
        o_s = acc_s * (1.0 / l_s)

        w0 = pl.multiple_of(jnp.maximum(start - WINDOW, 0), Q_BLK)
        kw = kv_ref[pl.ds(w0, wk), pl.ds(2 * KV_W + g * LANES, LANES)]
        vw = kv_ref[pl.ds(w0, wk), pl.ds(3 * KV_W + g * LANES, LANES)]
        dpos = qpos - (w0 + _iota((tq, wk), 1))
        o_w = _bias_softmax_pv(_dot_nt(qg, kw).reshape(HPG, tq, wk), (dpos >= 0) & (dpos < WINDOW), vw)

        for j in range(HPG):
            h = g * HPG + j
            rows = slice(j * tq, (j + 1) * tq)
            o = (o_c[rows] * gates[:, 3 * h:3 * h + 1] + o_s[j] * gates[:, 3 * h + 1:3 * h + 2]
                 + o_w[rows] * gates[:, 3 * h + 2:3 * h + 3])
            o_ref[:, h * LANES:(h + 1) * LANES] = o.astype(o_ref.dtype)


def _nsa_prompt(q, gate, kc, vc, kvb, batch, seq):
    nq = seq // Q_BLK
    nh = seq // CMP_STRIDE
    return pl.pallas_call(
        functools.partial(_nsa_prompt_body, seq=seq),
        grid=(batch, nq),
        in_specs=[pl.BlockSpec((Q_BLK, NSA_Q), lambda b, i: (b * nq + i, 0)),
                  pl.BlockSpec((Q_BLK, LANES), lambda b, i: (b * nq + i, 0)),
                  pl.BlockSpec((None, nh, KV_W), lambda b, i: (b, 0, 0)),
                  pl.BlockSpec((None, nh, KV_W), lambda b, i: (b, 0, 0)),
                  pl.BlockSpec((seq, 4 * KV_W), lambda b, i: (b, 0))],
        out_specs=pl.BlockSpec((Q_BLK, NSA_Q), lambda b, i: (b * nq + i, 0)),
        out_shape=jax.ShapeDtypeStruct((batch * seq, NSA_Q), BF16),
        compiler_params=_cparams("parallel", "parallel"),
        name="nsa_prompt",
    )(q, gate, kc, vc, kvb)


def _nsa_sample_one(q, gates, new_refs, kc_ref, vc_ref, kpages, vpages, kwbuf_ref, vwbuf_ref,
                    kwout_ref, vwout_ref, past):
    t = q.shape[0]
    wb = kwbuf_ref.shape[0] // NSA_GROUPS
    n_past = past // SEL_BLK
    n_sel = n_past + -(-t // SEL_BLK)
    n_cmp = (past + t) // CMP_STRIDE - CMP_LEN // CMP_STRIDE + 1
    qpos = past + _iota((t, 1), 0)
    pad = jnp.zeros((LANES - t, LANES), BF16)
    n_keys = past + LANES
    wk = wb + LANES
    heads = []
    for g in range(NSA_GROUPS):
        gsl = slice(g * LANES, (g + 1) * LANES)

        def new_rows(which):
            return new_refs[which][pl.ds(g, t, stride=NSA_GROUPS), :]

        qg = jnp.concatenate(
            [q[:, (g * HPG + j) * LANES:(g * HPG + j + 1) * LANES] for j in range(HPG)], axis=0).astype(BF16)
        o_c, psum = _compressed_branch(qg, kc_ref[:, gsl], vc_ref[:, gsl], qpos, n_cmp)
        sel = _select_lanes(psum, qpos, n_sel)

        def keys(pages, which):
            past_rows = [p[pl.ds(g, PAGE_SIZE, stride=NSA_GROUPS), :].astype(BF16) for p in pages]
            return jnp.concatenate(past_rows + [new_rows(which).astype(BF16), pad], axis=0)

        ks, vs = keys(kpages, 2), keys(vpages, 3)
        chosen = _dot(sel.astype(BF16), _block_to_key(0, n_keys))
        ok = (chosen > 0.5) & (_iota((t, n_keys), 1) <= qpos)
        o_s = _bias_softmax_pv(_dot_nt(qg, ks).reshape(HPG, t, n_keys), ok, vs)

        def window(buf_ref, which):
            return jnp.concatenate([buf_ref[pl.ds(g, wb, stride=NSA_GROUPS), :].astype(BF16),
                                    new_rows(which).astype(BF16), pad], axis=0)

        kw, vw = window(kwbuf_ref, 4), window(vwbuf_ref, 5)
        dpos = qpos - (past - wb + _iota((t, wk), 1))
        o_w = _bias_softmax_pv(_dot_nt(qg, kw).reshape(HPG, t, wk), (dpos >= 0) & (dpos < WINDOW), vw)

        for j in range(HPG):
            h = g * HPG + j
            rows = slice(j * t, (j + 1) * t)
            heads.append(o_c[rows] * gates[:, 3 * h:3 * h + 1] + o_s[rows] * gates[:, 3 * h + 1:3 * h + 2]
                         + o_w[rows] * gates[:, 3 * h + 2:3 * h + 3])

        kwout_ref[pl.ds((wb - t) * NSA_GROUPS + g, t, stride=NSA_GROUPS), :] = new_rows(4)
        vwout_ref[pl.ds((wb - t) * NSA_GROUPS + g, t, stride=NSA_GROUPS), :] = new_rows(5)
    keep = (wb - t) * NSA_GROUPS
    kwout_ref[pl.ds(0, keep), :] = kwbuf_ref[pl.ds(t * NSA_GROUPS, keep), :]
    vwout_ref[pl.ds(0, keep), :] = vwbuf_ref[pl.ds(t * NSA_GROUPS, keep), :]
    return heads


def _nsa_sample_body(pt_ref, q_ref, gate_ref, ksn_ref, vsn_ref, kwn_ref, vwn_ref, kc_ref, vc_ref, *refs,
                     n_pages, past, nb):
    del pt_ref
    kpages, vpages = refs[:nb * n_pages], refs[nb * n_pages:2 * nb * n_pages]
    kwbuf_ref, vwbuf_ref, o_ref, kwout_ref, vwout_ref = refs[2 * nb * n_pages:]
    t = q_ref.shape[0] // nb
    q = q_ref[...].astype(F32)
    gates = gate_ref[...]
    per_item = []
    for bb in range(nb):
        tok = slice(bb * t, (bb + 1) * t)
        tok2 = pl.ds(bb * NSA_GROUPS * t, NSA_GROUPS * t)
        new_refs = {2: ksn_ref.at[tok2], 3: vsn_ref.at[tok2], 4: kwn_ref.at[tok2], 5: vwn_ref.at[tok2]}
        pages = slice(bb * n_pages, (bb + 1) * n_pages)
        per_item.append(_nsa_sample_one(
            q[tok], gates[tok], new_refs, kc_ref.at[bb], vc_ref.at[bb], kpages[pages], vpages[pages],
            kwbuf_ref.at[bb], vwbuf_ref.at[bb], kwout_ref.at[bb], vwout_ref.at[bb], past))
    for h in range(NSA_HEADS):
        o_ref[:, h * LANES:(h + 1) * LANES] = jnp.concatenate(
            [heads[h] for heads in per_item], axis=0).astype(o_ref.dtype)


def _nsa_sample(q, gate, new, row_block0, kc, vc, pool_k, pool_v, kwbuf, vwbuf, pt_flat, dec_batch, t, n_pages,
                nb=2):
    past = n_pages * PAGE_SIZE
    nh = past // CMP_STRIDE
    wrows = kwbuf.shape[1]

    def rows(r, n):
        return pl.BlockSpec((nb * r, n), lambda b, pt: (row_block0 + b, 0))

    def per_b(r, c):
        return pl.BlockSpec((nb, r, c), lambda b, pt: (b, 0, 0))

    pages = [spec for bb in range(nb) for spec in _page_specs(n_pages, nb, bb)]
    grid_spec = pltpu.PrefetchScalarGridSpec(
        num_scalar_prefetch=1,
        grid=(dec_batch // nb,),
        in_specs=[rows(t, NSA_Q), rows(t, LANES)] + [rows(NSA_GROUPS * t, HEAD_DIM)] * 4
        + [per_b(nh, KV_W), per_b(nh, KV_W)] + pages + pages + [per_b(wrows, HEAD_DIM), per_b(wrows, HEAD_DIM)],
        out_specs=[pl.BlockSpec((nb * t, NSA_Q), lambda b, pt: (b, 0)), per_b(wrows, HEAD_DIM),
                   per_b(wrows, HEAD_DIM)],
    )
    wout = jax.ShapeDtypeStruct(kwbuf.shape, F32)
    pools = [pool_k] * (nb * n_pages) + [pool_v] * (nb * n_pages)
    return pl.pallas_call(
        functools.partial(_nsa_sample_body, n_pages=n_pages, past=past, nb=nb),
        grid_spec=grid_spec,
        out_shape=[jax.ShapeDtypeStruct((dec_batch * t, NSA_Q), BF16), wout, wout],
        compiler_params=_cparams("parallel"),
        name="nsa_sample",
    )(pt_flat, q, gate, *new, kc, vc, *pools, kwbuf, vwbuf)


def _log_decay(h):
    return math.log(1.0 - 2.0 ** (-5.0 - h))


def _ret_finish(o, h, gn_ref, rg):
    sl = slice(h * RET_DV, (h + 1) * RET_DV)
    return _unit_rms(o) * gn_ref[:, sl] * rg[:, sl]


def _ret_prompt_body(rq_ref, rk_ref, rv_ref, rg_ref, gn_ref, o_ref, s_ref):
    c = RET_CHUNK

    @pl.when(pl.program_id(1) == 0)
    def _():
        s_ref[...] = jnp.zeros(s_ref.shape, F32)

    n_col = _iota((c, 1), 0).astype(F32)
    diff = (_iota((c, c), 0) - _iota((c, c), 1)).astype(F32)
    rg = rg_ref[...].astype(F32)
    for h in range(RET_HEADS):
        lg = _log_decay(h)
        q = rq_ref[:, h * RET_DK:(h + 1) * RET_DK]
        k = rk_ref[:, h * RET_DK:(h + 1) * RET_DK]
        v = rv_ref[:, h * RET_DV:(h + 1) * RET_DV]
        s_prev = s_ref[h]
        dmask = jnp.where(diff >= 0.0, jnp.exp(jnp.maximum(diff, 0.0) * lg), 0.0)
        o = _dot((_dot_nt(q, k) * dmask).astype(BF16), v)
        q_dec = (q.astype(F32) * jnp.exp((n_col + 1.0) * lg)).astype(BF16)
        o = o + _dot(q_dec, s_prev.astype(BF16))
        k_dec = (k.astype(F32) * jnp.exp((c - 1.0 - n_col) * lg)).astype(BF16)
        s_ref[h] = math.exp(c * lg) * s_prev + _dot_tn(k_dec, v)
        o_ref[:, h * RET_DV:(h + 1) * RET_DV] = _ret_finish(o, h, gn_ref, rg).astype(o_ref.dtype)


def _ret_prompt(rot, rv, rg, gn, batch, seq):
    nch = seq // RET_CHUNK
    qk_w = RET_QK

    def rows(n, col):
        return pl.BlockSpec((RET_CHUNK, n), lambda b, i: (b * nch + i, col))

    return pl.pallas_call(
        _ret_prompt_body,
        grid=(batch, nch),
        in_specs=[rows(qk_w, 0), rows(qk_w, 1), rows(RET_V, 0), rows(RET_V, 0),
                  pl.BlockSpec((1, RET_V), lambda b, i: (0, 0))],
        out_specs=[rows(RET_V, 0),
                   pl.BlockSpec((None, RET_HEADS, RET_DK, RET_DV), lambda b, i: (b, 0, 0, 0))],
        out_shape=[jax.ShapeDtypeStruct((batch * seq, RET_V), BF16),
                   jax.ShapeDtypeStruct((batch, RET_HEADS, RET_DK, RET_DV), F32)],
        compiler_params=_cparams("parallel", "arbitrary"),
        name="retention_prompt",
    )(rot, rot, rv, rg, gn)


def _ret_sample_body(rq_ref, rk_ref, rv_ref, rg_ref, gn_ref, s_ref, o_ref, so_ref):
    c = rq_ref.shape[0]
    n_col = _iota((c, 1), 0).astype(F32)
    rg = rg_ref[...]
    zk = jnp.zeros((LANES - c, RET_DK), BF16)
    zv = jnp.zeros((LANES - c, RET_DV), BF16)
    for h in range(RET_HEADS):
        lg = _log_decay(h)
        q = rq_ref[:, h * RET_DK:(h + 1) * RET_DK]
        k = rk_ref[:, h * RET_DK:(h + 1) * RET_DK]
        v = rv_ref[:, h * RET_DV:(h + 1) * RET_DV]
        s_prev = s_ref[h]
        o = _dot((q * jnp.exp((n_col + 1.0) * lg)).astype(BF16), s_prev.astype(BF16))
        for j in range(c):
            qk = jnp.sum(q * k[j:j + 1, :], axis=-1, keepdims=True)
            dj = jnp.where(n_col >= j, jnp.exp(jnp.maximum(n_col - j, 0.0) * lg), 0.0)
            o = o + (qk * dj) * v[j:j + 1, :]
        k_dec = jnp.concatenate([(k * jnp.exp((c - 1.0 - n_col) * lg)).astype(BF16), zk], axis=0)
        v_pad = jnp.concatenate([v.astype(BF16), zv], axis=0)
        so_ref[h] = math.exp(c * lg) * s_prev + _dot_tn(k_dec, v_pad)
        o_ref[:, h * RET_DV:(h + 1) * RET_DV] = _ret_finish(o, h, gn_ref, rg)


def _ret_sample(rq, rk, rv, rg, gn, state, dec_batch, t):
    def rows(n):
        return pl.BlockSpec((t, n), lambda b: (b, 0))

    sspec = pl.BlockSpec((None, RET_HEADS, RET_DK, RET_DV), lambda b: (b, 0, 0, 0))
    return pl.pallas_call(
        _ret_sample_body,
        grid=(dec_batch,),
        in_specs=[rows(RET_QK), rows(RET_QK), rows(RET_V), rows(RET_V),
                  pl.BlockSpec((1, RET_V), lambda b: (0, 0)), sspec],
        out_specs=[rows(RET_V), sspec],
        out_shape=[jax.ShapeDtypeStruct((dec_batch * t, RET_V), F32),
                   jax.ShapeDtypeStruct(state.shape, F32)],
        compiler_params=_cparams("parallel"),
        name="retention_sample",
    )(rq, rk, rv, rg, gn, state)


def _merge_body(on_ref, or_ref, wn_ref, wr_ref, ga_ref, gr_ref, o_ref):
    a = _dot(on_ref[...], wn_ref[...])
    r = _dot(or_ref[...], wr_ref[...])
    o_ref[...] = (ga_ref[...].astype(F32) * a + gr_ref[...].astype(F32) * r).astype(o_ref.dtype)


def _merge(o_nsa, o_ret, wn, wr, gates, tn=1024):
    t = o_nsa.shape[0]
    d = wn.shape[1]
    ga0 = 0
    gr0 = d // tn
    return pl.pallas_call(
        _merge_body,
        grid=(d // tn, t // ROW_TILE),
        in_specs=[pl.BlockSpec((ROW_TILE, NSA_Q), lambda j, i: (i, 0)),
                  pl.BlockSpec((ROW_TILE, RET_V), lambda j, i: (i, 0)),
                  pl.BlockSpec((NSA_Q, tn), lambda j, i: (0, j)),
                  pl.BlockSpec((RET_V, tn), lambda j, i: (0, j)),
                  pl.BlockSpec((ROW_TILE, tn), lambda j, i: (i, ga0 + j)),
                  pl.BlockSpec((ROW_TILE, tn), lambda j, i: (i, gr0 + j))],
        out_specs=pl.BlockSpec((ROW_TILE, tn), lambda j, i: (i, j)),
        out_shape=jax.ShapeDtypeStruct((t, d), BF16),
        compiler_params=_cparams("parallel", "parallel"),
        name="merge",
    )(o_nsa, o_ret, wn, wr, gates, gates)


def _out_router_body(xp_ref, xs_ref, mix_ref, wo_ref, g_ref, rwh_ref, rwl_ref, rb_ref,
                     x1_ref, h_ref, route_ref, cnt_ref, *, tiles_p):
    @pl.when(pl.program_id(0) == 0)
    def _():
        cnt_ref[...] = jnp.zeros(cnt_ref.shape, F32)

    x = jnp.where(pl.program_id(0) < tiles_p, xp_ref[...], xs_ref[...])
    x1 = x + _dot(mix_ref[...], wo_ref[...])
    x1_ref[...] = x1
    h = _unit_rms(x1) * g_ref[...]
    tm = h.shape[0]
    for cc in range(h.shape[1] // LANES):
        h_ref[_flat_idx(0, tm, cc)] = h[:, cc * LANES:(cc + 1) * LANES]
    for cc in range(h.shape[1] // LANES, ROW_PITCH):
        h_ref[_flat_idx(0, tm, cc)] = jnp.zeros((tm, LANES), F32)
    tm = h.shape[0]
    lane = _iota((tm, LANES), 1)
    hi, lo = _split_bf16(h)
    logits = _dot(hi, rwh_ref[...]) + _dot(lo, rwh_ref[...]) + _dot(hi, rwl_ref[...]) + rb_ref[...]
    work = jnp.where(lane < N_EXPERTS, logits, -jnp.inf)
    vals, idxs = [], []
    for _ in range(TOP_K):
        v = jnp.max(work, axis=-1, keepdims=True)
        ix = jnp.min(jnp.where(work == v, lane, LANES), axis=-1, keepdims=True)
        vals.append(v)
        idxs.append(ix)
        work = jnp.where(lane == ix, -jnp.inf, work)
    es = [jnp.exp(v - vals[0]) for v in vals]
    den = es[0] + es[1] + es[2] + es[3]
    hot = jnp.zeros((tm, LANES), F32)
    for ix in idxs:
        hot = hot + jnp.where(lane == ix, 1.0, 0.0)
    before = jnp.where(_iota((tm, tm), 1) < _iota((tm, tm), 0), 1.0, 0.0).astype(BF16)
    ranks = _dot(before, hot.astype(BF16)) + cnt_ref[...]
    route = jnp.zeros((tm, LANES), F32)
    for k in range(TOP_K):
        rk = jnp.sum(jnp.where(lane == idxs[k], ranks, 0.0), axis=-1, keepdims=True)
        route = route + jnp.where(lane == k, idxs[k].astype(F32), 0.0)
        route = route + jnp.where(lane == TOP_K + k, es[k] / den, 0.0)
        route = route + jnp.where(lane == 2 * TOP_K + k, rk, 0.0)
    route_ref[...] = route
    cnt_ref[...] = cnt_ref[...] + jnp.sum(hot, axis=0, keepdims=True)


def _out_router(xp, xs, mixed, wo, g2, rwh, rwl, rb, tm=256):
    d = xp.shape[1]
    t = xp.shape[0] + xs.shape[0]
    tiles_p = xp.shape[0] // tm
    rows = pl.BlockSpec((tm, d), lambda i: (i, 0))
    lanes = pl.BlockSpec((tm, LANES), lambda i: (i, 0))

    def const(r, c):
        return pl.BlockSpec((r, c), lambda i: (0, 0))

    return pl.pallas_call(
        functools.partial(_out_router_body, tiles_p=tiles_p),
        grid=(t // tm,),
        in_specs=_two_group_specs(tm, d, tiles_p)
        + [rows, const(d, d), const(1, d), const(d, LANES), const(d, LANES), const(1, LANES)],
        out_specs=[rows, pl.BlockSpec((tm * ROW_PITCH, LANES), lambda i: (i, 0)), lanes, const(1, LANES)],
        out_shape=[jax.ShapeDtypeStruct((t, d), F32), jax.ShapeDtypeStruct((t * ROW_PITCH, LANES), F32),
                   jax.ShapeDtypeStruct((t, LANES), F32), jax.ShapeDtypeStruct((1, LANES), F32)],
        compiler_params=_cparams("arbitrary"),
        name="out_router",
    )(xp, xs, mixed, wo, g2, rwh, rwl, rb)


def _row_copy(src, r_src, dst, r_dst, sem):
    return pltpu.make_async_copy(src.at[pl.ds(r_src * ROW_PITCH, ROW_PITCH), :],
                                 dst.at[pl.ds(r_dst * ROW_PITCH, ROW_PITCH), :], sem)


def _flat_idx(row0, n, c):
    return (pl.ds(row0 * ROW_PITCH + c, n, stride=ROW_PITCH), slice(None))


def _dispatch_body(start_ref, cnt_ref, pend_ref, e_ref, rank_ref, h_ref, xs_ref, zero_ref, sem, zsem):
    tm = h_ref.shape[0] // ROW_PITCH

    def slot(r, k):
        return start_ref[e_ref[r * TOP_K + k]] + rank_ref[r * TOP_K + k]

    def issue(r, _):
        for k in range(TOP_K):
            _row_copy(h_ref, r, xs_ref, slot(r, k), sem).start(priority=k % 2)
        return 0

    def drain(r, _):
        for k in range(TOP_K):
            _row_copy(h_ref, r, xs_ref, slot(r, k), sem).wait()
        return 0

    lax.fori_loop(0, tm, issue, 0)

    @pl.when(pl.program_id(0) == 0)
    def _():
        zero_ref[...] = jnp.zeros(zero_ref.shape, F32)
        zrows = zero_ref.shape[0] // ROW_PITCH
        for phase in ("start", "wait"):
            def per_expert(e, _):
                def per_row(s, _):
                    cp = _row_copy(zero_ref, 0, xs_ref, s, zsem)
                    cp.start() if phase == "start" else cp.wait()
                    return 0
                return lax.fori_loop(start_ref[e] + cnt_ref[e], pend_ref[e], per_row, 0)
            lax.fori_loop(0, N_EXPERTS, per_expert, 0)

            def per_chunk(s, _):
                s0 = pl.multiple_of(s * (zrows * ROW_PITCH), zrows * ROW_PITCH)
                cp = pltpu.make_async_copy(zero_ref, xs_ref.at[pl.ds(s0, zrows * ROW_PITCH), :], zsem)
                cp.start() if phase == "start" else cp.wait()
                return 0
            lax.fori_loop(pend_ref[N_EXPERTS - 1] // zrows, xs_ref.shape[0] // (zrows * ROW_PITCH), per_chunk, 0)

    lax.fori_loop(0, tm, drain, 0)


def _dispatch(h_flat, e_flat, rank_flat, pad_start, counts, pad_end, n_slots, tm=256):
    t = h_flat.shape[0] // ROW_PITCH
    smem = pl.BlockSpec((tm * TOP_K,), lambda i, *_: (i,), memory_space=pltpu.SMEM)
    grid_spec = pltpu.PrefetchScalarGridSpec(
        num_scalar_prefetch=3,
        grid=(t // tm,),
        in_specs=[smem, smem, pl.BlockSpec((tm * ROW_PITCH, LANES), lambda i, *_: (i, 0))],
        out_specs=pl.BlockSpec(memory_space=pl.ANY),
        scratch_shapes=[pltpu.VMEM((MOE_TILE // 8 * ROW_PITCH, LANES), F32), pltpu.SemaphoreType.DMA(()),
                        pltpu.SemaphoreType.DMA(())],
    )
    return pl.pallas_call(
        _dispatch_body,
        grid_spec=grid_spec,
        out_shape=jax.ShapeDtypeStruct((n_slots * ROW_PITCH, LANES), F32),
        compiler_params=pltpu.CompilerParams(dimension_semantics=("arbitrary",), vmem_limit_bytes=VMEM_LIMIT,
                                             has_side_effects=True),
        name="moe_dispatch",
    )(pad_start, counts, pad_end, e_flat, rank_flat, h_flat)


def _combine_body(start_ref, e_ref, rank_ref, x1_ref, route_ref, y_ref, op_ref, os_ref, buf_ref, sem, *, tiles_p):
    tm = x1_ref.shape[0]

    def slot(r, k):
        return start_ref[e_ref[r * TOP_K + k]] + rank_ref[r * TOP_K + k]

    def copy(r, k):
        return pltpu.make_async_copy(y_ref.at[pl.ds(slot(r, k), 1), :], buf_ref.at[k, pl.ds(r, 1), :], sem)

    def issue(r, _):
        for k in range(TOP_K):
            copy(r, k).start(priority=k % 2)
        return 0

    def drain(r, _):
        for k in range(TOP_K):
            copy(r, k).wait()
        return 0

    lax.fori_loop(0, tm, issue, 0)
    lax.fori_loop(0, tm, drain, 0)
    route = route_ref[...]
    out = x1_ref[...]
    for k in range(TOP_K):
        out = out + route[:, TOP_K + k:TOP_K + k + 1] * buf_ref[k]

    @pl.when(pl.program_id(0) < tiles_p)
    def _():
        op_ref[...] = out

    @pl.when(pl.program_id(0) >= tiles_p)
    def _():
        os_ref[...] = out


def _combine(x1, route, y, e_flat, rank_flat, pad_start, n_prompt, tm=256):
    t, d = x1.shape
    tiles_p = n_prompt // tm
    smem = pl.BlockSpec((tm * TOP_K,), lambda i, *_: (i,), memory_space=pltpu.SMEM)
    grid_spec = pltpu.PrefetchScalarGridSpec(
        num_scalar_prefetch=1,
        grid=(t // tm,),
        in_specs=[smem, smem, pl.BlockSpec((tm, d), lambda i, *_: (i, 0)),
                  pl.BlockSpec((tm, LANES), lambda i, *_: (i, 0)),
                  pl.BlockSpec(memory_space=pl.ANY)],
        out_specs=[pl.BlockSpec((tm, d), lambda i, *_: (jnp.minimum(i, tiles_p - 1), 0)),
                   pl.BlockSpec((tm, d), lambda i, *_: (jnp.maximum(i - tiles_p, 0), 0))],
        scratch_shapes=[pltpu.VMEM((TOP_K, tm, d), F32), pltpu.SemaphoreType.DMA(())],
    )
    return pl.pallas_call(
        functools.partial(_combine_body, tiles_p=tiles_p),
        grid_spec=grid_spec,
        out_shape=[jax.ShapeDtypeStruct((n_prompt, d), F32), jax.ShapeDtypeStruct((t - n_prompt, d), F32)],
        compiler_params=_cparams("arbitrary"),
        name="moe_combine",
    )(pad_start, e_flat, rank_flat, x1, route, y)


def _work_list(tiles_e, tile_start, n_tiles, n_chunks):
    ids = jnp.arange(N_EXPERTS, dtype=jnp.int32)

    def take(table, idx):
        return jnp.sum(jnp.where(idx[:, None] == ids[None, :], table[None, :], 0), axis=1)

    steps_e = tiles_e * n_chunks
    cum = jnp.cumsum(steps_e)
    total = cum[-1]
    s = jnp.arange(n_tiles * n_chunks, dtype=jnp.int32)
    sv = jnp.minimum(s, total - 1)
    e = jnp.minimum(jnp.sum((sv[:, None] >= cum[None, :]).astype(jnp.int32), axis=1), N_EXPERTS - 1)
    local = sv - (take(cum, e) - take(steps_e, e))
    per = jnp.maximum(take(tiles_e, e), 1)
    c = local // per
    r = local - c * per
    first_tile = take(tile_start, e)
    cached = per <= MOE_CACHE_TILES
    load = jnp.logical_or(c == 0, jnp.logical_not(cached))
    tile = jnp.where(load, first_tile + r, first_tile + per - 1)
    slot = jnp.where(cached, r, MOE_CACHE_TILES)
    real = s < total
    k = s - total
    out_tile = jnp.where(real, first_tile + r, total // n_chunks + k // n_chunks)
    out_chunk = jnp.where(real, c, k % n_chunks)
    flag = (real.astype(jnp.int32) + 2 * (real & (r == 0)).astype(jnp.int32)
            + 4 * (real & load).astype(jnp.int32))
    return [a.astype(jnp.int32) for a in (e, c, tile, out_tile, out_chunk, flag, slot)]


def _step_flags(pf):
    s = pl.program_id(0)
    return pf[5][s], pf[6][s]


def _expert_up_body(*refs):
    pf, (xs_ref, wg_ref, wu_ref, bg_ref, bu_ref, a_ref, wgb_ref, wub_ref, xc_ref) = refs[:7], refs[7:]
    flag, slot = _step_flags(pf)

    @pl.when((flag & 2) != 0)
    def _():
        wgb_ref[...] = wg_ref[...].astype(BF16)
        wub_ref[...] = wu_ref[...].astype(BF16)

    @pl.when((flag & 4) != 0)
    def _():
        for cc in range(xc_ref.shape[2] // LANES):
            xc_ref[slot, :, cc * LANES:(cc + 1) * LANES] = xs_ref[_flat_idx(0, MOE_TILE, cc)].astype(BF16)

    @pl.when((flag & 1) != 0)
    def _():
        x = xc_ref[slot]
        g = jnp.minimum(_dot(x, wgb_ref[...]) + bg_ref[...], SWIGLU_LIMIT)
        u = jnp.clip(_dot(x, wub_ref[...]) + bu_ref[...], -SWIGLU_LIMIT, SWIGLU_LIMIT)
        a_ref[...] = ((u + 1.0) * g * _sigmoid(g * SWIGLU_ALPHA)).astype(a_ref.dtype)

    @pl.when(flag == 0)
    def _():
        a_ref[...] = jnp.zeros(a_ref.shape, a_ref.dtype)


def _expert_down_body(*refs):
    pf, (a_ref, wd_ref, bd_ref, y_ref, wdb_ref, ac_ref) = refs[:7], refs[7:]
    flag, slot = _step_flags(pf)

    @pl.when((flag & 2) != 0)
    def _():
        wdb_ref[...] = wd_ref[...].astype(BF16)

    @pl.when((flag & 4) != 0)
    def _():
        ac_ref[slot] = a_ref[...]

    @pl.when((flag & 1) != 0)
    def _():
        y_ref[...] = _dot(ac_ref[slot], wdb_ref[...]) + bd_ref[...]

    @pl.when(flag == 0)
    def _():
        y_ref[...] = jnp.zeros(y_ref.shape, F32)


def _experts_two_pass(xs, tiles_e, tile_start, n_tiles, w_gu, b_gu, w_dn, b_dn):
    d = w_dn.shape[2]
    n_slots = n_tiles * MOE_TILE
    n_ff = D_FF // MOE_FF_TILE
    tf = MOE_FF_TILE
    up = _work_list(tiles_e, tile_start, n_tiles, n_ff)
    up_spec = pltpu.PrefetchScalarGridSpec(
        num_scalar_prefetch=len(up),
        grid=(n_tiles * n_ff,),
        in_specs=[
            pl.BlockSpec((MOE_TILE * ROW_PITCH, LANES), lambda s, *pf: (pf[2][s], 0)),
            pl.BlockSpec((None, d, tf), lambda s, *pf: (pf[0][s], 0, pf[1][s])),
            pl.BlockSpec((None, d, tf), lambda s, *pf: (pf[0][s], 0, n_ff + pf[1][s])),
            pl.BlockSpec((None, 1, tf), lambda s, *pf: (pf[0][s], 0, pf[1][s])),
            pl.BlockSpec((None, 1, tf), lambda s, *pf: (pf[0][s], 0, n_ff + pf[1][s])),
        ],
        out_specs=pl.BlockSpec((MOE_TILE, tf), lambda s, *pf: (pf[3][s], pf[4][s])),
        scratch_shapes=[pltpu.VMEM((d, tf), BF16), pltpu.VMEM((d, tf), BF16),
                        pltpu.VMEM((MOE_CACHE_TILES + 1, MOE_TILE, d), BF16)],
    )
    a = pl.pallas_call(
        _expert_up_body,
        grid_spec=up_spec,
        out_shape=jax.ShapeDtypeStruct((n_slots, D_FF), BF16),
        compiler_params=_cparams("arbitrary"),
        name="moe_up",
    )(*up, xs, w_gu, w_gu, b_gu, b_gu)

    tn = MOE_DOWN_TILE
    n_dn = d // tn
    down = _work_list(tiles_e, tile_start, n_tiles, n_dn)
    down_spec = pltpu.PrefetchScalarGridSpec(
        num_scalar_prefetch=len(down),
        grid=(n_tiles * n_dn,),
        in_specs=[
            pl.BlockSpec((MOE_TILE, D_FF), lambda s, *pf: (pf[2][s], 0)),
            pl.BlockSpec((None, D_FF, tn), lambda s, *pf: (pf[0][s], 0, pf[1][s])),
            pl.BlockSpec((None, 1, tn), lambda s, *pf: (pf[0][s], 0, pf[1][s])),
        ],
        out_specs=pl.BlockSpec((MOE_TILE, tn), lambda s, *pf: (pf[3][s], pf[4][s])),
        scratch_shapes=[pltpu.VMEM((D_FF, tn), BF16), pltpu.VMEM((MOE_CACHE_TILES + 1, MOE_TILE, D_FF), BF16)],
    )
    return pl.pallas_call(
        _expert_down_body,
        grid_spec=down_spec,
        out_shape=jax.ShapeDtypeStruct((n_slots, d), F32),
        compiler_params=_cparams("arbitrary"),
        name="moe_down",
    )(*down, a, w_dn, b_dn)


def _moe(x1, h, route, counts, w_gu, b_gu, w_dn, b_dn, n_prompt):
    t = x1.shape[0]
    n_tiles = -(-t * TOP_K // MOE_TILE) + N_EXPERTS
    n_slots = n_tiles * MOE_TILE
    cnt = counts[0, :N_EXPERTS].astype(jnp.int32)
    padded = (cnt + MOE_TILE - 1) // MOE_TILE * MOE_TILE
    pad_end = jnp.cumsum(padded)
    pad_start = pad_end - padded
    e_flat = route[:, :TOP_K].astype(jnp.int32).reshape(-1)
    rank_flat = route[:, 2 * TOP_K:3 * TOP_K].astype(jnp.int32).reshape(-1)
    xs = _dispatch(h, e_flat, rank_flat, pad_start, cnt, pad_end, n_slots)
    y = _experts_two_pass(xs, padded // MOE_TILE, pad_start // MOE_TILE, n_tiles, w_gu,
                          b_gu.reshape(N_EXPERTS, 1, -1), w_dn, b_dn.reshape(N_EXPERTS, 1, -1))
    return _combine(x1, route, y, e_flat, rank_flat, pad_start, n_prompt)


def _layer(xp, xs, pools, kwbuf, vwbuf, s_ret, page_table, p):
    batch, seq, d = xp.shape
    dec_batch, t_new, _ = xs.shape
    n_pages = page_table.shape[1]
    past = n_pages * PAGE_SIZE
    n_prompt = batch * seq
    n_sample = dec_batch * t_new
    assert seq & (seq - 1) == 0 and t_new & (t_new - 1) == 0 and seq % SEL_KV_TILE == 0
    assert n_prompt % ROW_TILE == 0 and n_sample % ROW_TILE == 0 and t_new % 8 == 0

    xp2, xs2 = xp.reshape(n_prompt, d), xs.reshape(n_sample, d)
    t_all = n_prompt + n_sample
    xn = _ln1(xp2, xs2, p["ln1_g"])

    w_in = p["w_in"]
    offs = [0]
    for n in (NSA_Q, KV_W, KV_W, KV_W, KV_W, KV_W, KV_W, 3 * NSA_HEADS, RET_QK, RET_QK, RET_V, RET_V, D_MODEL, D_MODEL):
        offs.append(offs[-1] + n)
    w_q = w_in[:, offs[0]:offs[1]].astype(BF16)
    w_kv = jnp.concatenate([w_in[:, offs[1]:offs[8]],
                            jnp.zeros((d, LANES - 3 * NSA_HEADS), F32)], axis=1).astype(BF16)
    w_rot = w_in[:, offs[8]:offs[10]].astype(BF16)
    w_act = w_in[:, offs[10]:offs[14]].astype(BF16)

    def rows(n):
        return pl.BlockSpec((ROW_TILE, n), lambda j, i: (i, 0))

    q = _proj_call(_proj_q_body, xn, w_q, [p["q_norm_g"].reshape(1, HEAD_DIM)], [_vec_spec(HEAD_DIM)],
                   jax.ShapeDtypeStruct((t_all, NSA_Q), BF16), rows(NSA_Q), NSA_Q, "proj_q")
    row2 = jax.ShapeDtypeStruct((NSA_GROUPS * t_all, HEAD_DIM), F32)
    row2_spec = pl.BlockSpec((NSA_GROUPS * ROW_TILE, HEAD_DIM), lambda j, i: (i, 0))
    *kv_rows, kvb, gate = _proj_call(
        _proj_kv_body, xn, w_kv,
        [p["k_sel_norm_g"].reshape(1, HEAD_DIM), p["k_win_norm_g"].reshape(1, HEAD_DIM)],
        [_vec_spec(HEAD_DIM), _vec_spec(HEAD_DIM)],
        [row2] * 6 + [jax.ShapeDtypeStruct((t_all, 4 * KV_W), BF16), jax.ShapeDtypeStruct((t_all, LANES), F32)],
        [row2_spec] * 6 + [rows(4 * KV_W), rows(LANES)], 6 * KV_W + LANES, "proj_kv")
    inv = 1.0 / (ROPE_BASE ** jnp.linspace(0.0, 1.0, RET_DK // 2, dtype=F32))
    rot = _proj_call(
        functools.partial(_proj_rot_body, n_prompt=n_prompt, seq=seq, past=past, dec_seq=t_new),
        xn, w_rot, [jnp.repeat(inv, 2).reshape(1, RET_DK)], [_vec_spec(RET_DK)],
        jax.ShapeDtypeStruct((t_all, 2 * RET_QK), BF16),
        pl.BlockSpec((ROW_TILE, RET_QK), lambda j, i: (i, j)), RET_QK, "proj_rot")
    act_tn = 1024

    def proj_act(w, act, name):
        return _proj_call(functools.partial(_proj_act_body, act=act), xn, w, [], [],
                          jax.ShapeDtypeStruct((t_all, w.shape[1]), BF16),
                          pl.BlockSpec((ROW_TILE, act_tn), lambda j, i: (i, j)), act_tn, name)

    rv = proj_act(w_act[:, :RET_V], "none", "proj_rv")
    rg = proj_act(w_act[:, RET_V:2 * RET_V], "silu", "proj_rg")
    merge_gates = proj_act(w_act[:, 2 * RET_V:], "sigmoid", "proj_gates")

    cmp_w = (_compress_weights(p["cmp_k_w1"], p["cmp_k_b1"], p["cmp_k_w2"])
             + _compress_weights(p["cmp_v_w1"], p["cmp_v_b1"], p["cmp_v_w2"])
             + (p["k_cmp_norm_g"].reshape(1, HEAD_DIM),))
    gn = p["ret_norm_g"].reshape(1, RET_V)

    kc_p, vc_p = _compress_prompt(kv_rows[0], kv_rows[1], cmp_w, batch, seq)
    o_nsa_p = _nsa_prompt(q, gate, kc_p, vc_p, kvb, batch, seq)
    o_ret_p, ret_p = _ret_prompt(rot, rv, rg, gn, batch, seq)

    pool_kc, pool_vc, pool_ks, pool_vs = [
        a.reshape(a.shape[0], NSA_GROUPS * PAGE_SIZE, HEAD_DIM) for a in pools]
    pt_flat = page_table.reshape(-1)
    kc_s, vc_s = _compress_sample(pool_kc, pool_vc, pt_flat, cmp_w, dec_batch, n_pages)
    wb = kwbuf.shape[1]
    o_nsa_s, kw_s, vw_s = _nsa_sample(
        q, gate, kv_rows[2:], n_prompt // (2 * t_new), kc_s, vc_s, pool_ks, pool_vs,
        kwbuf.reshape(dec_batch, wb * NSA_GROUPS, HEAD_DIM), vwbuf.reshape(dec_batch, wb * NSA_GROUPS, HEAD_DIM),
        pt_flat, dec_batch, t_new, n_pages)
    rot_s = rot[n_prompt:].astype(F32)
    o_ret_s, ret_s = _ret_sample(rot_s[:, :RET_QK], rot_s[:, RET_QK:], rv[n_prompt:].astype(F32),
                                 rg[n_prompt:].astype(F32),
                                 gn, s_ret, dec_batch, t_new)

    o_nsa = jnp.concatenate([o_nsa_p, o_nsa_s], axis=0)
    o_ret = jnp.concatenate([o_ret_p, o_ret_s.astype(BF16)], axis=0)
    mixed = _merge(o_nsa, o_ret, p["w_nsa_br"].astype(BF16), p["w_ret_br"].astype(BF16), merge_gates)
    rw = jnp.concatenate([p["router_w"], jnp.zeros((d, LANES - N_EXPERTS), F32)], axis=1)
    rwh, rwl = _split_bf16(rw)
    rb = jnp.concatenate([p["router_b"], jnp.zeros((LANES - N_EXPERTS,), F32)]).reshape(1, LANES)
    x1, h, route, counts = _out_router(xp2, xs2, mixed, p["w_out"].astype(BF16), p["ln2_g"].reshape(1, d),
                                       rwh, rwl, rb)
    y_p, y_s = _moe(x1, h, route, counts, p["w_gate_up"], p["b_gate_up"], p["w_down"], p["b_down"], n_prompt)

    kv5 = (NSA_GROUPS, HEAD_DIM)
    wbp = min(WINDOW, seq)
    rows_p = [a[:NSA_GROUPS * n_prompt].reshape(batch, seq, *kv5) for a in kv_rows]
    rows_s = [a[NSA_GROUPS * n_prompt:].reshape(dec_batch, t_new, *kv5) for a in kv_rows[:4]]
    states_p = (*rows_p[:4], rows_p[4][:, seq - wbp:], rows_p[5][:, seq - wbp:], ret_p)
    states_s = (*rows_s, kw_s.reshape(dec_batch, wb, *kv5), vw_s.reshape(dec_batch, wb, *kv5), ret_s)
    return y_p.reshape(batch, seq, d), y_s.reshape(dec_batch, t_new, d), states_p, states_s


def kernel(x_prompt, x_sample, cache_k_cmp, cache_v_cmp, cache_k_sel, cache_v_sel, state_k_win, state_v_win,
           state_ret, page_table, ln1_g, w_in, cmp_k_w1, cmp_k_b1, cmp_k_w2, cmp_v_w1, cmp_v_b1, cmp_v_w2,
           q_norm_g, k_cmp_norm_g, k_sel_norm_g, k_win_norm_g, ret_norm_g, w_nsa_br, w_ret_br, w_out, ln2_g,
           router_w, router_b, w_gate_up, b_gate_up, w_down, b_down):
    depth = w_in.shape[0]
    y_p, y_s = x_prompt, x_sample
    new_p, new_s = [], []
    for l in range(depth):
        p = {
            "ln1_g": ln1_g[l], "w_in": w_in[l],
            "cmp_k_w1": cmp_k_w1[l], "cmp_k_b1": cmp_k_b1[l], "cmp_k_w2": cmp_k_w2[l],
            "cmp_v_w1": cmp_v_w1[l], "cmp_v_b1": cmp_v_b1[l], "cmp_v_w2": cmp_v_w2[l],
            "q_norm_g": q_norm_g[l], "k_cmp_norm_g": k_cmp_norm_g[l], "k_sel_norm_g": k_sel_norm_g[l],
            "k_win_norm_g": k_win_norm_g[l], "ret_norm_g": ret_norm_g[l],
            "w_nsa_br": w_nsa_br[l], "w_ret_br": w_ret_br[l], "w_out": w_out[l], "ln2_g": ln2_g[l],
            "router_w": router_w[l], "router_b": router_b[l], "w_gate_up": w_gate_up[l],
            "b_gate_up": b_gate_up[l], "w_down": w_down[l], "b_down": b_down[l],
        }
        pools = (cache_k_cmp[l], cache_v_cmp[l], cache_k_sel[l], cache_v_sel[l])
        y_p, y_s, sp, ss = _layer(y_p, y_s, pools, state_k_win[l], state_v_win[l], state_ret[l], page_table, p)
        new_p.append(sp)
        new_s.append(ss)
    outs_p = [jnp.stack(a) for a in zip(*new_p)]
    outs_s = [jnp.stack(a) for a in zip(*new_s)]
    return (y_p, y_s, *outs_p, *outs_s)
```

```python
import functools
import math

import jax
import jax.numpy as jnp
from jax import lax
from jax.experimental import pallas as pl
from jax.experimental.pallas import tpu as pltpu

F32 = jnp.float32
BF16 = jnp.bfloat16

D_MODEL = 2048
PAGE_SIZE = 128
NSA_HEADS = 8
NSA_GROUPS = 2
HPG = NSA_HEADS // NSA_GROUPS
HEAD_DIM = 128
CMP_LEN = 32
CMP_STRIDE = 16
CMP_HIDDEN = 2 * HEAD_DIM
SEL_BLK = 64
SEL_SHIFT = 6
SEL_TOPK = 16
WINDOW = 512
Q_BLK = 128
RET_HEADS = 8
RET_DK = 128
RET_DV = 256
RET_CHUNK = 128
ROPE_BASE = 10000.0
N_EXPERTS = 32
TOP_K = 4
D_FF = D_MODEL
SWIGLU_LIMIT = 7.0
SWIGLU_ALPHA = 1.702
EPS = 1e-6
NEG = -1e30
BIG = 1e30

NSA_Q = NSA_HEADS * HEAD_DIM
KV_W = NSA_GROUPS * HEAD_DIM
RET_QK = RET_HEADS * RET_DK
RET_V = RET_HEADS * RET_DV

LANES = 128
ROW_TILE = 512
SEL_KV_TILE = 512
MOE_TILE = 512
STAGE_PITCH = 24
ROW_PITCH = 17
MOE_SUB = 256
MOE_ROUTE_TILE = 512
MOE_FF_TILE = 512
VMEM_LIMIT = 56 * 1024 * 1024


def _cparams(*sem):
    return pltpu.CompilerParams(dimension_semantics=sem, vmem_limit_bytes=VMEM_LIMIT)


def _dot(a, b):
    return jnp.dot(a, b, preferred_element_type=F32)


def _dot_nt(a, b):
    return lax.dot_general(a, b, (((1,), (1,)), ((), ())), preferred_element_type=F32)


def _dot_tn(a, b):
    return lax.dot_general(a, b, (((0,), (0,)), ((), ())), preferred_element_type=F32)


def _sigmoid(x):
    return 1.0 / (1.0 + jnp.exp(-x))


def _unit_rms(x):
    return x * lax.rsqrt(jnp.mean(x * x, axis=-1, keepdims=True) + EPS)


def _iota(shape, dim):
    return lax.broadcasted_iota(jnp.int32, shape, dim)


def _split_bf16(x):
    hi = x.astype(BF16)
    lo = (x - hi.astype(F32)).astype(BF16)
    return hi, lo


def _two_group_specs(tm, d, tiles_p):
    return [pl.BlockSpec((tm, d), lambda i: (jnp.minimum(i, tiles_p - 1), 0)),
            pl.BlockSpec((tm, d), lambda i: (jnp.maximum(i - tiles_p, 0), 0))]


def _ln1_body(xp_ref, xs_ref, g_ref, o_ref, *, tiles_p):
    def norm(x_ref):
        o_ref[...] = (_unit_rms(x_ref[...]) * g_ref[...]).astype(o_ref.dtype)

    pl.when(pl.program_id(0) < tiles_p)(lambda: norm(xp_ref))
    pl.when(pl.program_id(0) >= tiles_p)(lambda: norm(xs_ref))


def _ln1(xp, xs, g):
    d = xp.shape[1]
    tiles_p = xp.shape[0] // ROW_TILE
    t = xp.shape[0] + xs.shape[0]
    return pl.pallas_call(
        functools.partial(_ln1_body, tiles_p=tiles_p),
        grid=(t // ROW_TILE,),
        in_specs=_two_group_specs(ROW_TILE, d, tiles_p) + [pl.BlockSpec((1, d), lambda i: (0, 0))],
        out_specs=pl.BlockSpec((ROW_TILE, d), lambda i: (i, 0)),
        out_shape=jax.ShapeDtypeStruct((t, d), BF16),
        compiler_params=_cparams("arbitrary"),
        name="ln1",
    )(xp, xs, g.reshape(1, d))


def _proj_q_body(x_ref, w_ref, g_ref, o_ref):
    y = _dot(x_ref[...], w_ref[...])
    g = g_ref[...] * (HEAD_DIM ** -0.5)
    for c in range(NSA_HEADS):
        sl = slice(c * LANES, (c + 1) * LANES)
        o_ref[:, sl] = (_unit_rms(y[:, sl]) * g).astype(o_ref.dtype)


def _proj_kv_body(x_ref, w_ref, gs_ref, gw_ref, kc_ref, vc_ref, ks_ref, vs_ref, kw_ref, vw_ref, kvb_ref, gate_ref):
    y = _dot(x_ref[...], w_ref[...])
    tm = y.shape[0]
    outs = (kc_ref, vc_ref, ks_ref, vs_ref, kw_ref, vw_ref)
    for c in range(12):
        sl = slice(c * LANES, (c + 1) * LANES)
        yc = y[:, sl]
        if c in (4, 5):
            yc = _unit_rms(yc) * gs_ref[...]
        elif c in (8, 9):
            yc = _unit_rms(yc) * gw_ref[...]
        outs[c // NSA_GROUPS][pl.ds(c % NSA_GROUPS, tm, stride=NSA_GROUPS), :] = yc
        if c >= 4:
            kvb_ref[:, (c - 4) * LANES:(c - 3) * LANES] = yc.astype(BF16)
    gate_ref[...] = _sigmoid(y[:, 12 * LANES:13 * LANES])


def _proj_rot_body(x_ref, w_ref, inv_ref, o_ref, *, n_prompt, seq, past, dec_seq):
    j = pl.program_id(0)
    i = pl.program_id(1)
    y = _dot(x_ref[...], w_ref[...])
    tm = y.shape[0]
    row = i * tm + _iota((tm, LANES), 0)
    pos = jnp.where(row < n_prompt, row & (seq - 1), past + ((row - n_prompt) & (dec_seq - 1)))
    ang = pos.astype(F32) * inv_ref[...]
    cos = jnp.cos(ang)
    sin = jnp.sin(ang)
    even = (_iota((tm, LANES), 1) & 1) == 0
    sin = jnp.where(even, -sin, sin)
    scale = jnp.where(j == 0, 1.0, RET_DK ** -0.5).astype(F32)
    for c in range(RET_HEADS):
        sl = slice(c * LANES, (c + 1) * LANES)
        yc = y[:, sl]
        partner = jnp.where(even, pltpu.roll(yc, LANES - 1, 1), pltpu.roll(yc, 1, 1))
        o_ref[:, sl] = ((yc * cos + partner * sin) * scale).astype(o_ref.dtype)


def _proj_act_body(x_ref, w_ref, o_ref, *, act):
    y = _dot(x_ref[...], w_ref[...])
    if act == "silu":
        y = y * _sigmoid(y)
    elif act == "sigmoid":
        y = _sigmoid(y)
    o_ref[...] = y.astype(o_ref.dtype)


def _proj_call(body, xn, w, extra, extra_specs, out_shape, out_specs, tn, name):
    t, k = xn.shape
    n = w.shape[1]
    in_specs = [pl.BlockSpec((ROW_TILE, k), lambda j, i: (i, 0)),
                pl.BlockSpec((k, tn), lambda j, i: (0, j))] + extra_specs
    return pl.pallas_call(
        body,
        grid=(n // tn, t // ROW_TILE),
        in_specs=in_specs,
        out_specs=out_specs,
        out_shape=out_shape,
        compiler_params=_cparams("parallel", "parallel"),
        name=name,
    )(xn, w, *extra)


def _vec_spec(n):
    return pl.BlockSpec((1, n), lambda j, i: (0, 0))


def _compress_mlp(x_cat, w1_ref, b1_ref, w2_ref):
    nh = x_cat.shape[0]
    a = _dot(x_cat, w1_ref[...])
    pre = a[:, :CMP_HIDDEN] + pltpu.roll(a[:, CMP_HIDDEN:], nh - 1, 0) + b1_ref[...]
    hid = pre * (0.5 * (1.0 + jnp.tanh(math.sqrt(2.0 / math.pi) * (pre + 0.044715 * (pre * pre * pre)))))
    out = _dot(hid.astype(BF16), w2_ref[...])
    return jnp.where(_iota(out.shape, 0) < nh - 1, out, 0.0)


def _compress_finish(xk, xv, wrefs, kc_ref, vc_ref):
    w1k, b1k, w2k, w1v, b1v, w2v, gk = wrefs
    for g in range(NSA_GROUPS):
        sl = slice(g * LANES, (g + 1) * LANES)
        kc = _compress_mlp(xk[g], w1k, b1k, w2k)
        kc_ref[:, sl] = (_unit_rms(kc) * gk[...]).astype(kc_ref.dtype)
        vc_ref[:, sl] = _compress_mlp(xv[g], w1v, b1v, w2v).astype(vc_ref.dtype)


def _compress_prompt_body(krows_ref, vrows_ref, *refs, nh):
    wrefs, (kc_ref, vc_ref) = refs[:7], refs[7:]

    def gather(rows_ref, g):
        return jnp.concatenate(
            [rows_ref[pl.ds(NSA_GROUPS * s + g, nh, stride=NSA_GROUPS * CMP_STRIDE), :].astype(BF16)
             for s in range(CMP_STRIDE)], axis=1)

    xk = [gather(krows_ref, g) for g in range(NSA_GROUPS)]
    xv = [gather(vrows_ref, g) for g in range(NSA_GROUPS)]
    _compress_finish(xk, xv, wrefs, kc_ref, vc_ref)


def _compress_sample_body(pt_ref, *refs, n_pages):
    del pt_ref
    kpages, vpages = refs[:n_pages], refs[n_pages:2 * n_pages]
    wrefs = refs[2 * n_pages:2 * n_pages + 7]
    kc_ref, vc_ref, stage_ref = refs[2 * n_pages + 7:]
    per_page = PAGE_SIZE // CMP_STRIDE
    nh = n_pages * per_page

    def gather(pages, g, slab):
        stage = stage_ref.at[slab]
        for p, page in enumerate(pages):
            for m in range(per_page):
                half = page[pl.ds(NSA_GROUPS * CMP_STRIDE * m + g, CMP_STRIDE, stride=NSA_GROUPS), :]
                stage[pl.ds((p * per_page + m) * STAGE_PITCH, CMP_STRIDE), :] = half
        return jnp.concatenate(
            [stage[pl.ds(s, nh, stride=STAGE_PITCH), :].astype(BF16) for s in range(CMP_STRIDE)], axis=1)

    xk = [gather(kpages, g, g) for g in range(NSA_GROUPS)]
    xv = [gather(vpages, g, NSA_GROUPS + g) for g in range(NSA_GROUPS)]
    _compress_finish(xk, xv, wrefs, kc_ref, vc_ref)


def _const_spec(shape, nargs):
    zeros = (0,) * len(shape)
    return pl.BlockSpec(shape, lambda *a: zeros)


def _compress_weights(w1, b1, w2):
    r_n = CMP_LEN // CMP_STRIDE
    w1r = w1.reshape(r_n, CMP_STRIDE * HEAD_DIM, CMP_HIDDEN)
    w1cat = jnp.concatenate([w1r[r] for r in range(r_n)], axis=1).astype(BF16)
    return w1cat, b1.reshape(1, CMP_HIDDEN), w2.astype(BF16)


def _compress_weight_specs():
    k16 = CMP_STRIDE * HEAD_DIM
    one = [_const_spec((k16, 2 * CMP_HIDDEN), 0), _const_spec((1, CMP_HIDDEN), 0),
           _const_spec((CMP_HIDDEN, HEAD_DIM), 0)]
    return one + one + [_const_spec((1, HEAD_DIM), 0)]


def _compress_prompt(krows, vrows, weights, batch, seq):
    nh = seq // CMP_STRIDE
    out = jax.ShapeDtypeStruct((batch, nh, KV_W), BF16)
    ospec = pl.BlockSpec((None, nh, KV_W), lambda b: (b, 0, 0))
    return pl.pallas_call(
        functools.partial(_compress_prompt_body, nh=nh),
        grid=(batch,),
        in_specs=[pl.BlockSpec((NSA_GROUPS * seq, LANES), lambda b: (b, 0))] * 2 + _compress_weight_specs(),
        out_specs=[ospec, ospec],
        out_shape=[out, out],
        compiler_params=_cparams("parallel"),
        name="compress_prompt",
    )(krows, vrows, *weights)


def _page_specs(n_pages, nb=1, bb=0):
    def spec(p):
        return pl.BlockSpec((None, NSA_GROUPS * PAGE_SIZE, HEAD_DIM),
                            lambda b, pt: (pt[(b * nb + bb) * n_pages + p], 0, 0))
    return [spec(p) for p in range(n_pages)]


def _compress_sample(pool_k, pool_v, pt_flat, weights, dec_batch, n_pages):
    nh = n_pages * PAGE_SIZE // CMP_STRIDE
    out = jax.ShapeDtypeStruct((dec_batch, nh, KV_W), BF16)
    ospec = pl.BlockSpec((None, nh, KV_W), lambda b, pt: (b, 0, 0))
    grid_spec = pltpu.PrefetchScalarGridSpec(
        num_scalar_prefetch=1,
        grid=(dec_batch,),
        in_specs=_page_specs(n_pages) + _page_specs(n_pages) + _compress_weight_specs(),
        out_specs=[ospec, ospec],
        scratch_shapes=[pltpu.VMEM((2 * NSA_GROUPS, nh * STAGE_PITCH, HEAD_DIM), F32)],
    )
    return pl.pallas_call(
        functools.partial(_compress_sample_body, n_pages=n_pages),
        grid_spec=grid_spec,
        out_shape=[out, out],
        compiler_params=_cparams("parallel"),
        name="compress_sample",
    )(pt_flat, *([pool_k] * n_pages), *([pool_v] * n_pages), *weights)


def _masked_softmax(s3, ok):
    s3 = jnp.where(ok[None], s3, NEG)
    m = jnp.max(s3, axis=-1, keepdims=True)
    e = jnp.where(ok[None], jnp.exp(s3 - m), 0.0)
    den = jnp.sum(e, axis=-1, keepdims=True)
    return e / jnp.where(den > 0.0, den, 1.0)


def _topk_lanes(score, n_sel, k):
    lane = _iota(score.shape, 1)
    sc = jnp.where(lane < n_sel, score, -jnp.inf)
    rank = jnp.zeros(score.shape, F32)
    for i in range(n_sel):
        ci = sc[:, i:i + 1]
        later = jnp.where(lane > i, 1.0, 0.0)
        rank = rank + jnp.where(ci > sc, 1.0, jnp.where(ci == sc, later, 0.0))
    return jnp.where((rank < k) & (lane < n_sel), 1.0, 0.0)


def _bias_softmax_pv(s3, ok, v):
    h, t, n = s3.shape
    s3 = s3 + jnp.where(ok, 0.0, NEG)[None]
    e = jnp.exp(s3 - jnp.max(s3, axis=-1, keepdims=True))
    inv = 1.0 / jnp.sum(e, axis=-1, keepdims=True)
    return _dot(e.reshape(h * t, n).astype(BF16), v) * inv.reshape(h * t, 1)


def _compressed_branch(qg, kcg, vcg, qpos, n_cmp):
    t = qpos.shape[0]
    nc_pad = kcg.shape[0]
    c_idx = _iota((t, nc_pad), 1)
    vis = (c_idx * CMP_STRIDE + (CMP_LEN - 1) <= qpos) & (c_idx < n_cmp)
    p3 = _masked_softmax(_dot_nt(qg, kcg).reshape(HPG, t, nc_pad), vis)
    o_c = _dot(p3.reshape(HPG * t, nc_pad).astype(BF16), vcg)
    return o_c, jnp.sum(p3, axis=0)


def _overlap(nc_pad, blocks_first):
    shape = (LANES, nc_pad) if blocks_first else (nc_pad, LANES)
    ci = _iota(shape, 1 if blocks_first else 0) * CMP_STRIDE
    sj = _iota(shape, 0 if blocks_first else 1) * SEL_BLK
    return jnp.where((ci < sj + SEL_BLK) & (ci + CMP_LEN > sj), 1.0, 0.0).astype(BF16)


def _select_lanes(psum, qpos, n_sel):
    t, nc_pad = psum.shape
    hi, lo = _split_bf16(psum)
    ov = _overlap(nc_pad, False)
    score = _dot(hi, ov) + _dot(lo, ov)
    j = _iota((t, LANES), 1)
    cur = qpos >> SEL_SHIFT
    forced = (j == 0) | (j == cur) | (j == cur - 1)
    score = jnp.where(forced, BIG, score)
    score = jnp.where(j * SEL_BLK <= qpos, score, NEG)
    return _topk_lanes(score, n_sel, min(SEL_TOPK, n_sel))


def _select_sublanes(psum, start, n_sel):
    t, nc_pad = psum.shape
    hi, lo = _split_bf16(psum)
    ov = _overlap(nc_pad, True)
    rows = -(-n_sel // 8) * 8
    sc = (_dot_nt(ov, hi) + _dot_nt(ov, lo))[:rows]
    j = _iota((rows, t), 0)
    qp = start + _iota((rows, t), 1)
    cur = qp >> SEL_SHIFT
    forced = (j == 0) | (j == cur) | (j == cur - 1)
    sc = jnp.where(forced, BIG, sc)
    sc = jnp.where(j * SEL_BLK <= qp, sc, NEG)
    sc = jnp.where(j < n_sel, sc, -jnp.inf)
    rank = jnp.zeros((rows, t), F32)
    for i in range(n_sel):
        ri = sc[i:i + 1, :]
        later = jnp.where(j > i, 1.0, 0.0)
        rank = rank + jnp.where(ri > sc, 1.0, jnp.where(ri == sc, later, 0.0))
    sel = jnp.where((rank < min(SEL_TOPK, n_sel)) & (j < n_sel), 1.0, 0.0).astype(BF16)
    if rows < LANES:
        sel = jnp.concatenate([sel, jnp.zeros((LANES - rows, t), BF16)], axis=0)
    return sel


def _block_to_key(blk0, n):
    blk = blk0 + (_iota((LANES, n), 1) >> SEL_SHIFT)
    return jnp.where(_iota((LANES, n), 0) == blk, 1.0, 0.0).astype(BF16)


def _nsa_prompt_body(q_ref, gate_ref, kc_ref, vc_ref, kv_ref, o_ref, *, seq):
    tq = Q_BLK
    start = pl.program_id(1) * tq
    n_sel = seq // SEL_BLK
    n_cmp = seq // CMP_STRIDE - CMP_LEN // CMP_STRIDE + 1
    qpos = start + _iota((tq, 1), 0)
    gates = gate_ref[...]
    wk = WINDOW + tq
    for g in range(NSA_GROUPS):
        gsl = slice(g * LANES, (g + 1) * LANES)
        qg = jnp.concatenate(
            [q_ref[:, (g * HPG + j) * LANES:(g * HPG + j + 1) * LANES] for j in range(HPG)], axis=0)
        o_c, psum = _compressed_branch(qg, kc_ref[:, gsl], vc_ref[:, gsl], qpos, n_cmp)
        sel_t = _select_sublanes(psum, start, n_sel)

        def sel_step(k, carry):
            m, l, acc = carry
            k0 = pl.multiple_of(k * SEL_KV_TILE, SEL_KV_TILE)
            kt = kv_ref[pl.ds(k0, SEL_KV_TILE), pl.ds(g * LANES, LANES)]
            vt = kv_ref[pl.ds(k0, SEL_KV_TILE), pl.ds(KV_W + g * LANES, LANES)]
            chosen = _dot_tn(sel_t, _block_to_key(k * (SEL_KV_TILE // SEL_BLK), SEL_KV_TILE))
            ok = (chosen > 0.5) & (k0 + _iota((tq, SEL_KV_TILE), 1) <= qpos)
            s3 = _dot_nt(qg, kt).reshape(HPG, tq, SEL_KV_TILE) + jnp.where(ok, 0.0, NEG)[None]
            m_new = jnp.maximum(m, jnp.max(s3, axis=-1, keepdims=True))
            alpha = jnp.exp(m - m_new)
            p = jnp.exp(s3 - m_new)
            l = alpha * l + jnp.sum(p, axis=-1, keepdims=True)
            pv = _dot(p.reshape(HPG * tq, SEL_KV_TILE).astype(BF16), vt)
            return m_new, l, alpha * acc + pv.reshape(HPG, tq, LANES)

        n_kv = (start + tq + SEL_KV_TILE - 1) // SEL_KV_TILE
        init = (jnp.full((HPG, tq, 1), NEG, F32), jnp.zeros((HPG, tq, 1), F32),
                jnp.zeros((HPG, tq, LANES), F32))
        _, l_s, acc_s = lax.fori_loop(0, n_kv, sel_step, init)
        o_s = acc_s * (1.0 / l_s)

        w0 = pl.multiple_of(jnp.maximum(start - WINDOW, 0), Q_BLK)
        kw = kv_ref[pl.ds(w0, wk), pl.ds(2 * KV_W + g * LANES, LANES)]
        vw = kv_ref[pl.ds(w0, wk), pl.ds(3 * KV_W + g * LANES, LANES)]
        dpos = qpos - (w0 + _iota((tq, wk), 1))
        o_w = _bias_softmax_pv(_dot_nt(qg, kw).reshape(HPG, tq, wk), (dpos >= 0) & (dpos < WINDOW), vw)

        for j in range(HPG):
            h = g * HPG + j
            rows = slice(j * tq, (j + 1) * tq)
            o = (o_c[rows] * gates[:, 3 * h:3 * h + 1] + o_s[j] * gates[:, 3 * h + 1:3 * h + 2]
                 + o_w[rows] * gates[:, 3 * h + 2:3 * h + 3])
            o_ref[:, h * LANES:(h + 1) * LANES] = o.astype(o_ref.dtype)


def _nsa_prompt(q, gate, kc, vc, kvb, batch, seq):
    nq = seq // Q_BLK
    nh = seq // CMP_STRIDE
    return pl.pallas_call(
        functools.partial(_nsa_prompt_body, seq=seq),
        grid=(batch, nq),
        in_specs=[pl.BlockSpec((Q_BLK, NSA_Q), lambda b, i: (b * nq + i, 0)),
                  pl.BlockSpec((Q_BLK, LANES), lambda b, i: (b * nq + i, 0)),
                  pl.BlockSpec((None, nh, KV_W), lambda b, i: (b, 0, 0)),
                  pl.BlockSpec((None, nh, KV_W), lambda b, i: (b, 0, 0)),
                  pl.BlockSpec((seq, 4 * KV_W), lambda b, i: (b, 0))],
        out_specs=pl.BlockSpec((Q_BLK, NSA_Q), lambda b, i: (b * nq + i, 0)),
        out_shape=jax.ShapeDtypeStruct((batch * seq, NSA_Q), BF16),
        compiler_params=_cparams("parallel", "parallel"),
        name="nsa_prompt",
    )(q, gate, kc, vc, kvb)


def _nsa_sample_one(q, gates, new_refs, kc_ref, vc_ref, kpages, vpages, kwbuf_ref, vwbuf_ref,
                    kwout_ref, vwout_ref, past):
    t = q.shape[0]
    wb = kwbuf_ref.shape[0] // NSA_GROUPS
    n_past = past // SEL_BLK
    n_sel = n_past + -(-t // SEL_BLK)
    n_cmp = (past + t) // CMP_STRIDE - CMP_LEN // CMP_STRIDE + 1
    qpos = past + _iota((t, 1), 0)
    pad = jnp.zeros((LANES - t, LANES), BF16)
    n_keys = past + LANES
    wk = wb + LANES
    heads = []
    for g in range(NSA_GROUPS):
        gsl = slice(g * LANES, (g + 1) * LANES)

        def new_rows(which):
            return new_refs[which][pl.ds(g, t, stride=NSA_GROUPS), :]

        qg = jnp.concatenate(
            [q[:, (g * HPG + j) * LANES:(g * HPG + j + 1) * LANES] for j in range(HPG)], axis=0).astype(BF16)
        o_c, psum = _compressed_branch(qg, kc_ref[:, gsl], vc_ref[:, gsl], qpos, n_cmp)
        sel = _select_lanes(psum, qpos, n_sel)

        def keys(pages, which):
            past_rows = [p[pl.ds(g, PAGE_SIZE, stride=NSA_GROUPS), :].astype(BF16) for p in pages]
            return jnp.concatenate(past_rows + [new_rows(which).astype(BF16), pad], axis=0)

        ks, vs = keys(kpages, 2), keys(vpages, 3)
        chosen = _dot(sel.astype(BF16), _block_to_key(0, n_keys))
        ok = (chosen > 0.5) & (_iota((t, n_keys), 1) <= qpos)
        o_s = _bias_softmax_pv(_dot_nt(qg, ks).reshape(HPG, t, n_keys), ok, vs)

        def window(buf_ref, which):
            return jnp.concatenate([buf_ref[pl.ds(g, wb, stride=NSA_GROUPS), :].astype(BF16),
                                    new_rows(which).astype(BF16), pad], axis=0)

        kw, vw = window(kwbuf_ref, 4), window(vwbuf_ref, 5)
        dpos = qpos - (past - wb + _iota((t, wk), 1))
        o_w = _bias_softmax_pv(_dot_nt(qg, kw).reshape(HPG, t, wk), (dpos >= 0) & (dpos < WINDOW), vw)

        for j in range(HPG):
            h = g * HPG + j
            rows = slice(j * t, (j + 1) * t)
            heads.append(o_c[rows] * gates[:, 3 * h:3 * h + 1] + o_s[rows] * gates[:, 3 * h + 1:3 * h + 2]
                         + o_w[rows] * gates[:, 3 * h + 2:3 * h + 3])

        kwout_ref[pl.ds((wb - t) * NSA_GROUPS + g, t, stride=NSA_GROUPS), :] = new_rows(4)
        vwout_ref[pl.ds((wb - t) * NSA_GROUPS + g, t, stride=NSA_GROUPS), :] = new_rows(5)
    keep = (wb - t) * NSA_GROUPS
    kwout_ref[pl.ds(0, keep), :] = kwbuf_ref[pl.ds(t * NSA_GROUPS, keep), :]
    vwout_ref[pl.ds(0, keep), :] = vwbuf_ref[pl.ds(t * NSA_GROUPS, keep), :]
    return heads


def _nsa_sample_body(pt_ref, q_ref, gate_ref, ksn_ref, vsn_ref, kwn_ref, vwn_ref, kc_ref, vc_ref, *refs,
                     n_pages, past, nb):
    del pt_ref
    kpages, vpages = refs[:nb * n_pages], refs[nb * n_pages:2 * nb * n_pages]
    kwbuf_ref, vwbuf_ref, o_ref, kwout_ref, vwout_ref = refs[2 * nb * n_pages:]
    t = q_ref.shape[0] // nb
    q = q_ref[...].astype(F32)
    gates = gate_ref[...]
    per_item = []
    for bb in range(nb):
        tok = slice(bb * t, (bb + 1) * t)
        tok2 = pl.ds(bb * NSA_GROUPS * t, NSA_GROUPS * t)
        new_refs = {2: ksn_ref.at[tok2], 3: vsn_ref.at[tok2], 4: kwn_ref.at[tok2], 5: vwn_ref.at[tok2]}
        pages = slice(bb * n_pages, (bb + 1) * n_pages)
        per_item.append(_nsa_sample_one(
            q[tok], gates[tok], new_refs, kc_ref.at[bb], vc_ref.at[bb], kpages[pages], vpages[pages],
            kwbuf_ref.at[bb], vwbuf_ref.at[bb], kwout_ref.at[bb], vwout_ref.at[bb], past))
    for h in range(NSA_HEADS):
        o_ref[:, h * LANES:(h + 1) * LANES] = jnp.concatenate(
            [heads[h] for heads in per_item], axis=0).astype(o_ref.dtype)


def _nsa_sample(q, gate, new, row_block0, kc, vc, pool_k, pool_v, kwbuf, vwbuf, pt_flat, dec_batch, t, n_pages,
                nb=2):
    past = n_pages * PAGE_SIZE
    nh = past // CMP_STRIDE
    wrows = kwbuf.shape[1]

    def rows(r, n):
        return pl.BlockSpec((nb * r, n), lambda b, pt: (row_block0 + b, 0))

    def per_b(r, c):
        return pl.BlockSpec((nb, r, c), lambda b, pt: (b, 0, 0))

    pages = [spec for bb in range(nb) for spec in _page_specs(n_pages, nb, bb)]
    grid_spec = pltpu.PrefetchScalarGridSpec(
        num_scalar_prefetch=1,
        grid=(dec_batch // nb,),
        in_specs=[rows(t, NSA_Q), rows(t, LANES)] + [rows(NSA_GROUPS * t, HEAD_DIM)] * 4
        + [per_b(nh, KV_W), per_b(nh, KV_W)] + pages + pages + [per_b(wrows, HEAD_DIM), per_b(wrows, HEAD_DIM)],
        out_specs=[pl.BlockSpec((nb * t, NSA_Q), lambda b, pt: (b, 0)), per_b(wrows, HEAD_DIM),
                   per_b(wrows, HEAD_DIM)],
    )
    wout = jax.ShapeDtypeStruct(kwbuf.shape, F32)
    pools = [pool_k] * (nb * n_pages) + [pool_v] * (nb * n_pages)
    return pl.pallas_call(
        functools.partial(_nsa_sample_body, n_pages=n_pages, past=past, nb=nb),
        grid_spec=grid_spec,
        out_shape=[jax.ShapeDtypeStruct((dec_batch * t, NSA_Q), BF16), wout, wout],
        compiler_params=_cparams("parallel"),
        name="nsa_sample",
    )(pt_flat, q, gate, *new, kc, vc, *pools, kwbuf, vwbuf)


def _log_decay(h):
    return math.log(1.0 - 2.0 ** (-5.0 - h))


def _ret_finish(o, h, gn_ref, rg):
    sl = slice(h * RET_DV, (h + 1) * RET_DV)
    return _unit_rms(o) * gn_ref[:, sl] * rg[:, sl]


def _ret_prompt_body(rq_ref, rk_ref, rv_ref, rg_ref, gn_ref, o_ref, s_ref):
    c = RET_CHUNK

    @pl.when(pl.program_id(1) == 0)
    def _():
        s_ref[...] = jnp.zeros(s_ref.shape, F32)

    n_col = _iota((c, 1), 0).astype(F32)
    diff = (_iota((c, c), 0) - _iota((c, c), 1)).astype(F32)
    rg = rg_ref[...].astype(F32)
    for h in range(RET_HEADS):
        lg = _log_decay(h)
        q = rq_ref[:, h * RET_DK:(h + 1) * RET_DK]
        k = rk_ref[:, h * RET_DK:(h + 1) * RET_DK]
        v = rv_ref[:, h * RET_DV:(h + 1) * RET_DV]
        s_prev = s_ref[h]
        dmask = jnp.where(diff >= 0.0, jnp.exp(jnp.maximum(diff, 0.0) * lg), 0.0)
        o = _dot((_dot_nt(q, k) * dmask).astype(BF16), v)
        q_dec = (q.astype(F32) * jnp.exp((n_col + 1.0) * lg)).astype(BF16)
        o = o + _dot(q_dec, s_prev.astype(BF16))
        k_dec = (k.astype(F32) * jnp.exp((c - 1.0 - n_col) * lg)).astype(BF16)
        s_ref[h] = math.exp(c * lg) * s_prev + _dot_tn(k_dec, v)
        o_ref[:, h * RET_DV:(h + 1) * RET_DV] = _ret_finish(o, h, gn_ref, rg).astype(o_ref.dtype)


def _ret_prompt(rot, rv, rg, gn, batch, seq):
    nch = seq // RET_CHUNK
    qk_w = RET_QK

    def rows(n, col):
        return pl.BlockSpec((RET_CHUNK, n), lambda b, i: (b * nch + i, col))

    return pl.pallas_call(
        _ret_prompt_body,
        grid=(batch, nch),
        in_specs=[rows(qk_w, 0), rows(qk_w, 1), rows(RET_V, 0), rows(RET_V, 0),
                  pl.BlockSpec((1, RET_V), lambda b, i: (0, 0))],
        out_specs=[rows(RET_V, 0),
                   pl.BlockSpec((None, RET_HEADS, RET_DK, RET_DV), lambda b, i: (b, 0, 0, 0))],
        out_shape=[jax.ShapeDtypeStruct((batch * seq, RET_V), BF16),
                   jax.ShapeDtypeStruct((batch, RET_HEADS, RET_DK, RET_DV), F32)],
        compiler_params=_cparams("parallel", "arbitrary"),
        name="retention_prompt",
    )(rot, rot, rv, rg, gn)


def _ret_sample_body(rq_ref, rk_ref, rv_ref, rg_ref, gn_ref, s_ref, o_ref, so_ref):
    c = rq_ref.shape[0]
    n_col = _iota((c, 1), 0).astype(F32)
    rg = rg_ref[...]
    zk = jnp.zeros((LANES - c, RET_DK), BF16)
    zv = jnp.zeros((LANES - c, RET_DV), BF16)
    for h in range(RET_HEADS):
        lg = _log_decay(h)
        q = rq_ref[:, h * RET_DK:(h + 1) * RET_DK]
        k = rk_ref[:, h * RET_DK:(h + 1) * RET_DK]
        v = rv_ref[:, h * RET_DV:(h + 1) * RET_DV]
        s_prev = s_ref[h]
        o = _dot((q * jnp.exp((n_col + 1.0) * lg)).astype(BF16), s_prev.astype(BF16))
        for j in range(c):
            qk = jnp.sum(q * k[j:j + 1, :], axis=-1, keepdims=True)
            dj = jnp.where(n_col >= j, jnp.exp(jnp.maximum(n_col - j, 0.0) * lg), 0.0)
            o = o + (qk * dj) * v[j:j + 1, :]
        k_dec = jnp.concatenate([(k * jnp.exp((c - 1.0 - n_col) * lg)).astype(BF16), zk], axis=0)
        v_pad = jnp.concatenate([v.astype(BF16), zv], axis=0)
        so_ref[h] = math.exp(c * lg) * s_prev + _dot_tn(k_dec, v_pad)
        o_ref[:, h * RET_DV:(h + 1) * RET_DV] = _ret_finish(o, h, gn_ref, rg)


def _ret_sample(rq, rk, rv, rg, gn, state, dec_batch, t):
    def rows(n):
        return pl.BlockSpec((t, n), lambda b: (b, 0))

    sspec = pl.BlockSpec((None, RET_HEADS, RET_DK, RET_DV), lambda b: (b, 0, 0, 0))
    return pl.pallas_call(
        _ret_sample_body,
        grid=(dec_batch,),
        in_specs=[rows(RET_QK), rows(RET_QK), rows(RET_V), rows(RET_V),
                  pl.BlockSpec((1, RET_V), lambda b: (0, 0)), sspec],
        out_specs=[rows(RET_V), sspec],
        out_shape=[jax.ShapeDtypeStruct((dec_batch * t, RET_V), F32),
                   jax.ShapeDtypeStruct(state.shape, F32)],
        compiler_params=_cparams("parallel"),
        name="retention_sample",
    )(rq, rk, rv, rg, gn, state)


def _merge_body(onp_ref, ons_ref, orp_ref, ors_ref, wn_ref, wr_ref, ga_ref, gr_ref, o_ref, *, tiles_p):
    def run(o_nsa, o_ret):
        a = _dot(o_nsa, wn_ref[...])
        r = _dot(o_ret, wr_ref[...])
        o_ref[...] = (ga_ref[...].astype(F32) * a + gr_ref[...].astype(F32) * r).astype(o_ref.dtype)

    pl.when(pl.program_id(1) < tiles_p)(lambda: run(onp_ref[...], orp_ref[...]))
    pl.when(pl.program_id(1) >= tiles_p)(lambda: run(ons_ref[...], ors_ref[...].astype(BF16)))


def _merge(o_nsa_p, o_nsa_s, o_ret_p, o_ret_s, wn, wr, gates, tn=1024):
    t = o_nsa_p.shape[0] + o_nsa_s.shape[0]
    d = wn.shape[1]
    tiles_p = o_nsa_p.shape[0] // ROW_TILE
    gr0 = d // tn

    def group_rows(n):
        return [pl.BlockSpec((ROW_TILE, n), lambda j, i: (jnp.minimum(i, tiles_p - 1), 0)),
                pl.BlockSpec((ROW_TILE, n), lambda j, i: (jnp.maximum(i - tiles_p, 0), 0))]

    return pl.pallas_call(
        functools.partial(_merge_body, tiles_p=tiles_p),
        grid=(d // tn, t // ROW_TILE),
        in_specs=group_rows(NSA_Q) + group_rows(RET_V)
        + [pl.BlockSpec((NSA_Q, tn), lambda j, i: (0, j)),
           pl.BlockSpec((RET_V, tn), lambda j, i: (0, j)),
           pl.BlockSpec((ROW_TILE, tn), lambda j, i: (i, j)),
           pl.BlockSpec((ROW_TILE, tn), lambda j, i: (i, gr0 + j))],
        out_specs=pl.BlockSpec((ROW_TILE, tn), lambda j, i: (i, j)),
        out_shape=jax.ShapeDtypeStruct((t, d), BF16),
        compiler_params=_cparams("arbitrary", "arbitrary"),
        name="merge",
    )(o_nsa_p, o_nsa_s, o_ret_p, o_ret_s, wn, wr, gates, gates)


def _out_router_body(xp_ref, xs_ref, mix_ref, wo_ref, g_ref, rwh_ref, rwl_ref, rb_ref,
                     x1_ref, h_ref, route_ref, cnt_ref, *, tiles_p):
    @pl.when(pl.program_id(0) == 0)
    def _():
        cnt_ref[...] = jnp.zeros(cnt_ref.shape, F32)

    x = jnp.where(pl.program_id(0) < tiles_p, xp_ref[...], xs_ref[...])
    x1 = x + _dot(mix_ref[...], wo_ref[...])
    x1_ref[...] = x1
    h = _unit_rms(x1) * g_ref[...]
    tm = h.shape[0]
    for cc in range(h.shape[1] // LANES):
        h_ref[_flat_idx(0, tm, cc)] = h[:, cc * LANES:(cc + 1) * LANES]
    for cc in range(h.shape[1] // LANES, ROW_PITCH):
        h_ref[_flat_idx(0, tm, cc)] = jnp.zeros((tm, LANES), F32)
    tm = h.shape[0]
    lane = _iota((tm, LANES), 1)
    hi, lo = _split_bf16(h)
    logits = _dot(hi, rwh_ref[...]) + _dot(lo, rwh_ref[...]) + _dot(hi, rwl_ref[...]) + rb_ref[...]
    work = jnp.where(lane < N_EXPERTS, logits, -jnp.inf)
    vals, idxs = [], []
    for _ in range(TOP_K):
        v = jnp.max(work, axis=-1, keepdims=True)
        ix = jnp.min(jnp.where(work == v, lane, LANES), axis=-1, keepdims=True)
        vals.append(v)
        idxs.append(ix)
        work = jnp.where(lane == ix, -jnp.inf, work)
    es = [jnp.exp(v - vals[0]) for v in vals]
    den = es[0] + es[1] + es[2] + es[3]
    hot = jnp.zeros((tm, LANES), F32)
    for ix in idxs:
        hot = hot + jnp.where(lane == ix, 1.0, 0.0)
    before = jnp.where(_iota((tm, tm), 1) < _iota((tm, tm), 0), 1.0, 0.0).astype(BF16)
    ranks = _dot(before, hot.astype(BF16)) + cnt_ref[...]
    route = jnp.zeros((tm, LANES), F32)
    for k in range(TOP_K):
        rk = jnp.sum(jnp.where(lane == idxs[k], ranks, 0.0), axis=-1, keepdims=True)
        route = route + jnp.where(lane == k, idxs[k].astype(F32), 0.0)
        route = route + jnp.where(lane == TOP_K + k, es[k] / den, 0.0)
        route = route + jnp.where(lane == 2 * TOP_K + k, rk, 0.0)
    route_ref[...] = route
    cnt_ref[...] = cnt_ref[...] + jnp.sum(hot, axis=0, keepdims=True)


def _out_router(xp, xs, mixed, wo, g2, rwh, rwl, rb, tm=256):
    d = xp.shape[1]
    t = xp.shape[0] + xs.shape[0]
    tiles_p = xp.shape[0] // tm
    rows = pl.BlockSpec((tm, d), lambda i: (i, 0))
    lanes = pl.BlockSpec((tm, LANES), lambda i: (i, 0))

    def const(r, c):
        return pl.BlockSpec((r, c), lambda i: (0, 0))

    return pl.pallas_call(
        functools.partial(_out_router_body, tiles_p=tiles_p),
        grid=(t // tm,),
        in_specs=_two_group_specs(tm, d, tiles_p)
        + [rows, const(d, d), const(1, d), const(d, LANES), const(d, LANES), const(1, LANES)],
        out_specs=[rows, pl.BlockSpec((tm * ROW_PITCH, LANES), lambda i: (i, 0)), lanes, const(1, LANES)],
        out_shape=[jax.ShapeDtypeStruct((t, d), F32), jax.ShapeDtypeStruct((t * ROW_PITCH, LANES), F32),
                   jax.ShapeDtypeStruct((t, LANES), F32), jax.ShapeDtypeStruct((1, LANES), F32)],
        compiler_params=_cparams("arbitrary"),
        name="out_router",
    )(xp, xs, mixed, wo, g2, rwh, rwl, rb)


def _row_copy(src, r_src, dst, r_dst, sem):
    return pltpu.make_async_copy(src.at[pl.ds(r_src * ROW_PITCH, ROW_PITCH), :],
                                 dst.at[pl.ds(r_dst * ROW_PITCH, ROW_PITCH), :], sem)


def _flat_idx(row0, n, c):
    return (pl.ds(row0 * ROW_PITCH + c, n, stride=ROW_PITCH), slice(None))


def _dispatch_body(start_ref, cnt_ref, pend_ref, e_ref, rank_ref, h_ref, xs_ref, zero_ref, sem, zsem):
    tm = h_ref.shape[0] // ROW_PITCH

    def slot(r, k):
        return start_ref[e_ref[r * TOP_K + k]] + rank_ref[r * TOP_K + k]

    def issue(r, _):
        for k in range(TOP_K):
            _row_copy(h_ref, r, xs_ref, slot(r, k), sem).start(priority=k % 2)
        return 0

    def drain(r, _):
        for k in range(TOP_K):
            _row_copy(h_ref, r, xs_ref, slot(r, k), sem).wait()
        return 0

    lax.fori_loop(0, tm, issue, 0)

    @pl.when(pl.program_id(0) == 0)
    def _():
        zero_ref[...] = jnp.zeros(zero_ref.shape, F32)
        zrows = zero_ref.shape[0] // ROW_PITCH
        for phase in ("start", "wait"):
            def per_expert(e, _):
                def per_row(s, _):
                    cp = _row_copy(zero_ref, 0, xs_ref, s, zsem)
                    cp.start() if phase == "start" else cp.wait()
                    return 0
                return lax.fori_loop(start_ref[e] + cnt_ref[e], pend_ref[e], per_row, 0)
            lax.fori_loop(0, N_EXPERTS, per_expert, 0)

            def per_chunk(s, _):
                s0 = pl.multiple_of(s * (zrows * ROW_PITCH), zrows * ROW_PITCH)
                cp = pltpu.make_async_copy(zero_ref, xs_ref.at[pl.ds(s0, zrows * ROW_PITCH), :], zsem)
                cp.start() if phase == "start" else cp.wait()
                return 0
            lax.fori_loop(pend_ref[N_EXPERTS - 1] // zrows, xs_ref.shape[0] // (zrows * ROW_PITCH), per_chunk, 0)

    lax.fori_loop(0, tm, drain, 0)


def _dispatch(h_flat, e_flat, rank_flat, pad_start, counts, pad_end, n_slots, tm=MOE_ROUTE_TILE):
    t = h_flat.shape[0] // ROW_PITCH
    smem = pl.BlockSpec((tm * TOP_K,), lambda i, *_: (i,), memory_space=pltpu.SMEM)
    grid_spec = pltpu.PrefetchScalarGridSpec(
        num_scalar_prefetch=3,
        grid=(t // tm,),
        in_specs=[smem, smem, pl.BlockSpec((tm * ROW_PITCH, LANES), lambda i, *_: (i, 0))],
        out_specs=pl.BlockSpec(memory_space=pl.ANY),
        scratch_shapes=[pltpu.VMEM((MOE_TILE // 8 * ROW_PITCH, LANES), F32), pltpu.SemaphoreType.DMA(()),
                        pltpu.SemaphoreType.DMA(())],
    )
    return pl.pallas_call(
        _dispatch_body,
        grid_spec=grid_spec,
        out_shape=jax.ShapeDtypeStruct((n_slots * ROW_PITCH, LANES), F32),
        compiler_params=pltpu.CompilerParams(dimension_semantics=("arbitrary",), vmem_limit_bytes=VMEM_LIMIT,
                                             has_side_effects=True),
        name="moe_dispatch",
    )(pad_start, counts, pad_end, e_flat, rank_flat, h_flat)


def _experts_body(te_ref, nu_ref, rows_ref, xs_ref, wg_ref, wu_ref, bg_ref, bu_ref, wd_ref, bd_ref, y_ref,
                  xb_ref, wgb_ref, wub_ref, wdb_ref):
    del te_ref, nu_ref
    i = pl.program_id(0)
    c = pl.program_id(1)
    n_rows = rows_ref[i]

    n_chunks = xb_ref.shape[1] // LANES

    @pl.when((n_rows > 0) & (c == 0))
    def _():
        for cc in range(n_chunks):
            xb_ref[:, cc * LANES:(cc + 1) * LANES] = xs_ref[_flat_idx(0, MOE_TILE, cc)].astype(BF16)

    @pl.when(n_rows > 0)
    def _():
        wgb_ref[...] = wg_ref[...].astype(BF16)
        wub_ref[...] = wu_ref[...].astype(BF16)
        wdb_ref[...] = wd_ref[...].astype(BF16)

    for sb in range(MOE_TILE // MOE_SUB):
        @pl.when(n_rows > sb * MOE_SUB)
        def _():
            x = xb_ref[pl.ds(sb * MOE_SUB, MOE_SUB), :]
            g = jnp.minimum(_dot(x, wgb_ref[...]) + bg_ref[...], SWIGLU_LIMIT)
            u = jnp.clip(_dot(x, wub_ref[...]) + bu_ref[...], -SWIGLU_LIMIT, SWIGLU_LIMIT)
            a = (u + 1.0) * g * _sigmoid(g * SWIGLU_ALPHA)
            part = _dot(a.astype(BF16), wdb_ref[...])

            @pl.when(c == 0)
            def _():
                first = part + bd_ref[...]
                for cc in range(n_chunks):
                    y_ref[_flat_idx(sb * MOE_SUB, MOE_SUB, cc)] = first[:, cc * LANES:(cc + 1) * LANES]
                for cc in range(n_chunks, ROW_PITCH):
                    y_ref[_flat_idx(sb * MOE_SUB, MOE_SUB, cc)] = jnp.zeros((MOE_SUB, LANES), F32)

            @pl.when(c > 0)
            def _():
                for cc in range(n_chunks):
                    idx = _flat_idx(sb * MOE_SUB, MOE_SUB, cc)
                    y_ref[idx] = y_ref[idx] + part[:, cc * LANES:(cc + 1) * LANES]

        @pl.when((n_rows <= sb * MOE_SUB) & (c == 0))
        def _():
            y_ref[pl.ds(sb * MOE_SUB * ROW_PITCH, MOE_SUB * ROW_PITCH), :] = jnp.zeros(
                (MOE_SUB * ROW_PITCH, LANES), F32)


def _experts(xs, tile_e, n_used, tile_rows, w_gu, b_gu, w_dn, b_dn):
    n_slots = xs.shape[0] // ROW_PITCH
    d = w_dn.shape[2]
    n_tiles = n_slots // MOE_TILE
    flat_tile = (MOE_TILE * ROW_PITCH, LANES)
    n_ff = D_FF // MOE_FF_TILE
    tf = MOE_FF_TILE

    def last_used(i, nu):
        return jnp.minimum(i, nu[0] - 1)

    def chunk(i, c, nu):
        return jnp.where(i < nu[0], c, n_ff - 1)

    grid_spec = pltpu.PrefetchScalarGridSpec(
        num_scalar_prefetch=3,
        grid=(n_tiles, n_ff),
        in_specs=[
            pl.BlockSpec(flat_tile, lambda i, c, te, nu, nr: (last_used(i, nu), 0)),
            pl.BlockSpec((None, d, tf), lambda i, c, te, nu, nr: (te[i], 0, chunk(i, c, nu))),
            pl.BlockSpec((None, d, tf), lambda i, c, te, nu, nr: (te[i], 0, n_ff + chunk(i, c, nu))),
            pl.BlockSpec((None, 1, tf), lambda i, c, te, nu, nr: (te[i], 0, chunk(i, c, nu))),
            pl.BlockSpec((None, 1, tf), lambda i, c, te, nu, nr: (te[i], 0, n_ff + chunk(i, c, nu))),
            pl.BlockSpec((None, tf, d), lambda i, c, te, nu, nr: (te[i], chunk(i, c, nu), 0)),
            pl.BlockSpec((None, 1, d), lambda i, c, te, nu, nr: (te[i], 0, 0)),
        ],
        out_specs=pl.BlockSpec(flat_tile, lambda i, c, te, nu, nr: (i, 0)),
        scratch_shapes=[pltpu.VMEM((MOE_TILE, d), BF16), pltpu.VMEM((d, tf), BF16), pltpu.VMEM((d, tf), BF16),
                        pltpu.VMEM((tf, d), BF16)],
    )
    return pl.pallas_call(
        _experts_body,
        grid_spec=grid_spec,
        out_shape=jax.ShapeDtypeStruct((n_slots * ROW_PITCH, LANES), F32),
        compiler_params=_cparams("arbitrary", "arbitrary"),
        name="moe_experts",
    )(tile_e, n_used, tile_rows, xs, w_gu, w_gu, b_gu, b_gu, w_dn, b_dn)


def _combine_body(start_ref, e_ref, rank_ref, x1_ref, route_ref, y_ref, op_ref, os_ref, buf_ref, sem, *, tiles_p):
    tm = x1_ref.shape[0]

    def slot(r, k):
        return start_ref[e_ref[r * TOP_K + k]] + rank_ref[r * TOP_K + k]

    def issue(r, _):
        for k in range(TOP_K):
            _row_copy(y_ref, slot(r, k), buf_ref.at[k], r, sem).start(priority=k % 2)
        return 0

    def drain(r, _):
        for k in range(TOP_K):
            _row_copy(y_ref, slot(r, k), buf_ref.at[k], r, sem).wait()
        return 0

    lax.fori_loop(0, tm, issue, 0)
    lax.fori_loop(0, tm, drain, 0)
    route = route_ref[...]
    gate = [route[:, TOP_K + k:TOP_K + k + 1] for k in range(TOP_K)]

    def write(o_ref):
        for cc in range(o_ref.shape[1] // LANES):
            sl = slice(cc * LANES, (cc + 1) * LANES)
            out = x1_ref[:, sl]
            for k in range(TOP_K):
                out = out + gate[k] * buf_ref.at[k][_flat_idx(0, tm, cc)]
            o_ref[:, sl] = out

    pl.when(pl.program_id(0) < tiles_p)(lambda: write(op_ref))
    pl.when(pl.program_id(0) >= tiles_p)(lambda: write(os_ref))


def _combine(x1, route, y, e_flat, rank_flat, pad_start, n_prompt, tm=MOE_ROUTE_TILE):
    t, d = x1.shape
    tiles_p = n_prompt // tm
    smem = pl.BlockSpec((tm * TOP_K,), lambda i, *_: (i,), memory_space=pltpu.SMEM)
    grid_spec = pltpu.PrefetchScalarGridSpec(
        num_scalar_prefetch=1,
        grid=(t // tm,),
        in_specs=[smem, smem, pl.BlockSpec((tm, d), lambda i, *_: (i, 0)),
                  pl.BlockSpec((tm, LANES), lambda i, *_: (i, 0)),
                  pl.BlockSpec(memory_space=pl.ANY)],
        out_specs=[pl.BlockSpec((tm, d), lambda i, *_: (jnp.minimum(i, tiles_p - 1), 0)),
                   pl.BlockSpec((tm, d), lambda i, *_: (jnp.maximum(i - tiles_p, 0), 0))],
        scratch_shapes=[pltpu.VMEM((TOP_K, tm * ROW_PITCH, LANES), F32), pltpu.SemaphoreType.DMA(())],
    )
    return pl.pallas_call(
        functools.partial(_combine_body, tiles_p=tiles_p),
        grid_spec=grid_spec,
        out_shape=[jax.ShapeDtypeStruct((n_prompt, d), F32), jax.ShapeDtypeStruct((t - n_prompt, d), F32)],
        compiler_params=_cparams("arbitrary"),
        name="moe_combine",
    )(pad_start, e_flat, rank_flat, x1, route, y)


def _moe(x1, h, route, counts, w_gu, b_gu, w_dn, b_dn, n_prompt):
    t = x1.shape[0]
    n_tiles = -(-t * TOP_K // MOE_TILE) + N_EXPERTS
    n_slots = n_tiles * MOE_TILE
    cnt = counts[0, :N_EXPERTS].astype(jnp.int32)
    padded = (cnt + MOE_TILE - 1) // MOE_TILE * MOE_TILE
    pad_end = jnp.cumsum(padded)
    pad_start = pad_end - padded
    n_used = (pad_end[-1] // MOE_TILE).reshape(1)
    tile_first = jnp.minimum(jnp.arange(n_tiles, dtype=jnp.int32), n_used[0] - 1) * MOE_TILE
    tile_e = jnp.minimum(jnp.searchsorted(pad_end, tile_first, side="right"), N_EXPERTS - 1).astype(jnp.int32)
    tile_idx = jnp.arange(n_tiles, dtype=jnp.int32)
    tile_rows = jnp.clip(pad_start[tile_e] + cnt[tile_e] - tile_idx * MOE_TILE, 0, MOE_TILE)
    tile_rows = jnp.where(tile_idx < n_used[0], tile_rows, 0).astype(jnp.int32)
    e_flat = route[:, :TOP_K].astype(jnp.int32).reshape(-1)
    rank_flat = route[:, 2 * TOP_K:3 * TOP_K].astype(jnp.int32).reshape(-1)
    xs = _dispatch(h, e_flat, rank_flat, pad_start, cnt, pad_end, n_slots)
    y = _experts(xs, tile_e, n_used, tile_rows, w_gu, b_gu.reshape(N_EXPERTS, 1, -1), w_dn,
                 b_dn.reshape(N_EXPERTS, 1, -1))
    return _combine(x1, route, y, e_flat, rank_flat, pad_start, n_prompt)


def _layer(xp, xs, pools, kwbuf, vwbuf, s_ret, page_table, p):
    batch, seq, d = xp.shape
    dec_batch, t_new, _ = xs.shape
    n_pages = page_table.shape[1]
    past = n_pages * PAGE_SIZE
    n_prompt = batch * seq
    n_sample = dec_batch * t_new
    assert seq & (seq - 1) == 0 and t_new & (t_new - 1) == 0 and seq % SEL_KV_TILE == 0
    assert n_prompt % ROW_TILE == 0 and n_sample % ROW_TILE == 0 and t_new % 8 == 0

    xp2, xs2 = xp.reshape(n_prompt, d), xs.reshape(n_sample, d)
    t_all = n_prompt + n_sample
    xn = _ln1(xp2, xs2, p["ln1_g"])

    w_in = p["w_in"]
    offs = [0]
    for n in (NSA_Q, KV_W, KV_W, KV_W, KV_W, KV_W, KV_W, 3 * NSA_HEADS, RET_QK, RET_QK, RET_V, RET_V, D_MODEL, D_MODEL):
        offs.append(offs[-1] + n)
    w_q = w_in[:, offs[0]:offs[1]].astype(BF16)
    w_kv = jnp.concatenate([w_in[:, offs[1]:offs[8]],
                            jnp.zeros((d, LANES - 3 * NSA_HEADS), F32)], axis=1).astype(BF16)
    w_rot = w_in[:, offs[8]:offs[10]].astype(BF16)
    w_act = w_in[:, offs[10]:offs[14]].astype(BF16)

    def rows(n):
        return pl.BlockSpec((ROW_TILE, n), lambda j, i: (i, 0))

    q = _proj_call(_proj_q_body, xn, w_q, [p["q_norm_g"].reshape(1, HEAD_DIM)], [_vec_spec(HEAD_DIM)],
                   jax.ShapeDtypeStruct((t_all, NSA_Q), BF16), rows(NSA_Q), NSA_Q, "proj_q")
    row2 = jax.ShapeDtypeStruct((NSA_GROUPS * t_all, HEAD_DIM), F32)
    row2_spec = pl.BlockSpec((NSA_GROUPS * ROW_TILE, HEAD_DIM), lambda j, i: (i, 0))
    *kv_rows, kvb, gate = _proj_call(
        _proj_kv_body, xn, w_kv,
        [p["k_sel_norm_g"].reshape(1, HEAD_DIM), p["k_win_norm_g"].reshape(1, HEAD_DIM)],
        [_vec_spec(HEAD_DIM), _vec_spec(HEAD_DIM)],
        [row2] * 6 + [jax.ShapeDtypeStruct((t_all, 4 * KV_W), BF16), jax.ShapeDtypeStruct((t_all, LANES), F32)],
        [row2_spec] * 6 + [rows(4 * KV_W), rows(LANES)], 6 * KV_W + LANES, "proj_kv")
    inv = 1.0 / (ROPE_BASE ** jnp.linspace(0.0, 1.0, RET_DK // 2, dtype=F32))
    rot = _proj_call(
        functools.partial(_proj_rot_body, n_prompt=n_prompt, seq=seq, past=past, dec_seq=t_new),
        xn, w_rot, [jnp.repeat(inv, 2).reshape(1, RET_DK)], [_vec_spec(RET_DK)],
        jax.ShapeDtypeStruct((t_all, 2 * RET_QK), BF16),
        pl.BlockSpec((ROW_TILE, RET_QK), lambda j, i: (i, j)), RET_QK, "proj_rot")
    act_tn = 1024

    def proj_act(w, act, name):
        return _proj_call(functools.partial(_proj_act_body, act=act), xn, w, [], [],
                          jax.ShapeDtypeStruct((t_all, w.shape[1]), BF16),
                          pl.BlockSpec((ROW_TILE, act_tn), lambda j, i: (i, j)), act_tn, name)

    rv = proj_act(w_act[:, :RET_V], "none", "proj_rv")
    rg = proj_act(w_act[:, RET_V:2 * RET_V], "silu", "proj_rg")
    merge_gates = proj_act(w_act[:, 2 * RET_V:], "sigmoid", "proj_gates")

    cmp_w = (_compress_weights(p["cmp_k_w1"], p["cmp_k_b1"], p["cmp_k_w2"])
             + _compress_weights(p["cmp_v_w1"], p["cmp_v_b1"], p["cmp_v_w2"])
             + (p["k_cmp_norm_g"].reshape(1, HEAD_DIM),))
    gn = p["ret_norm_g"].reshape(1, RET_V)

    kc_p, vc_p = _compress_prompt(kv_rows[0], kv_rows[1], cmp_w, batch, seq)
    o_nsa_p = _nsa_prompt(q, gate, kc_p, vc_p, kvb, batch, seq)
    o_ret_p, ret_p = _ret_prompt(rot, rv, rg, gn, batch, seq)

    pool_kc, pool_vc, pool_ks, pool_vs = [
        a.reshape(a.shape[0], NSA_GROUPS * PAGE_SIZE, HEAD_DIM) for a in pools]
    pt_flat = page_table.reshape(-1)
    kc_s, vc_s = _compress_sample(pool_kc, pool_vc, pt_flat, cmp_w, dec_batch, n_pages)
    wb = kwbuf.shape[1]
    o_nsa_s, kw_s, vw_s = _nsa_sample(
        q, gate, kv_rows[2:], n_prompt // (2 * t_new), kc_s, vc_s, pool_ks, pool_vs,
        kwbuf.reshape(dec_batch, wb * NSA_GROUPS, HEAD_DIM), vwbuf.reshape(dec_batch, wb * NSA_GROUPS, HEAD_DIM),
        pt_flat, dec_batch, t_new, n_pages)
    rot_s = rot[n_prompt:].astype(F32)
    o_ret_s, ret_s = _ret_sample(rot_s[:, :RET_QK], rot_s[:, RET_QK:], rv[n_prompt:].astype(F32),
                                 rg[n_prompt:].astype(F32),
                                 gn, s_ret, dec_batch, t_new)

    mixed = _merge(o_nsa_p, o_nsa_s, o_ret_p, o_ret_s, p["w_nsa_br"].astype(BF16), p["w_ret_br"].astype(BF16),
                   merge_gates)
    rw = jnp.concatenate([p["router_w"], jnp.zeros((d, LANES - N_EXPERTS), F32)], axis=1)
    rwh, rwl = _split_bf16(rw)
    rb = jnp.concatenate([p["router_b"], jnp.zeros((LANES - N_EXPERTS,), F32)]).reshape(1, LANES)
    x1, h, route, counts = _out_router(xp2, xs2, mixed, p["w_out"].astype(BF16), p["ln2_g"].reshape(1, d),
                                       rwh, rwl, rb)
    y_p, y_s = _moe(x1, h, route, counts, p["w_gate_up"], p["b_gate_up"], p["w_down"], p["b_down"], n_prompt)

    kv5 = (NSA_GROUPS, HEAD_DIM)
    wbp = min(WINDOW, seq)
    rows_p = [a[:NSA_GROUPS * n_prompt].reshape(batch, seq, *kv5) for a in kv_rows]
    rows_s = [a[NSA_GROUPS * n_prompt:].reshape(dec_batch, t_new, *kv5) for a in kv_rows[:4]]
    states_p = (*rows_p[:4], rows_p[4][:, seq - wbp:], rows_p[5][:, seq - wbp:], ret_p)
    states_s = (*rows_s, kw_s.reshape(dec_batch, wb, *kv5), vw_s.reshape(dec_batch, wb, *kv5), ret_s)
    return y_p.reshape(batch, seq, d), y_s.reshape(dec_batch, t_new, d), states_p, states_s


def kernel(x_prompt, x_sample, cache_k_cmp, cache_v_cmp, cache_k_sel, cache_v_sel, state_k_win, state_v_win,
           state_ret, page_table, ln1_g, w_in, cmp_k_w1, cmp_k_b1, cmp_k_w2, cmp_v_w1, cmp_v_b1, cmp_v_w2,
           q_norm_g, k_cmp_norm_g, k_sel_norm_g, k_win_norm_g, ret_norm_g, w_nsa_br, w_ret_br, w_out, ln2_g,
           router_w, router_b, w_gate_up, b_gate_up, w_down, b_down):
    depth = w_in.shape[0]
    y_p, y_s = x_prompt, x_sample
    new_p, new_s = [], []
    for l in range(depth):
        p = {
            "ln1_g": ln1_g[l], "w_in": w_in[l],
            "cmp_k_w1": cmp_k_w1[l], "cmp_k_b1": cmp_k_b1[l], "cmp_k_w2": cmp_k_w2[l],
            "cmp_v_w1": cmp_v_w1[l], "cmp_v_b1": cmp_v_b1[l], "cmp_v_w2": cmp_v_w2[l],
            "q_norm_g": q_norm_g[l], "k_cmp_norm_g": k_cmp_norm_g[l], "k_sel_norm_g": k_sel_norm_g[l],
            "k_win_norm_g": k_win_norm_g[l], "ret_norm_g": ret_norm_g[l],
            "w_nsa_br": w_nsa_br[l], "w_ret_br": w_ret_br[l], "w_out": w_out[l], "ln2_g": ln2_g[l],
            "router_w": router_w[l], "router_b": router_b[l], "w_gate_up": w_gate_up[l],
            "b_gate_up": b_gate_up[l], "w_down": w_down[l], "b_down": b_down[l],
        }
        pools = (cache_k_cmp[l], cache_v_cmp[l], cache_k_sel[l], cache_v_sel[l])
        y_p, y_s, sp, ss = _layer(y_p, y_s, pools, state_k_win[l], state_v_win[l], state_ret[l], page_table, p)
        new_p.append(sp)
        new_s.append(ss)
    outs_p = [jnp.stack(a) for a in zip(*new_p)]
    outs_s = [jnp.stack(a) for a in zip(*new_s)]
    return (y_p, y_s, *outs_p, *outs_s)
```
